```python
import jax, jax.numpy as jnp
from jax import lax
import numpy as np

D_MODEL = 1024
BATCH = 16
SEQ = 2048
DEPTH = 4

N_MIXERS = 4
HEAD_DIM = 64
N_HEADS = D_MODEL // HEAD_DIM
SWA_KV_HEADS = 4
SWA_WINDOW = 128
SWA_BLOCK = 128
NSA_KV_GROUPS = 2
NSA_CMP_LEN = 32
NSA_CMP_STRIDE = 16
NSA_SEL_LEN = 64
NSA_TOPK = 8
NSA_WINDOW = 512
NSA_QBLOCK = 64
NSA_FORCE_BONUS = 1.0e4
GLA_HEADS = 4
GLA_DK = D_MODEL // 2
GLA_DV = D_MODEL
GLA_GATE_RANK = 16
GLA_TAU = 16.0
GLA_CHUNK = 64
POOL_WINDOWS = (2, 4, 8, 16)
POOL_GROUP = D_MODEL // 4
MEM_LEN = 256
XATTN_HEADS = 4
FFN_DIM = 2816
DN_ALPHA = (2 * DEPTH) ** 0.25
DN_BETA = (8 * DEPTH) ** -0.25
LN_EPS = 1e-5
NEG_INF = -1e30

SWA_IN = N_HEADS * HEAD_DIM + 2 * SWA_KV_HEADS * HEAD_DIM
NSA_IN = N_HEADS * HEAD_DIM + 6 * NSA_KV_GROUPS * HEAD_DIM + 3 * N_HEADS
GLA_IN = 2 * GLA_DK + 2 * GLA_DV + GLA_GATE_RANK

kernel_name = 'hybrid_interleaved_deepnorm_macaron_block'


def layer_norm(x, g, b):
    xf = x.astype(jnp.float32)
    mu = jnp.mean(xf, -1, keepdims=True)
    var = jnp.mean(jnp.square(xf - mu), -1, keepdims=True)
    return ((xf - mu) * lax.rsqrt(var + LN_EPS) * g + b).astype(x.dtype)


def alibi_slopes(n):
    return jnp.exp2(-8.0 * jnp.arange(1, n + 1, dtype=jnp.float32) / n)


def swiglu(x, w_gate, w_up, w_down):
    return (jax.nn.silu(x @ w_gate) * (x @ w_up)) @ w_down


def swa_mixer(x, w_in, sinks, w_out):
    B, S, _ = x.shape
    H, G, d, L = N_HEADS, SWA_KV_HEADS, HEAD_DIM, SWA_BLOCK
    R = H // G
    nb = S // L
    q, k, v = jnp.split(x @ w_in, [H * d, H * d + G * d], axis=-1)
    q = q.reshape(B, nb, L, G, R, d) * d ** -0.5
    k = k.reshape(B, nb, L, G, d)
    v = v.reshape(B, nb, L, G, d)
    prev = lambda t: jnp.concatenate([jnp.zeros_like(t[:, :1]), t[:, :-1]], axis=1)
    kb = jnp.concatenate([prev(k), k], axis=2)
    vb = jnp.concatenate([prev(v), v], axis=2)
    q_abs = jnp.arange(nb)[:, None] * L + jnp.arange(L)[None, :]
    k_abs = jnp.arange(nb)[:, None] * L - L + jnp.arange(2 * L)[None, :]
    dist = q_abs[:, :, None] - k_abs[:, None, :]
    valid = (dist >= 0) & (dist < SWA_WINDOW) & (k_abs[:, None, :] >= 0)
    slopes = alibi_slopes(H).reshape(1, G, R, 1, 1)
    s = jnp.einsum('bnqgrd,bnkgd->bngrqk', q, kb).astype(jnp.float32)
    s = s - slopes * dist[:, None, None].astype(jnp.float32)
    s = jnp.where(valid[:, None, None], s, NEG_INF)
    sink = sinks.astype(jnp.float32).reshape(1, 1, G, R, 1, 1)
    m = jnp.maximum(jnp.max(s, -1, keepdims=True), sink)
    e = jnp.exp(s - m)
    p = e / (jnp.sum(e, -1, keepdims=True) + jnp.exp(sink - m))
    o = jnp.einsum('bngrqk,bnkgd->bnqgrd', p.astype(v.dtype), vb)
    return o.reshape(B, S, H * d) @ w_out


def nsa_mixer(x, w_in, cmp_pos, cmp_w1, cmp_w2, w_out):
    B, S, _ = x.shape
    H, G, d = N_HEADS, NSA_KV_GROUPS, HEAD_DIM
    R = H // G
    L, STR, SL, W, QB = NSA_CMP_LEN, NSA_CMP_STRIDE, NSA_SEL_LEN, NSA_WINDOW, NSA_QBLOCK
    n_cmp = S // STR - 1
    n_sel = S // SL
    topk = min(NSA_TOPK, n_sel)
    f32 = jnp.float32
    splits = [H * d + i * G * d for i in range(7)]
    q, kc, vc, ks, vs, kw, vw, gl = jnp.split(x @ w_in, splits, axis=-1)
    q = q.reshape(B, S, G, R, d) * d ** -0.5
    kv = lambda t: t.reshape(B, S, G, d)
    slopes = alibi_slopes(H).reshape(G, R)
    t_pos = jnp.arange(S)

    def compress(t, pos, w1, w2):
        c = t.reshape(B, S // STR, STR, G, d)
        blocks = jnp.concatenate([c[:, :-1], c[:, 1:]], axis=2) + pos[:, None, :]
        return jax.nn.gelu(jnp.einsum('bnlgd,lde->bnge', blocks, w1)) @ w2

    k_cmp = compress(kv(kc), cmp_pos[0], cmp_w1[0], cmp_w2[0])
    v_cmp = compress(kv(vc), cmp_pos[1], cmp_w1[1], cmp_w2[1])
    blk = jnp.arange(n_cmp) * STR
    dist_c = (t_pos[:, None] - (blk[None, :] + (L - 1) / 2)).astype(f32)
    valid_c = (blk[None, :] + L - 1) <= t_pos[:, None]
    s_c = jnp.einsum('bsgrd,bngd->bgrsn', q, k_cmp).astype(f32) - slopes[:, :, None, None] * dist_c
    p_c = jnp.where(valid_c, jax.nn.softmax(jnp.where(valid_c, s_c, NEG_INF), axis=-1), 0.0)
    o_cmp = jnp.einsum('bgrsn,bngd->bsgrd', p_c.astype(x.dtype), v_cmp)

    sel_start = jnp.arange(n_sel) * SL
    overlap = ((blk[:, None] < sel_start[None, :] + SL) & (blk[:, None] + L > sel_start[None, :])).astype(f32)
    imp = jnp.einsum('bgrsn,nm->bgsm', p_c, overlap)
    cur = t_pos // SL
    j = jnp.arange(n_sel)
    causal_sel = j[None, :] <= cur[:, None]
    forced = (j[None, :] == 0) | (j[None, :] == cur[:, None]) | (j[None, :] == cur[:, None] - 1)
    score = jnp.where(causal_sel, imp + jnp.where(forced, NSA_FORCE_BONUS, 0.0), NEG_INF)
    top_score, sel_idx = lax.top_k(score, topk)
    sel_ok = top_score > 0.5 * NEG_INF

    ks_blocks = kv(ks).reshape(B, n_sel, SL, G, d).transpose(0, 3, 1, 2, 4)
    vs_blocks = kv(vs).reshape(B, n_sel, SL, G, d).transpose(0, 3, 1, 2, 4)
    kw_pad = jnp.pad(kv(kw), ((0, 0), (W, 0), (0, 0), (0, 0)))
    vw_pad = jnp.pad(kv(vw), ((0, 0), (W, 0), (0, 0), (0, 0)))
    b_ix = jnp.arange(B)[:, None, None, None]
    g_ix = jnp.arange(G)[None, :, None, None]

    def query_block(c):
        q0 = c * QB
        qc = lax.dynamic_slice_in_dim(q, q0, QB, axis=1)
        tq = q0 + jnp.arange(QB)
        idx = lax.dynamic_slice_in_dim(sel_idx, q0, QB, axis=2)
        ok = lax.dynamic_slice_in_dim(sel_ok, q0, QB, axis=2)
        k_sel = ks_blocks[b_ix, g_ix, idx]
        v_sel = vs_blocks[b_ix, g_ix, idx]
        kpos = idx[..., None] * SL + jnp.arange(SL)
        dsel = tq[None, None, :, None, None] - kpos
        msel = ok[..., None] & (dsel >= 0)
        s = jnp.einsum('bqgrd,bgqkld->bgrqkl', qc, k_sel).astype(f32)
        s = s - slopes[None, :, :, None, None, None] * dsel[:, :, None].astype(f32)
        s = jnp.where(msel[:, :, None], s, NEG_INF).reshape(B, G, R, QB, topk * SL)
        p = jax.nn.softmax(s, axis=-1).reshape(B, G, R, QB, topk, SL).astype(x.dtype)
        o_sel = jnp.einsum('bgrqkl,bgqkld->bqgrd', p, v_sel)
        kwin = lax.dynamic_slice_in_dim(kw_pad, q0, W + QB, axis=1)
        vwin = lax.dynamic_slice_in_dim(vw_pad, q0, W + QB, axis=1)
        wpos = q0 - W + jnp.arange(W + QB)
        dw = tq[:, None] - wpos[None, :]
        mw = (dw >= 0) & (dw < W) & (wpos[None, :] >= 0)
        sw = jnp.einsum('bqgrd,bkgd->bgrqk', qc, kwin).astype(f32) - slopes[:, :, None, None] * dw.astype(f32)
        pw = jax.nn.softmax(jnp.where(mw, sw, NEG_INF), axis=-1).astype(x.dtype)
        o_win = jnp.einsum('bgrqk,bkgd->bqgrd', pw, vwin)
        return o_sel, o_win

    o_sel, o_win = lax.map(query_block, jnp.arange(S // QB))
    o_sel = jnp.moveaxis(o_sel, 0, 1).reshape(B, S, G, R, d)
    o_win = jnp.moveaxis(o_win, 0, 1).reshape(B, S, G, R, d)
    g = jax.nn.sigmoid(gl.astype(f32)).reshape(B, S, 3, G, R)[..., None]
    o = g[:, :, 0] * o_cmp + g[:, :, 1] * o_sel + g[:, :, 2] * o_win
    return o.astype(x.dtype).reshape(B, S, H * d) @ w_out


def gla_mixer(x, w_in, w_gate2, b_gate, norm_g, w_out):
    B, S, _ = x.shape
    H, C = GLA_HEADS, GLA_CHUNK
    dk, dv = GLA_DK // H, GLA_DV // H
    nc = S // C
    f32 = jnp.float32
    q, k, v, g_lr, r = jnp.split(
        x @ w_in, [GLA_DK, 2 * GLA_DK, 2 * GLA_DK + GLA_DV, 2 * GLA_DK + GLA_DV + GLA_GATE_RANK], axis=-1)
    log_a = jax.nn.log_sigmoid((g_lr @ w_gate2 + b_gate).astype(f32)) / GLA_TAU

    def chunks(t, hd):
        return t.astype(f32).reshape(B, nc, C, H, hd).transpose(1, 0, 3, 2, 4)

    qc = chunks(q, dk) * dk ** -0.5
    kc = chunks(k, dk)
    vc = chunks(v, dv)
    bc = jnp.cumsum(chunks(log_a, dk), axis=3)
    causal = jnp.tril(jnp.ones((C, C), dtype=bool))

    def step(state, inp):
        q_, k_, v_, b_ = inp
        decay = jnp.exp(jnp.where(causal[:, :, None], b_[:, :, :, None, :] - b_[:, :, None, :, :], NEG_INF))
        att = jnp.einsum('bhik,bhjk,bhijk->bhij', q_, k_, decay)
        out = jnp.einsum('bhij,bhjv->bhiv', att, v_) + jnp.einsum('bhik,bhkv->bhiv', q_ * jnp.exp(b_), state)
        b_last = b_[:, :, -1, :]
        state = jnp.exp(b_last)[..., None] * state + jnp.einsum(
            'bhjk,bhjv->bhkv', k_ * jnp.exp(b_last[:, :, None, :] - b_), v_)
        return state, out

    state0 = jnp.zeros((B, H, dk, dv), f32)
    _, o = lax.scan(step, state0, (qc, kc, vc, bc))
    o = o.transpose(1, 0, 3, 2, 4).reshape(B, S, H, dv)
    o = o * lax.rsqrt(jnp.mean(jnp.square(o), -1, keepdims=True) + LN_EPS) * norm_g
    o = o.reshape(B, S, GLA_DV) * jax.nn.silu(r.astype(f32))
    return o.astype(x.dtype) @ w_out


def pool_mixer(x, w_grp, scale):
    B, S, D = x.shape
    ng = len(POOL_WINDOWS)
    xg = x.astype(jnp.float32).reshape(B, S, ng, POOL_GROUP)
    csum = jnp.pad(jnp.cumsum(xg, axis=1), ((0, 0), (1, 0), (0, 0), (0, 0)))
    t = jnp.arange(S)
    pooled = []
    for gi, w in enumerate(POOL_WINDOWS):
        lo = jnp.maximum(t + 1 - w, 0)
        cg = csum[:, :, gi]
        total = cg[:, 1:] - cg[:, lo]
        pooled.append(total / (t + 1 - lo).astype(jnp.float32)[:, None])
    pooled = jnp.stack(pooled, axis=2)
    y = jnp.einsum('bsgc,gce->bsge', pooled - xg, w_grp.astype(jnp.float32)).reshape(B, S, D) * scale
    return y.astype(x.dtype)


def mem_cross_attention(x, mem, w_q, w_kv, w_o):
    B, S, D = x.shape
    M = mem.shape[1]
    Hx = XATTN_HEADS
    dh = D // Hx
    q = (x @ w_q).reshape(B, S, Hx, dh) * dh ** -0.5
    k, v = jnp.split(mem @ w_kv, 2, axis=-1)
    k = k.reshape(B, M, Hx, dh)
    v = v.reshape(B, M, Hx, dh)
    s = jnp.einsum('bshd,bmhd->bhsm', q, k).astype(jnp.float32)
    p = jax.nn.softmax(s, axis=-1).astype(v.dtype)
    o = jnp.einsum('bhsm,bmhd->bshd', p, v).reshape(B, S, D)
    return o @ w_o


def setup_inputs(seed: int = 0) -> dict:
    key = jax.random.key(seed)
    keys = jax.random.split(key, 28)
    D, F, d, H = D_MODEL, FFN_DIM, HEAD_DIM, N_HEADS
    nA, nB, nC, nD = [len(range(k, DEPTH, N_MIXERS)) for k in range(N_MIXERS)]
    L = NSA_CMP_LEN

    def nrm(i, shape, scale):
        return scale * jax.random.normal(keys[i], shape, jnp.float32)

    return {
        'x': nrm(0, (BATCH, SEQ, D), 1.0),
        'mem': nrm(1, (BATCH, MEM_LEN, D), 1.0),
        'ln_g': jnp.ones((DEPTH, 4, D), jnp.float32) + nrm(2, (DEPTH, 4, D), 0.02),
        'ln_b': nrm(3, (DEPTH, 4, D), 0.02),
        'ffn1_w_gate': nrm(4, (DEPTH, D, F), D ** -0.5),
        'ffn1_w_up': nrm(5, (DEPTH, D, F), D ** -0.5),
        'ffn1_w_down': nrm(6, (DEPTH, F, D), DN_BETA * F ** -0.5),
        'ffn2_w_gate': nrm(7, (DEPTH, D, F), D ** -0.5),
        'ffn2_w_up': nrm(8, (DEPTH, D, F), D ** -0.5),
        'ffn2_w_down': nrm(9, (DEPTH, F, D), DN_BETA * F ** -0.5),
        'xattn_w_q': nrm(10, (DEPTH, D, D), D ** -0.5),
        'xattn_w_kv': nrm(11, (DEPTH, D, 2 * D), D ** -0.5),
        'xattn_w_o': nrm(12, (DEPTH, D, D), DN_BETA * D ** -0.5),
        'swa_w_in': nrm(13, (nA, D, SWA_IN), D ** -0.5),
        'swa_sinks': nrm(14, (nA, H), 1.0),
        'swa_w_out': nrm(15, (nA, H * d, D), DN_BETA * D ** -0.5),
        'nsa_w_in': nrm(16, (nB, D, NSA_IN), D ** -0.5),
        'nsa_cmp_pos': nrm(17, (nB, 2, L, d), 0.1),
        'nsa_cmp_w1': nrm(18, (nB, 2, L, d, d), (L * d) ** -0.5),
        'nsa_cmp_w2': nrm(19, (nB, 2, d, d), d ** -0.5),
        'nsa_w_out': nrm(20, (nB, H * d, D), DN_BETA * D ** -0.5),
        'gla_w_in': nrm(21, (nC, D, GLA_IN), D ** -0.5),
        'gla_w_gate2': nrm(22, (nC, GLA_GATE_RANK, GLA_DK), GLA_GATE_RANK ** -0.5),
        'gla_b_gate': nrm(23, (nC, GLA_DK), 0.1),
        'gla_norm_g': jnp.ones((nC, GLA_DV // GLA_HEADS), jnp.float32) + nrm(24, (nC, GLA_DV // GLA_HEADS), 0.02),
        'gla_w_out': nrm(25, (nC, GLA_DV, D), DN_BETA * GLA_DV ** -0.5),
        'pool_w': nrm(26, (nD, len(POOL_WINDOWS), POOL_GROUP, POOL_GROUP), DN_BETA * POOL_GROUP ** -0.5),
        'pool_scale': jnp.ones((nD, D), jnp.float32) + nrm(27, (nD, D), 0.02),
    }


def reference(x, mem, ln_g, ln_b,
              ffn1_w_gate, ffn1_w_up, ffn1_w_down, ffn2_w_gate, ffn2_w_up, ffn2_w_down,
              xattn_w_q, xattn_w_kv, xattn_w_o,
              swa_w_in, swa_sinks, swa_w_out,
              nsa_w_in, nsa_cmp_pos, nsa_cmp_w1, nsa_cmp_w2, nsa_w_out,
              gla_w_in, gla_w_gate2, gla_b_gate, gla_norm_g, gla_w_out,
              pool_w, pool_scale):
    for i in range(DEPTH):
        kind = i % N_MIXERS
        j = i // N_MIXERS
        x = layer_norm(DN_ALPHA * x + 0.5 * swiglu(x, ffn1_w_gate[i], ffn1_w_up[i], ffn1_w_down[i]),
                       ln_g[i, 0], ln_b[i, 0])
        if kind == 0:
            y = swa_mixer(x, swa_w_in[j], swa_sinks[j], swa_w_out[j])
        elif kind == 1:
            y = nsa_mixer(x, nsa_w_in[j], nsa_cmp_pos[j], nsa_cmp_w1[j], nsa_cmp_w2[j], nsa_w_out[j])
        elif kind == 2:
            y = gla_mixer(x, gla_w_in[j], gla_w_gate2[j], gla_b_gate[j], gla_norm_g[j], gla_w_out[j])
        else:
            y = pool_mixer(x, pool_w[j], pool_scale[j])
        x = layer_norm(DN_ALPHA * x + y, ln_g[i, 1], ln_b[i, 1])
        x = layer_norm(DN_ALPHA * x + mem_cross_attention(x, mem, xattn_w_q[i], xattn_w_kv[i], xattn_w_o[i]),
                       ln_g[i, 2], ln_b[i, 2])
        x = layer_norm(DN_ALPHA * x + 0.5 * swiglu(x, ffn2_w_gate[i], ffn2_w_up[i], ffn2_w_down[i]),
                       ln_g[i, 3], ln_b[i, 3])
    return x
```

```python
import functools

import numpy as np
import jax
import jax.numpy as jnp
from jax import lax
from jax.experimental import pallas as pl
from jax.experimental.pallas import tpu as pltpu

F32 = jnp.float32
BF16 = jnp.bfloat16

D_MODEL = 1024
DEPTH = 4
N_MIXERS = 4
HEAD_DIM = 64
N_HEADS = D_MODEL // HEAD_DIM
SWA_KV_HEADS = 4
SWA_WINDOW = 128
NSA_KV_GROUPS = 2
NSA_CMP_LEN = 32
NSA_CMP_STRIDE = 16
NSA_SEL_LEN = 64
NSA_TOPK = 8
NSA_WINDOW = 512
NSA_FORCE_BONUS = 1.0e4
GLA_HEADS = 4
GLA_DK = D_MODEL // 2
GLA_DV = D_MODEL
GLA_GATE_RANK = 16
GLA_TAU = 16.0
GLA_CHUNK = 64
POOL_WINDOWS = (2, 4, 8, 16)
POOL_GROUP = D_MODEL // 4
XATTN_HEADS = 4
DN_ALPHA = (2 * DEPTH) ** 0.25
LN_EPS = 1e-5
NEG_INF = -1e30

LANES = 128
V7X_VMEM_LIMIT_BYTES = 56 * 1024 * 1024

_NT = (((1,), (1,)), ((), ()))
_TN = (((0,), (0,)), ((), ()))


def _cparams(*sem):
    return pltpu.CompilerParams(dimension_semantics=sem, vmem_limit_bytes=V7X_VMEM_LIMIT_BYTES)


def _dot(a, b):
    return jnp.dot(a, b, preferred_element_type=F32)


def _dot_nt(a, b):
    return lax.dot_general(a, b, _NT, preferred_element_type=F32)


def _dot_tn(a, b):
    return lax.dot_general(a, b, _TN, preferred_element_type=F32)


def _layer_norm(y, g, b):
    mu = jnp.mean(y, axis=-1, keepdims=True)
    yc = y - mu
    var = jnp.mean(yc * yc, axis=-1, keepdims=True)
    return yc * lax.rsqrt(var + LN_EPS) * g + b


def _alibi_slope(h, n):
    return float(2.0 ** (-8.0 * (h + 1) / n))


def _full(shape):
    nd = len(shape)
    return pl.BlockSpec(shape, lambda *_: (0,) * nd)


FFN_TM = 512
FFN_TF = 256


def _ffn_body(x_ref, wg_ref, wu_ref, wd_ref, g_ref, b_ref, o_ref, acc_ref, *, nf):
    x = x_ref[...]
    xb = x.astype(BF16)
    for c in range(nf):
        gate = _dot(xb, wg_ref[c])
        up = _dot(xb, wu_ref[c])
        h = (gate * jax.nn.sigmoid(gate) * up).astype(BF16)
        d = _dot(h, wd_ref[c])
        if c == 0:
            acc_ref[...] = d
        else:
            acc_ref[...] += d
    y = DN_ALPHA * x + 0.5 * acc_ref[...]
    o_ref[...] = _layer_norm(y, g_ref[...], b_ref[...])


def _ffn_ln(x2, w_gate, w_up, w_down, ln_g, ln_b):
    n, d = x2.shape
    f = w_gate.shape[1]
    nf = f // FFN_TF
    wg = w_gate.astype(BF16).reshape(d, nf, FFN_TF).transpose(1, 0, 2)
    wu = w_up.astype(BF16).reshape(d, nf, FFN_TF).transpose(1, 0, 2)
    wd = w_down.astype(BF16).reshape(nf, FFN_TF, d)
    tm = min(FFN_TM, n)
    return pl.pallas_call(
        functools.partial(_ffn_body, nf=nf),
        grid=(n // tm,),
        in_specs=[
            pl.BlockSpec((tm, d), lambda i: (i, 0)),
            _full((nf, d, FFN_TF)),
            _full((nf, d, FFN_TF)),
            _full((nf, FFN_TF, d)),
            _full((1, d)),
            _full((1, d)),
        ],
        out_specs=pl.BlockSpec((tm, d), lambda i: (i, 0)),
        out_shape=jax.ShapeDtypeStruct((n, d), F32),
        scratch_shapes=[pltpu.VMEM((tm, d), F32)],
        compiler_params=_cparams("parallel"),
        name="ffn_ln",
    )(x2, wg, wu, wd, ln_g.reshape(1, d), ln_b.reshape(1, d))


PROJ_TM = 512


def _proj_body(x_ref, *refs, scales):
    n = len(scales)
    xb = x_ref[...].astype(BF16)
    for w_ref, o_ref, s in zip(refs[:n], refs[n:], scales):
        r = _dot(xb, w_ref[...])
        if s != 1.0:
            r = r * s
        o_ref[...] = r.astype(o_ref.dtype)


def _proj(x2, weights, dtypes, scales=None):
    n, k = x2.shape
    scales = tuple(scales) if scales is not None else (1.0,) * len(weights)
    tm = min(PROJ_TM, n)
    in_specs = [pl.BlockSpec((tm, k), lambda i: (i, 0))] + [_full(w.shape) for w in weights]
    out_specs = [pl.BlockSpec((tm, w.shape[1]), lambda i: (i, 0)) for w in weights]
    out_shape = [jax.ShapeDtypeStruct((n, w.shape[1]), dt) for w, dt in zip(weights, dtypes)]
    return pl.pallas_call(
        functools.partial(_proj_body, scales=scales),
        grid=(n // tm,),
        in_specs=in_specs,
        out_specs=out_specs,
        out_shape=out_shape,
        compiler_params=_cparams("parallel"),
        name="proj",
    )(x2, *weights)


def _pad_cols(w, m):
    return jnp.pad(w, ((0, 0), (0, m - w.shape[1])))


def _oproj_body(o_ref, w_ref, x_ref, g_ref, b_ref, out_ref):
    y = _dot(o_ref[...], w_ref[...])
    out_ref[...] = _layer_norm(DN_ALPHA * x_ref[...] + y, g_ref[...], b_ref[...])


def _oproj_ln(o2, w_out, x2, ln_g, ln_b):
    n, k = o2.shape
    d = x2.shape[1]
    tm = min(PROJ_TM, n)
    return pl.pallas_call(
        _oproj_body,
        grid=(n // tm,),
        in_specs=[
            pl.BlockSpec((tm, k), lambda i: (i, 0)),
            _full((k, d)),
            pl.BlockSpec((tm, d), lambda i: (i, 0)),
            _full((1, d)),
            _full((1, d)),
        ],
        out_specs=pl.BlockSpec((tm, d), lambda i: (i, 0)),
        out_shape=jax.ShapeDtypeStruct((n, d), F32),
        compiler_params=_cparams("parallel"),
        name="oproj_ln",
    )(o2, w_out.astype(BF16), x2, ln_g.reshape(1, d), ln_b.reshape(1, d))


XATTN_TQ = 512


def _xattn_body(x_ref, kv_ref, wq_ref, wo_ref, g_ref, b_ref, o_ref, *, heads):
    x = x_ref[0]
    d = x.shape[-1]
    dh = d // heads
    q = (_dot(x.astype(BF16), wq_ref[...]) * dh ** -0.5).astype(BF16)
    outs = []
    for h in range(heads):
        qh = q[:, h * dh:(h + 1) * dh]
        kh = kv_ref[0, :, h * dh:(h + 1) * dh]
        vh = kv_ref[0, :, d + h * dh:d + (h + 1) * dh]
        s = _dot_nt(qh, kh)
        m = jnp.max(s, axis=-1, keepdims=True)
        e = jnp.exp(s - m)
        p = e / jnp.sum(e, axis=-1, keepdims=True)
        outs.append(_dot(p.astype(BF16), vh).astype(BF16))
    o = jnp.concatenate(outs, axis=-1)
    y = _dot(o, wo_ref[...])
    o_ref[0] = _layer_norm(DN_ALPHA * x + y, g_ref[...], b_ref[...])


def _xattn_ln(x, mem, w_q, w_kv, w_o, ln_g, ln_b):
    b, s, d = x.shape
    m = mem.shape[1]
    (kv,) = _proj(mem.reshape(b * m, d), [w_kv.astype(BF16)], [BF16])
    kv = kv.reshape(b, m, 2 * d)
    tq = min(XATTN_TQ, s)
    return pl.pallas_call(
        functools.partial(_xattn_body, heads=XATTN_HEADS),
        grid=(b, s // tq),
        in_specs=[
            pl.BlockSpec((1, tq, d), lambda i, j: (i, j, 0)),
            pl.BlockSpec((1, m, 2 * d), lambda i, j: (i, 0, 0)),
            _full((d, d)),
            _full((d, d)),
            _full((1, d)),
            _full((1, d)),
        ],
        out_specs=pl.BlockSpec((1, tq, d), lambda i, j: (i, j, 0)),
        out_shape=jax.ShapeDtypeStruct((b, s, d), F32),
        compiler_params=_cparams("parallel", "parallel"),
        name="xattn_ln",
    )(x, kv, w_q.astype(BF16), w_o.astype(BF16), ln_g.reshape(1, d), ln_b.reshape(1, d))


def _swa_body(sink_ref, q_ref, kp_ref, kc_ref, vp_ref, vc_ref, o_ref):
    blk = SWA_WINDOW
    d = HEAD_DIM
    rep = N_HEADS // SWA_KV_HEADS
    n = pl.program_id(1)
    qi = lax.broadcasted_iota(jnp.int32, (blk, 2 * blk), 0)
    ki = lax.broadcasted_iota(jnp.int32, (blk, 2 * blk), 1)
    dist = qi + blk - ki
    valid = (dist >= 0) & (dist < SWA_WINDOW) & ((ki >= blk) | (n > 0))
    distf = dist.astype(F32)
    for g in range(SWA_KV_HEADS):
        cs = slice(g * d, (g + 1) * d)
        kg = jnp.concatenate([kp_ref[0, :, cs], kc_ref[0, :, cs]], axis=0)
        vg = jnp.concatenate([vp_ref[0, :, cs], vc_ref[0, :, cs]], axis=0)
        for r in range(rep):
            h = g * rep + r
            qh = q_ref[0, :, h * d:(h + 1) * d]
            s = _dot_nt(qh, kg)
            s = jnp.where(valid, s - _alibi_slope(h, N_HEADS) * distf, NEG_INF)
            sink = sink_ref[h]
            m = jnp.maximum(jnp.max(s, axis=-1, keepdims=True), sink)
            e = jnp.exp(s - m)
            p = e / (jnp.sum(e, axis=-1, keepdims=True) + jnp.exp(sink - m))
            o_ref[0, :, h * d:(h + 1) * d] = _dot(p.astype(BF16), vg).astype(o_ref.dtype)


def _swa_mixer_ln(x, w_in, sinks, w_out, ln_g, ln_b):
    b, s, d = x.shape
    hd = N_HEADS * HEAD_DIM
    gd = SWA_KV_HEADS * HEAD_DIM
    blk = SWA_WINDOW
    x2 = x.reshape(b * s, d)
    w = w_in.astype(BF16)
    q, k, v = _proj(x2, [w[:, :hd], w[:, hd:hd + gd], w[:, hd + gd:]], [BF16] * 3,
                    scales=(HEAD_DIM ** -0.5, 1.0, 1.0))
    q = q.reshape(b, s, hd)
    k = k.reshape(b, s, gd)
    v = v.reshape(b, s, gd)
    prev = lambda i, j: (i, jnp.maximum(j - 1, 0), 0)
    cur = lambda i, j: (i, j, 0)
    o = pl.pallas_call(
        _swa_body,
        grid=(b, s // blk),
        in_specs=[
            pl.BlockSpec(memory_space=pltpu.SMEM),
            pl.BlockSpec((1, blk, hd), cur),
            pl.BlockSpec((1, blk, gd), prev),
            pl.BlockSpec((1, blk, gd), cur),
            pl.BlockSpec((1, blk, gd), prev),
            pl.BlockSpec((1, blk, gd), cur),
        ],
        out_specs=pl.BlockSpec((1, blk, hd), cur),
        out_shape=jax.ShapeDtypeStruct((b, s, hd), BF16),
        compiler_params=_cparams("parallel", "parallel"),
        name="swa_core",
    )(sinks.astype(F32), q, k, k, v, v)
    return _oproj_ln(o.reshape(b * s, hd), w_out, x2, ln_g, ln_b).reshape(b, s, d)


POOL_TM = 512
POOL_HALO = 16


def _pool_body(x_ref, halo_ref, w_ref, sc_ref, g_ref, b_ref, o_ref, ext_ref, *, tm):
    t = pl.program_id(1)
    x = x_ref[0]
    ext_ref[0:POOL_HALO, :] = jnp.where(t > 0, halo_ref[0], 0.0)
    ext_ref[POOL_HALO:, :] = x
    row = t * tm + lax.broadcasted_iota(jnp.int32, (tm, 1), 0)
    ys = []
    for gi, w in enumerate(POOL_WINDOWS):
        cs = slice(gi * POOL_GROUP, (gi + 1) * POOL_GROUP)
        xg = x[:, cs]
        acc = xg
        for k in range(1, w):
            acc = acc + ext_ref[POOL_HALO - k:POOL_HALO - k + tm, cs]
        cnt = jnp.minimum(row + 1, w).astype(F32)
        diff = (acc / cnt - xg).astype(BF16)
        ys.append(_dot(diff, w_ref[gi]))
    y = jnp.concatenate(ys, axis=-1) * sc_ref[...]
    o_ref[0] = _layer_norm(DN_ALPHA * x + y, g_ref[...], b_ref[...])


def _pool_mixer_ln(x, w_grp, scale, ln_g, ln_b):
    b, s, d = x.shape
    tm = min(POOL_TM, s)
    hb = tm // POOL_HALO
    ng = len(POOL_WINDOWS)
    return pl.pallas_call(
        functools.partial(_pool_body, tm=tm),
        grid=(b, s // tm),
        in_specs=[
            pl.BlockSpec((1, tm, d), lambda i, j: (i, j, 0)),
            pl.BlockSpec((1, POOL_HALO, d), lambda i, j: (i, jnp.maximum(j * hb - 1, 0), 0)),
            _full((ng, POOL_GROUP, POOL_GROUP)),
            _full((1, d)),
            _full((1, d)),
            _full((1, d)),
        ],
        out_specs=pl.BlockSpec((1, tm, d), lambda i, j: (i, j, 0)),
        out_shape=jax.ShapeDtypeStruct((b, s, d), F32),
        scratch_shapes=[pltpu.VMEM((tm + POOL_HALO, d), F32)],
        compiler_params=_cparams("parallel", "parallel"),
        name="pool_ln",
    )(x, x, w_grp.astype(BF16), scale.reshape(1, d), ln_g.reshape(1, d), ln_b.reshape(1, d))


def _gla_tables(c):
    levels = []
    s = c
    while s >= 1:
        levels.append(s)
        s //= 2
    rows, masks = [], []
    idx = np.arange(c)
    for s in levels:
        dq = np.zeros((c, c), np.float32)
        dk = np.zeros((c, c), np.float32)
        for i in range(c):
            blk = i // s
            if s == c or blk % 2 == 1:
                dq[i, blk * s:i + 1] = 1.0
            dk[i, i + 1:blk * s + s] = 1.0
        rows.append(dq)
        if s > 1:
            rows.append(dk)
        if s < c:
            masks.append(((idx[:, None] // (2 * s) == idx[None, :] // (2 * s))
                          & ((idx[:, None] // s) % 2 == 1) & ((idx[None, :] // s) % 2 == 0)))
    masks.append(np.eye(c, dtype=bool))
    return np.concatenate(rows, 0), np.stack(masks).astype(np.float32), len(levels)


def _gla_body(q_ref, k_ref, v_ref, glr_ref, r_ref, wg2_ref, bg_ref, ng_ref, dall_ref, mask_ref,
              o_ref, st_ref, *, nlev):
    c = GLA_CHUNK
    heads = GLA_HEADS
    dk = GLA_DK // heads
    dv = GLA_DV // heads

    @pl.when(pl.program_id(1) == 0)
    def _():
        st_ref[...] = jnp.zeros_like(st_ref)

    z = _dot(glr_ref[0].astype(BF16), wg2_ref[...]) + bg_ref[...]
    log_a = (jnp.minimum(z, 0.0) - jnp.log1p(jnp.exp(-jnp.abs(z)))) * (1.0 / GLA_TAU)
    hi = log_a.astype(BF16)
    lo = (log_a - hi.astype(F32)).astype(BF16)
    e2 = _dot(dall_ref[...], jnp.concatenate([hi, lo], axis=-1))
    decay = jnp.exp(e2[:, :GLA_DK] + e2[:, GLA_DK:])
    q = q_ref[0]
    k = k_ref[0]
    outs = []
    for h in range(heads):
        ks = slice(h * dk, (h + 1) * dk)
        qh = q[:, ks]
        kh = k[:, ks]
        vh = v_ref[0, :, h * dv:(h + 1) * dv]
        tbl = lambda i: decay[i * c:(i + 1) * c, ks]
        q_in = (qh * tbl(0)).astype(BF16)
        k_out = (kh * tbl(1)).astype(BF16)
        khb = kh.astype(BF16)
        att = mask_ref[nlev - 1] * _dot_nt(qh.astype(BF16), khb)
        for li in range(nlev - 1):
            ql = (qh * tbl(2 + 2 * li)).astype(BF16)
            kl = (kh * tbl(3 + 2 * li)).astype(BF16) if li < nlev - 2 else khb
            att = att + mask_ref[li] * _dot_nt(ql, kl)
        st = st_ref[h]
        out = _dot(att.astype(BF16), vh) + _dot_nt(q_in, st.astype(BF16))
        total = decay[c - 1:c, ks]
        st_ref[h] = st * total + _dot_tn(vh, k_out)
        out = out * lax.rsqrt(jnp.mean(out * out, axis=-1, keepdims=True) + LN_EPS) * ng_ref[...]
        outs.append(out)
    r = r_ref[0]
    o = jnp.concatenate(outs, axis=-1) * (r * jax.nn.sigmoid(r))
    o_ref[0] = o.astype(o_ref.dtype)


def _gla_mixer_ln(x, w_in, w_gate2, b_gate, norm_g, w_out, ln_g, ln_b):
    b, s, d = x.shape
    c = GLA_CHUNK
    heads = GLA_HEADS
    dk = GLA_DK // heads
    dv = GLA_DV // heads
    x2 = x.reshape(b * s, d)
    w = w_in.astype(BF16)
    o0, o1, o2, o3 = GLA_DK, 2 * GLA_DK, 2 * GLA_DK + GLA_DV, 2 * GLA_DK + GLA_DV + GLA_GATE_RANK
    q, k, v, glr, r = _proj(
        x2, [w[:, :o0], w[:, o0:o1], w[:, o1:o2], _pad_cols(w[:, o2:o3], LANES), w[:, o3:]],
        [F32, F32, BF16, F32, F32], scales=(dk ** -0.5, 1.0, 1.0, 1.0, 1.0))
    wg2 = jnp.pad(w_gate2.astype(BF16), ((0, LANES - GLA_GATE_RANK), (0, 0)))
    dall, masks, nlev = _gla_tables(c)
    tok = lambda width: pl.BlockSpec((1, c, width), lambda i, j: (i, j, 0))
    o = pl.pallas_call(
        functools.partial(_gla_body, nlev=nlev),
        grid=(b, s // c),
        in_specs=[
            tok(GLA_DK), tok(GLA_DK), tok(GLA_DV), tok(LANES), tok(GLA_DV),
            _full((LANES, GLA_DK)),
            _full((1, GLA_DK)),
            _full((1, dv)),
            _full(dall.shape),
            _full(masks.shape),
        ],
        out_specs=tok(GLA_DV),
        out_shape=jax.ShapeDtypeStruct((b, s, GLA_DV), BF16),
        scratch_shapes=[pltpu.VMEM((heads, dv, dk), F32)],
        compiler_params=_cparams("parallel", "arbitrary"),
        name="gla_core",
    )(q.reshape(b, s, -1), k.reshape(b, s, -1), v.reshape(b, s, -1), glr.reshape(b, s, -1),
      r.reshape(b, s, -1), wg2, b_gate.reshape(1, -1), norm_g.reshape(1, -1),
      jnp.asarray(dall, BF16), jnp.asarray(masks, F32))
    return _oproj_ln(o.reshape(b * s, GLA_DV), w_out, x2, ln_g, ln_b).reshape(b, s, d)


NSA_TQ = 128
NSA_TK = 128


def _gelu_tanh(x):
    return 0.5 * x * (1.0 + jnp.tanh(np.sqrt(2.0 / np.pi).astype(np.float32) * (x + 0.044715 * (x * x * x))))


def _cmp_body(c_ref, plo_ref, phi_ref, w1a_ref, w1b_ref, w2_ref, o_ref):
    c = c_ref[0]
    a = _dot((c + plo_ref[...]).astype(BF16), w1a_ref[...])
    bm = _dot((c + phi_ref[...]).astype(BF16), w1b_ref[...])
    nxt = jnp.concatenate([bm[1:], jnp.zeros_like(bm[:1])], axis=0)
    row = lax.broadcasted_iota(jnp.int32, a.shape, 0)
    pre = jnp.where(row < a.shape[0] - 1, a + nxt, 0.0)
    o_ref[0] = _dot(_gelu_tanh(pre).astype(BF16), w2_ref[...]).astype(o_ref.dtype)


def _nsa_compress(kvc, cmp_pos, cmp_w1, cmp_w2):
    b, s, _ = kvc.shape
    st, g, d = NSA_CMP_STRIDE, NSA_KV_GROUPS, HEAD_DIM
    nch = s // st
    width = 2 * g * d
    c = kvc.reshape(b, nch, st * width)
    eye = jnp.eye(2 * g, dtype=F32)
    sel = jnp.repeat(jnp.eye(2, dtype=F32), g, axis=1)

    def expand(w1_half):
        wj = jnp.einsum('jlde,jc->lcde', w1_half, sel)
        return jnp.einsum('lcde,cf->lcdfe', wj, eye).reshape(st * width, width).astype(BF16)

    w1a = expand(cmp_w1[:, :st])
    w1b = expand(cmp_w1[:, st:])
    w2 = jnp.einsum('cde,cf->cdfe', jnp.einsum('jde,jc->cde', cmp_w2, sel), eye).reshape(width, width).astype(BF16)

    def pos_row(p_half):
        return jnp.einsum('jld,jc->lcd', p_half, sel).reshape(1, st * width)

    return pl.pallas_call(
        _cmp_body,
        grid=(b,),
        in_specs=[
            pl.BlockSpec((1, nch, st * width), lambda i: (i, 0, 0)),
            _full((1, st * width)),
            _full((1, st * width)),
            _full((st * width, width)),
            _full((st * width, width)),
            _full((width, width)),
        ],
        out_specs=pl.BlockSpec((1, nch, width), lambda i: (i, 0, 0)),
        out_shape=jax.ShapeDtypeStruct((b, nch, width), BF16),
        compiler_params=_cparams("parallel"),
        name="nsa_compress",
    )(c, pos_row(cmp_pos[:, :st]), pos_row(cmp_pos[:, st:]), w1a, w1b, w2)


def _nsa_overlap_t(s):
    n_cmp_pad = s // NSA_CMP_STRIDE
    n_sel = s // NSA_SEL_LEN
    blk = np.arange(n_cmp_pad) * NSA_CMP_STRIDE
    sel_start = np.arange(n_sel) * NSA_SEL_LEN
    ov = (blk[None, :] < sel_start[:, None] + NSA_SEL_LEN) & (blk[None, :] + NSA_CMP_LEN > sel_start[:, None])
    return ov.astype(np.float32)


def _flash_sweep(qs, k_ref, v_ref, kcols, vcols, lo, hi, mask_fn, slopes, tq, tk):
    nh = len(slopes)
    d = qs.shape[-1]

    def body(kt, carry):
        m_prev, l_prev, acc_prev = carry
        k0 = pl.multiple_of(kt * tk, tk)
        kk = k_ref[0, pl.ds(k0, tk), kcols]
        vv = v_ref[0, pl.ds(k0, tk), vcols]
        s = _dot_nt(qs, kk)
        valid, dist = mask_fn(kt)
        ps, ms, ls, alphas = [], [], [], []
        for r in range(nh):
            rs = slice(r * tq, (r + 1) * tq)
            sr = jnp.where(valid, s[rs] - slopes[r] * dist, NEG_INF)
            m_new = jnp.maximum(m_prev[rs], jnp.max(sr, axis=-1, keepdims=True))
            e = jnp.where(valid, jnp.exp(sr - m_new), 0.0)
            alpha = jnp.exp(m_prev[rs] - m_new)
            ls.append(alpha * l_prev[rs] + jnp.sum(e, axis=-1, keepdims=True))
            ms.append(m_new)
            alphas.append(alpha)
            ps.append(e.astype(BF16))
        pv = _dot(jnp.concatenate(ps, axis=0), vv)
        acc = jnp.concatenate(alphas, axis=0) * acc_prev + pv
        return jnp.concatenate(ms, axis=0), jnp.concatenate(ls, axis=0), acc

    init = (jnp.full((nh * tq, 1), NEG_INF, F32), jnp.zeros((nh * tq, 1), F32), jnp.zeros((nh * tq, d), F32))
    _, l_fin, acc = lax.fori_loop(lo, hi, body, init)
    return acc / l_fin


def _nsa_body(q_ref, cmp_ref, kv_ref, gl_ref, ovt_ref, o_ref, *, tq, tk):
    d = HEAD_DIM
    groups = NSA_KV_GROUPS
    rep = N_HEADS // groups
    n_cmp_pad = cmp_ref.shape[1]
    n_sel = ovt_ref.shape[0]
    t_idx = pl.program_id(1)
    q0 = t_idx * tq
    gates = jax.nn.sigmoid(gl_ref[0])

    tc = q0 + lax.broadcasted_iota(jnp.int32, (tq, n_cmp_pad), 0)
    nc = lax.broadcasted_iota(jnp.int32, (tq, n_cmp_pad), 1)
    valid_c = nc * NSA_CMP_STRIDE + (NSA_CMP_LEN - 1) <= tc
    dist_c = tc.astype(F32) - (nc.astype(F32) * NSA_CMP_STRIDE + (NSA_CMP_LEN - 1) / 2)
    jm = lax.broadcasted_iota(jnp.int32, (n_sel, tq), 0)
    cur = (q0 + lax.broadcasted_iota(jnp.int32, (n_sel, tq), 1)) // NSA_SEL_LEN
    causal_sel = jm <= cur
    forced = (jm == 0) | (jm == cur) | (jm == cur - 1)
    tqk = q0 + lax.broadcasted_iota(jnp.int32, (tq, tk), 0)
    kcol = lax.broadcasted_iota(jnp.int32, (tq, tk), 1)
    em = lax.broadcasted_iota(jnp.int32, (n_sel, tk), 0)
    ek = lax.broadcasted_iota(jnp.int32, (n_sel, tk), 1)

    for g in range(groups):
        slopes = [_alibi_slope(g * rep + r, N_HEADS) for r in range(rep)]
        qs = jnp.concatenate([q_ref[0, :, (g * rep + r) * d:(g * rep + r + 1) * d] for r in range(rep)], axis=0)
        k_cmp = cmp_ref[0, :, g * d:(g + 1) * d]
        v_cmp = cmp_ref[0, :, (groups + g) * d:(groups + g + 1) * d]

        s_c = _dot_nt(qs, k_cmp)
        p_sum = jnp.zeros((tq, n_cmp_pad), F32)
        ps = []
        for r in range(rep):
            sr = jnp.where(valid_c, s_c[r * tq:(r + 1) * tq] - slopes[r] * dist_c, NEG_INF)
            m = jnp.max(sr, axis=-1, keepdims=True)
            e = jnp.where(valid_c, jnp.exp(sr - m), 0.0)
            den = jnp.sum(e, axis=-1, keepdims=True)
            p = e * jnp.where(den > 0.0, 1.0 / den, 0.0)
            p_sum = p_sum + p
            ps.append(p.astype(BF16))
        o_cmp = _dot(jnp.concatenate(ps, axis=0), v_cmp)

        p_hi = p_sum.astype(BF16)
        p_lo = (p_sum - p_hi.astype(F32)).astype(BF16)
        ovt = ovt_ref[...]
        imp_t = _dot_nt(ovt, p_hi) + _dot_nt(ovt, p_lo)
        score = jnp.where(causal_sel, imp_t + jnp.where(forced, NSA_FORCE_BONUS, 0.0), NEG_INF)
        rank = jnp.zeros((n_sel, tq), F32)
        for mp in range(n_sel):
            row = score[mp:mp + 1, :]
            ahead = (row > score) | ((row == score) & (mp < jm))
            rank = rank + jnp.where(ahead, 1.0, 0.0)
        sel_t = jnp.where((rank < NSA_TOPK) & causal_sel, 1.0, 0.0).astype(BF16)

        def sel_mask(kt):
            kabs = kt * tk + kcol
            expand = jnp.where(em == (kt * tk + ek) // NSA_SEL_LEN, 1.0, 0.0).astype(BF16)
            member = _dot_tn(sel_t, expand)
            dist = tqk - kabs
            return (member > 0.5) & (dist >= 0), dist.astype(F32)

        ks_cols = slice(g * d, (g + 1) * d)
        vs_cols = slice((groups + g) * d, (groups + g + 1) * d)
        o_sel = _flash_sweep(qs, kv_ref, kv_ref, ks_cols, vs_cols, 0, (q0 + tq) // tk, sel_mask, slopes, tq, tk)

        def win_mask(kt):
            dist = tqk - (kt * tk + kcol)
            return (dist >= 0) & (dist < NSA_WINDOW), dist.astype(F32)

        kw_cols = slice((2 * groups + g) * d, (2 * groups + g + 1) * d)
        vw_cols = slice((3 * groups + g) * d, (3 * groups + g + 1) * d)
        lo = jnp.maximum((q0 - NSA_WINDOW) // tk, 0)
        o_win = _flash_sweep(qs, kv_ref, kv_ref, kw_cols, vw_cols, lo, (q0 + tq) // tk, win_mask, slopes, tq, tk)

        for r in range(rep):
            h = g * rep + r
            rs = slice(r * tq, (r + 1) * tq)
            o = (gates[:, h:h + 1] * o_cmp[rs] + gates[:, N_HEADS + h:N_HEADS + h + 1] * o_sel[rs]
                 + gates[:, 2 * N_HEADS + h:2 * N_HEADS + h + 1] * o_win[rs])
            o_ref[0, :, h * d:(h + 1) * d] = o.astype(o_ref.dtype)


def _nsa_mixer_ln(x, w_in, cmp_pos, cmp_w1, cmp_w2, w_out, ln_g, ln_b):
    b, s, d = x.shape
    hd = N_HEADS * HEAD_DIM
    gd = NSA_KV_GROUPS * HEAD_DIM
    x2 = x.reshape(b * s, d)
    w = w_in.astype(BF16)
    q, kvc, kv4, gl = _proj(
        x2, [w[:, :hd], w[:, hd:hd + 2 * gd], w[:, hd + 2 * gd:hd + 6 * gd], _pad_cols(w[:, hd + 6 * gd:], LANES)],
        [BF16, F32, BF16, F32], scales=(HEAD_DIM ** -0.5, 1.0, 1.0, 1.0))
    cmp = _nsa_compress(kvc.reshape(b, s, 2 * gd), cmp_pos, cmp_w1, cmp_w2)
    ovt = _nsa_overlap_t(s)
    tq, tk = NSA_TQ, NSA_TK
    o = pl.pallas_call(
        functools.partial(_nsa_body, tq=tq, tk=tk),
        grid=(b, s // tq),
        in_specs=[
            pl.BlockSpec((1, tq, hd), lambda i, j: (i, j, 0)),
            pl.BlockSpec((1,) + cmp.shape[1:], lambda i, j: (i, 0, 0)),
            pl.BlockSpec((1, s, 4 * gd), lambda i, j: (i, 0, 0)),
            pl.BlockSpec((1, tq, LANES), lambda i, j: (i, j, 0)),
            _full(ovt.shape),
        ],
        out_specs=pl.BlockSpec((1, tq, hd), lambda i, j: (i, j, 0)),
        out_shape=jax.ShapeDtypeStruct((b, s, hd), BF16),
        compiler_params=_cparams("parallel", "parallel"),
        name="nsa_core",
    )(q.reshape(b, s, hd), cmp, kv4.reshape(b, s, 4 * gd), gl.reshape(b, s, LANES), jnp.asarray(ovt, BF16))
    return _oproj_ln(o.reshape(b * s, hd), w_out, x2, ln_g, ln_b).reshape(b, s, d)


def kernel(x, mem, ln_g, ln_b, ffn1_w_gate, ffn1_w_up, ffn1_w_down, ffn2_w_gate, ffn2_w_up, ffn2_w_down,
           xattn_w_q, xattn_w_kv, xattn_w_o, swa_w_in, swa_sinks, swa_w_out, nsa_w_in, nsa_cmp_pos,
           nsa_cmp_w1, nsa_cmp_w2, nsa_w_out, gla_w_in, gla_w_gate2, gla_b_gate, gla_norm_g, gla_w_out,
           pool_w, pool_scale):
    b, s, d = x.shape
    for i in range(DEPTH):
        kind = i % N_MIXERS
        j = i // N_MIXERS
        x = _ffn_ln(x.reshape(b * s, d), ffn1_w_gate[i], ffn1_w_up[i], ffn1_w_down[i],
                    ln_g[i, 0], ln_b[i, 0]).reshape(b, s, d)
        if kind == 0:
            x = _swa_mixer_ln(x, swa_w_in[j], swa_sinks[j], swa_w_out[j], ln_g[i, 1], ln_b[i, 1])
        elif kind == 1:
            x = _nsa_mixer_ln(x, nsa_w_in[j], nsa_cmp_pos[j], nsa_cmp_w1[j], nsa_cmp_w2[j], nsa_w_out[j],
                              ln_g[i, 1], ln_b[i, 1])
        elif kind == 2:
            x = _gla_mixer_ln(x, gla_w_in[j], gla_w_gate2[j], gla_b_gate[j], gla_norm_g[j], gla_w_out[j],
                              ln_g[i, 1], ln_b[i, 1])
        else:
            x = _pool_mixer_ln(x, pool_w[j], pool_scale[j], ln_g[i, 1], ln_b[i, 1])
        x = _xattn_ln(x, mem, xattn_w_q[i], xattn_w_kv[i], xattn_w_o[i], ln_g[i, 2], ln_b[i, 2])
        x = _ffn_ln(x.reshape(b * s, d), ffn2_w_gate[i], ffn2_w_up[i], ffn2_w_down[i],
                    ln_g[i, 3], ln_b[i, 3]).reshape(b, s, d)
    return x
```

```python
import functools

import numpy as np
import jax
import jax.numpy as jnp
from jax import lax
from jax.experimental import pallas as pl
from jax.experimental.pallas import tpu as pltpu

F32 = jnp.float32
BF16 = jnp.bfloat16

D_MODEL = 1024
DEPTH = 4
N_MIXERS = 4
HEAD_DIM = 64
N_HEADS = D_MODEL // HEAD_DIM
SWA_KV_HEADS = 4
SWA_WINDOW = 128
NSA_KV_GROUPS = 2
NSA_CMP_LEN = 32
NSA_CMP_STRIDE = 16
NSA_SEL_LEN = 64
NSA_TOPK = 8
NSA_WINDOW = 512
NSA_FORCE_BONUS = 1.0e4
GLA_HEADS = 4
GLA_DK = D_MODEL // 2
GLA_DV = D_MODEL
GLA_GATE_RANK = 16
GLA_TAU = 16.0
GLA_CHUNK = 64
POOL_WINDOWS = (2, 4, 8, 16)
POOL_GROUP = D_MODEL // 4
XATTN_HEADS = 4
DN_ALPHA = (2 * DEPTH) ** 0.25
LN_EPS = 1e-5
NEG_INF = -1e30

LANES = 128
V7X_VMEM_LIMIT_BYTES = 56 * 1024 * 1024

_NT = (((1,), (1,)), ((), ()))
_TN = (((0,), (0,)), ((), ()))


def _cparams(*sem):
    return pltpu.CompilerParams(dimension_semantics=sem, vmem_limit_bytes=V7X_VMEM_LIMIT_BYTES)


def _dot(a, b):
    return jnp.dot(a, b, preferred_element_type=F32)


def _dot_nt(a, b):
    return lax.dot_general(a, b, _NT, preferred_element_type=F32)


def _dot_tn(a, b):
    return lax.dot_general(a, b, _TN, preferred_element_type=F32)


def _layer_norm(y, g, b):
    mu = jnp.mean(y, axis=-1, keepdims=True)
    yc = y - mu
    var = jnp.mean(yc * yc, axis=-1, keepdims=True)
    return yc * lax.rsqrt(var + LN_EPS) * g + b


def _alibi_slope(h, n):
    return float(2.0 ** (-8.0 * (h + 1) / n))


def _full(shape):
    nd = len(shape)
    return pl.BlockSpec(shape, lambda *_: (0,) * nd)


FFN_TM = 512
FFN_TF = 256


def _ffn_body(x_ref, wg_ref, wu_ref, wd_ref, g_ref, b_ref, o_ref, acc_ref, *, nf):
    x = x_ref[...]
    xb = x.astype(BF16)
    for c in range(nf):
        gate = _dot(xb, wg_ref[c])
        up = _dot(xb, wu_ref[c])
        h = (gate * jax.nn.sigmoid(gate) * up).astype(BF16)
        d = _dot(h, wd_ref[c])
        if c == 0:
            acc_ref[...] = d
        else:
            acc_ref[...] += d
    y = DN_ALPHA * x + 0.5 * acc_ref[...]
    o_ref[...] = _layer_norm(y, g_ref[...], b_ref[...])


def _ffn_ln(x2, w_gate, w_up, w_down, ln_g, ln_b):
    n, d = x2.shape
    f = w_gate.shape[1]
    nf = f // FFN_TF
    wg = w_gate.astype(BF16).reshape(d, nf, FFN_TF).transpose(1, 0, 2)
    wu = w_up.astype(BF16).reshape(d, nf, FFN_TF).transpose(1, 0, 2)
    wd = w_down.astype(BF16).reshape(nf, FFN_TF, d)
    tm = min(FFN_TM, n)
    return pl.pallas_call(
        functools.partial(_ffn_body, nf=nf),
        grid=(n // tm,),
        in_specs=[
            pl.BlockSpec((tm, d), lambda i: (i, 0)),
            _full((nf, d, FFN_TF)),
            _full((nf, d, FFN_TF)),
            _full((nf, FFN_TF, d)),
            _full((1, d)),
            _full((1, d)),
        ],
        out_specs=pl.BlockSpec((tm, d), lambda i: (i, 0)),
        out_shape=jax.ShapeDtypeStruct((n, d), F32),
        scratch_shapes=[pltpu.VMEM((tm, d), F32)],
        compiler_params=_cparams("parallel"),
        name="ffn_ln",
    )(x2, wg, wu, wd, ln_g.reshape(1, d), ln_b.reshape(1, d))


PROJ_TM = 512


def _proj_body(x_ref, *refs, scales):
    n = len(scales)
    xb = x_ref[...].astype(BF16)
    for w_ref, o_ref, s in zip(refs[:n], refs[n:], scales):
        r = _dot(xb, w_ref[...])
        if s != 1.0:
            r = r * s
        o_ref[...] = r.astype(o_ref.dtype)


def _proj(x2, weights, dtypes, scales=None):
    n, k = x2.shape
    scales = tuple(scales) if scales is not None else (1.0,) * len(weights)
    tm = min(PROJ_TM, n)
    in_specs = [pl.BlockSpec((tm, k), lambda i: (i, 0))] + [_full(w.shape) for w in weights]
    out_specs = [pl.BlockSpec((tm, w.shape[1]), lambda i: (i, 0)) for w in weights]
    out_shape = [jax.ShapeDtypeStruct((n, w.shape[1]), dt) for w, dt in zip(weights, dtypes)]
    return pl.pallas_call(
        functools.partial(_proj_body, scales=scales),
        grid=(n // tm,),
        in_specs=in_specs,
        out_specs=out_specs,
        out_shape=out_shape,
        compiler_params=_cparams("parallel"),
        name="proj",
    )(x2, *weights)


def _pad_cols(w, m):
    return jnp.pad(w, ((0, 0), (0, m - w.shape[1])))


def _oproj_body(o_ref, w_ref, x_ref, g_ref, b_ref, out_ref):
    y = _dot(o_ref[...], w_ref[...])
    out_ref[...] = _layer_norm(DN_ALPHA * x_ref[...] + y, g_ref[...], b_ref[...])


def _oproj_ln(o2, w_out, x2, ln_g, ln_b):
    n, k = o2.shape
    d = x2.shape[1]
    tm = min(PROJ_TM, n)
    return pl.pallas_call(
        _oproj_body,
        grid=(n // tm,),
        in_specs=[
            pl.BlockSpec((tm, k), lambda i: (i, 0)),
            _full((k, d)),
            pl.BlockSpec((tm, d), lambda i: (i, 0)),
            _full((1, d)),
            _full((1, d)),
        ],
        out_specs=pl.BlockSpec((tm, d), lambda i: (i, 0)),
        out_shape=jax.ShapeDtypeStruct((n, d), F32),
        compiler_params=_cparams("parallel"),
        name="oproj_ln",
    )(o2, w_out.astype(BF16), x2, ln_g.reshape(1, d), ln_b.reshape(1, d))


XATTN_TQ = 512


def _xattn_body(x_ref, kv_ref, wq_ref, wo_ref, g_ref, b_ref, o_ref, *, heads):
    x = x_ref[0]
    d = x.shape[-1]
    dh = d // heads
    q = (_dot(x.astype(BF16), wq_ref[...]) * dh ** -0.5).astype(BF16)
    outs = []
    for h in range(heads):
        qh = q[:, h * dh:(h + 1) * dh]
        kh = kv_ref[0, :, h * dh:(h + 1) * dh]
        vh = kv_ref[0, :, d + h * dh:d + (h + 1) * dh]
        s = _dot_nt(qh, kh)
        m = jnp.max(s, axis=-1, keepdims=True)
        e = jnp.exp(s - m)
        p = e / jnp.sum(e, axis=-1, keepdims=True)
        outs.append(_dot(p.astype(BF16), vh).astype(BF16))
    o = jnp.concatenate(outs, axis=-1)
    y = _dot(o, wo_ref[...])
    o_ref[0] = _layer_norm(DN_ALPHA * x + y, g_ref[...], b_ref[...])


def _xattn_ln(x, mem, w_q, w_kv, w_o, ln_g, ln_b):
    b, s, d = x.shape
    m = mem.shape[1]
    (kv,) = _proj(mem.reshape(b * m, d), [w_kv.astype(BF16)], [BF16])
    kv = kv.reshape(b, m, 2 * d)
    tq = min(XATTN_TQ, s)
    return pl.pallas_call(
        functools.partial(_xattn_body, heads=XATTN_HEADS),
        grid=(b, s // tq),
        in_specs=[
            pl.BlockSpec((1, tq, d), lambda i, j: (i, j, 0)),
            pl.BlockSpec((1, m, 2 * d), lambda i, j: (i, 0, 0)),
            _full((d, d)),
            _full((d, d)),
            _full((1, d)),
            _full((1, d)),
        ],
        out_specs=pl.BlockSpec((1, tq, d), lambda i, j: (i, j, 0)),
        out_shape=jax.ShapeDtypeStruct((b, s, d), F32),
        compiler_params=_cparams("parallel", "parallel"),
        name="xattn_ln",
    )(x, kv, w_q.astype(BF16), w_o.astype(BF16), ln_g.reshape(1, d), ln_b.reshape(1, d))


def _swa_body(sink_ref, q_ref, kp_ref, kc_ref, vp_ref, vc_ref, o_ref):
    blk = SWA_WINDOW
    d = HEAD_DIM
    rep = N_HEADS // SWA_KV_HEADS
    n = pl.program_id(1)
    qi = lax.broadcasted_iota(jnp.int32, (blk, 2 * blk), 0)
    ki = lax.broadcasted_iota(jnp.int32, (blk, 2 * blk), 1)
    dist = qi + blk - ki
    valid = (dist >= 0) & (dist < SWA_WINDOW) & ((ki >= blk) | (n > 0))
    distf = dist.astype(F32)
    for g in range(SWA_KV_HEADS):
        cs = slice(g * d, (g + 1) * d)
        kg = jnp.concatenate([kp_ref[0, :, cs], kc_ref[0, :, cs]], axis=0)
        vg = jnp.concatenate([vp_ref[0, :, cs], vc_ref[0, :, cs]], axis=0)
        for r in range(rep):
            h = g * rep + r
            qh = q_ref[0, :, h * d:(h + 1) * d]
            s = _dot_nt(qh, kg)
            s = jnp.where(valid, s - _alibi_slope(h, N_HEADS) * distf, NEG_INF)
            sink = sink_ref[h]
            m = jnp.maximum(jnp.max(s, axis=-1, keepdims=True), sink)
            e = jnp.exp(s - m)
            p = e / (jnp.sum(e, axis=-1, keepdims=True) + jnp.exp(sink - m))
            o_ref[0, :, h * d:(h + 1) * d] = _dot(p.astype(BF16), vg).astype(o_ref.dtype)


def _swa_mixer_ln(x, w_in, sinks, w_out, ln_g, ln_b):
    b, s, d = x.shape
    hd = N_HEADS * HEAD_DIM
    gd = SWA_KV_HEADS * HEAD_DIM
    blk = SWA_WINDOW
    x2 = x.reshape(b * s, d)
    w = w_in.astype(BF16)
    q, k, v = _proj(x2, [w[:, :hd], w[:, hd:hd + gd], w[:, hd + gd:]], [BF16] * 3,
                    scales=(HEAD_DIM ** -0.5, 1.0, 1.0))
    q = q.reshape(b, s, hd)
    k = k.reshape(b, s, gd)
    v = v.reshape(b, s, gd)
    prev = lambda i, j: (i, jnp.maximum(j - 1, 0), 0)
    cur = lambda i, j: (i, j, 0)
    o = pl.pallas_call(
        _swa_body,
        grid=(b, s // blk),
        in_specs=[
            pl.BlockSpec(memory_space=pltpu.SMEM),
            pl.BlockSpec((1, blk, hd), cur),
            pl.BlockSpec((1, blk, gd), prev),
            pl.BlockSpec((1, blk, gd), cur),
            pl.BlockSpec((1, blk, gd), prev),
            pl.BlockSpec((1, blk, gd), cur),
        ],
        out_specs=pl.BlockSpec((1, blk, hd), cur),
        out_shape=jax.ShapeDtypeStruct((b, s, hd), BF16),
        compiler_params=_cparams("parallel", "parallel"),
        name="swa_core",
    )(sinks.astype(F32), q, k, k, v, v)
    return _oproj_ln(o.reshape(b * s, hd), w_out, x2, ln_g, ln_b).reshape(b, s, d)


POOL_TM = 512
POOL_HALO = 16


def _pool_body(x_ref, halo_ref, w_ref, sc_ref, g_ref, b_ref, o_ref, ext_ref, *, tm):
    t = pl.program_id(1)
    x = x_ref[0]
    ext_ref[0:POOL_HALO, :] = jnp.where(t > 0, halo_ref[0], 0.0)
    ext_ref[POOL_HALO:, :] = x
    row = t * tm + lax.broadcasted_iota(jnp.int32, (tm, 1), 0)
    ys = []
    for gi, w in enumerate(POOL_WINDOWS):
        cs = slice(gi * POOL_GROUP, (gi + 1) * POOL_GROUP)
        xg = x[:, cs]
        acc = xg
        for k in range(1, w):
            acc = acc + ext_ref[POOL_HALO - k:POOL_HALO - k + tm, cs]
        cnt = jnp.minimum(row + 1, w).astype(F32)
        diff = (acc / cnt - xg).astype(BF16)
        ys.append(_dot(diff, w_ref[gi]))
    y = jnp.concatenate(ys, axis=-1) * sc_ref[...]
    o_ref[0] = _layer_norm(DN_ALPHA * x + y, g_ref[...], b_ref[...])


def _pool_mixer_ln(x, w_grp, scale, ln_g, ln_b):
    b, s, d = x.shape
    tm = min(POOL_TM, s)
    hb = tm // POOL_HALO
    ng = len(POOL_WINDOWS)
    return pl.pallas_call(
        functools.partial(_pool_body, tm=tm),
        grid=(b, s // tm),
        in_specs=[
            pl.BlockSpec((1, tm, d), lambda i, j: (i, j, 0)),
            pl.BlockSpec((1, POOL_HALO, d), lambda i, j: (i, jnp.maximum(j * hb - 1, 0), 0)),
            _full((ng, POOL_GROUP, POOL_GROUP)),
            _full((1, d)),
            _full((1, d)),
            _full((1, d)),
        ],
        out_specs=pl.BlockSpec((1, tm, d), lambda i, j: (i, j, 0)),
        out_shape=jax.ShapeDtypeStruct((b, s, d), F32),
        scratch_shapes=[pltpu.VMEM((tm + POOL_HALO, d), F32)],
        compiler_params=_cparams("parallel", "parallel"),
        name="pool_ln",
    )(x, x, w_grp.astype(BF16), scale.reshape(1, d), ln_g.reshape(1, d), ln_b.reshape(1, d))


def _gla_tables(c):
    levels = []
    s = c
    while s >= 1:
        levels.append(s)
        s //= 2
    rows, masks = [], []
    idx = np.arange(c)
    for s in levels:
        dq = np.zeros((c, c), np.float32)
        dk = np.zeros((c, c), np.float32)
        for i in range(c):
            blk = i // s
            if s == c or blk % 2 == 1:
                dq[i, blk * s:i + 1] = 1.0
            dk[i, i + 1:blk * s + s] = 1.0
        rows.append(dq)
        if s > 1:
            rows.append(dk)
        if s < c:
            masks.append(((idx[:, None] // (2 * s) == idx[None, :] // (2 * s))
                          & ((idx[:, None] // s) % 2 == 1) & ((idx[None, :] // s) % 2 == 0)))
    masks.append(np.eye(c, dtype=bool))
    return np.concatenate(rows, 0), np.stack(masks).astype(np.float32), len(levels)


def _gla_body(q_ref, k_ref, v_ref, glr_ref, r_ref, wg2_ref, bg_ref, ng_ref, dall_ref, mask_ref,
              o_ref, st_ref, *, nlev):
    c = GLA_CHUNK
    heads = GLA_HEADS
    dk = GLA_DK // heads
    dv = GLA_DV // heads

    @pl.when(pl.program_id(1) == 0)
    def _():
        st_ref[...] = jnp.zeros_like(st_ref)

    z = _dot(glr_ref[0].astype(BF16), wg2_ref[...]) + bg_ref[...]
    log_a = (jnp.minimum(z, 0.0) - jnp.log1p(jnp.exp(-jnp.abs(z)))) * (1.0 / GLA_TAU)
    hi = log_a.astype(BF16)
    lo = (log_a - hi.astype(F32)).astype(BF16)
    e2 = _dot(dall_ref[...], jnp.concatenate([hi, lo], axis=-1))
    decay = jnp.exp(e2[:, :GLA_DK] + e2[:, GLA_DK:])
    q = q_ref[0]
    k = k_ref[0]
    outs = []
    for h in range(heads):
        ks = slice(h * dk, (h + 1) * dk)
        qh = q[:, ks]
        kh = k[:, ks]
        vh = v_ref[0, :, h * dv:(h + 1) * dv]
        tbl = lambda i: decay[i * c:(i + 1) * c, ks]
        q_in = (qh * tbl(0)).astype(BF16)
        k_out = (kh * tbl(1)).astype(BF16)
        khb = kh.astype(BF16)
        att = mask_ref[nlev - 1] * _dot_nt(qh.astype(BF16), khb)
        for li in range(nlev - 1):
            ql = (qh * tbl(2 + 2 * li)).astype(BF16)
            kl = (kh * tbl(3 + 2 * li)).astype(BF16) if li < nlev - 2 else khb
            att = att + mask_ref[li] * _dot_nt(ql, kl)
        st = st_ref[h]
        out = _dot(att.astype(BF16), vh) + _dot_nt(q_in, st.astype(BF16))
        total = decay[c - 1:c, ks]
        st_ref[h] = st * total + _dot_tn(vh, k_out)
        out = out * lax.rsqrt(jnp.mean(out * out, axis=-1, keepdims=True) + LN_EPS) * ng_ref[...]
        outs.append(out)
    r = r_ref[0]
    o = jnp.concatenate(outs, axis=-1) * (r * jax.nn.sigmoid(r))
    o_ref[0] = o.astype(o_ref.dtype)


def _gla_mixer_ln(x, w_in, w_gate2, b_gate, norm_g, w_out, ln_g, ln_b):
    b, s, d = x.shape
    c = GLA_CHUNK
    heads = GLA_HEADS
    dk = GLA_DK // heads
    dv = GLA_DV // heads
    x2 = x.reshape(b * s, d)
    w = w_in.astype(BF16)
    o0, o1, o2, o3 = GLA_DK, 2 * GLA_DK, 2 * GLA_DK + GLA_DV, 2 * GLA_DK + GLA_DV + GLA_GATE_RANK
    q, k, v, glr, r = _proj(
        x2, [w[:, :o0], w[:, o0:o1], w[:, o1:o2], _pad_cols(w[:, o2:o3], LANES), w[:, o3:]],
        [F32, F32, BF16, F32, F32], scales=(dk ** -0.5, 1.0, 1.0, 1.0, 1.0))
    wg2 = jnp.pad(w_gate2.astype(BF16), ((0, LANES - GLA_GATE_RANK), (0, 0)))
    dall, masks, nlev = _gla_tables(c)
    tok = lambda width: pl.BlockSpec((1, c, width), lambda i, j: (i, j, 0))
    o = pl.pallas_call(
        functools.partial(_gla_body, nlev=nlev),
        grid=(b, s // c),
        in_specs=[
            tok(GLA_DK), tok(GLA_DK), tok(GLA_DV), tok(LANES), tok(GLA_DV),
            _full((LANES, GLA_DK)),
            _full((1, GLA_DK)),
            _full((1, dv)),
            _full(dall.shape),
            _full(masks.shape),
        ],
        out_specs=tok(GLA_DV),
        out_shape=jax.ShapeDtypeStruct((b, s, GLA_DV), BF16),
        scratch_shapes=[pltpu.VMEM((heads, dv, dk), F32)],
        compiler_params=_cparams("parallel", "arbitrary"),
        name="gla_core",
    )(q.reshape(b, s, -1), k.reshape(b, s, -1), v.reshape(b, s, -1), glr.reshape(b, s, -1),
      r.reshape(b, s, -1), wg2, b_gate.reshape(1, -1), norm_g.reshape(1, -1),
      jnp.asarray(dall, BF16), jnp.asarray(masks, F32))
    return _oproj_ln(o.reshape(b * s, GLA_DV), w_out, x2, ln_g, ln_b).reshape(b, s, d)


NSA_TQ = 128
NSA_TK = 128


def _gelu_tanh(x):
    return 0.5 * x * (1.0 + jnp.tanh(np.sqrt(2.0 / np.pi).astype(np.float32) * (x + 0.044715 * (x * x * x))))


def _cmp_body(c_ref, plo_ref, phi_ref, w1a_ref, w1b_ref, w2_ref, o_ref):
    c = c_ref[0]
    a = _dot((c + plo_ref[...]).astype(BF16), w1a_ref[...])
    bm = _dot((c + phi_ref[...]).astype(BF16), w1b_ref[...])
    nxt = jnp.concatenate([bm[1:], jnp.zeros_like(bm[:1])], axis=0)
    row = lax.broadcasted_iota(jnp.int32, a.shape, 0)
    pre = jnp.where(row < a.shape[0] - 1, a + nxt, 0.0)
    o_ref[0] = _dot(_gelu_tanh(pre).astype(BF16), w2_ref[...]).astype(o_ref.dtype)


def _nsa_compress(kvc, cmp_pos, cmp_w1, cmp_w2):
    b, s, _ = kvc.shape
    st, g, d = NSA_CMP_STRIDE, NSA_KV_GROUPS, HEAD_DIM
    nch = s // st
    width = 2 * g * d
    c = kvc.reshape(b, nch, st * width)
    eye = jnp.eye(2 * g, dtype=F32)
    sel = jnp.repeat(jnp.eye(2, dtype=F32), g, axis=1)

    def expand(w1_half):
        wj = jnp.einsum('jlde,jc->lcde', w1_half, sel)
        return jnp.einsum('lcde,cf->lcdfe', wj, eye).reshape(st * width, width).astype(BF16)

    w1a = expand(cmp_w1[:, :st])
    w1b = expand(cmp_w1[:, st:])
    w2 = jnp.einsum('cde,cf->cdfe', jnp.einsum('jde,jc->cde', cmp_w2, sel), eye).reshape(width, width).astype(BF16)

    def pos_row(p_half):
        return jnp.einsum('jld,jc->lcd', p_half, sel).reshape(1, st * width)

    return pl.pallas_call(
        _cmp_body,
        grid=(b,),
        in_specs=[
            pl.BlockSpec((1, nch, st * width), lambda i: (i, 0, 0)),
            _full((1, st * width)),
            _full((1, st * width)),
            _full((st * width, width)),
            _full((st * width, width)),
            _full((width, width)),
        ],
        out_specs=pl.BlockSpec((1, nch, width), lambda i: (i, 0, 0)),
        out_shape=jax.ShapeDtypeStruct((b, nch, width), BF16),
        compiler_params=_cparams("parallel"),
        name="nsa_compress",
    )(c, pos_row(cmp_pos[:, :st]), pos_row(cmp_pos[:, st:]), w1a, w1b, w2)


def _nsa_overlap_t(s):
    n_cmp_pad = s // NSA_CMP_STRIDE
    n_sel = s // NSA_SEL_LEN
    blk = np.arange(n_cmp_pad) * NSA_CMP_STRIDE
    sel_start = np.arange(n_sel) * NSA_SEL_LEN
    ov = (blk[None, :] < sel_start[:, None] + NSA_SEL_LEN) & (blk[None, :] + NSA_CMP_LEN > sel_start[:, None])
    return ov.astype(np.float32)


NSA_QROWS = 128
NSA_BIAS_ROWS = 16
NSA_MASK = 1.0e30
NSA_COL_BLOCK = 256


def _split3_bf16(x):
    out = []
    r = np.float32(x)
    for _ in range(3):
        p = np.float32(np.asarray(r, np.float32).astype(BF16).astype(np.float32))
        out.append(float(p))
        r = np.float32(r - p)
    return out


def _nsa_key_tail(pos, n_sel, sel_len, with_blocks):
    n = pos.shape[0]
    hi = np.floor(pos / LANES)
    lo = pos - hi * LANES
    tail = np.zeros((n, NSA_QROWS - HEAD_DIM), np.float32)
    tail[:, 0:3] = hi[:, None]
    tail[:, 3:6] = lo[:, None]
    tail[:, 6:9] = 1.0
    if with_blocks:
        blk = (pos // sel_len).astype(np.int64)
        tail[np.arange(n), NSA_BIAS_ROWS + blk] = 1.0
    return tail


def _nsa_body(qt_ref, kc_ref, vct_ref, ks_ref, kw_ref, vt_ref, glt_ref, ovt_ref, o_ref, qt_scr, sa_scr, sb_scr, *, tq):
    d = HEAD_DIM
    groups = NSA_KV_GROUPS
    rep = N_HEADS // groups
    n_cmp = kc_ref.shape[2]
    n_sel = ovt_ref.shape[0]
    t = pl.program_id(1)
    q0 = t * tq
    win_tiles = NSA_WINDOW // tq
    gates = jax.nn.sigmoid(glt_ref[...])

    tpos = (q0 + lax.broadcasted_iota(jnp.int32, (1, tq), 1)).astype(F32)
    brow = lax.broadcasted_iota(jnp.int32, (NSA_BIAS_ROWS, tq), 0)
    ci = lax.broadcasted_iota(jnp.int32, (tq, tq), 0)
    ai = lax.broadcasted_iota(jnp.int32, (tq, tq), 1)
    tile8 = lambda m: jnp.concatenate([m] * rep, axis=1)
    diag_add = tile8(jnp.where(ci <= ai, 0.0, NEG_INF))
    part_add = tile8(jnp.where((ci > ai) & (t >= win_tiles), 0.0, NEG_INF))
    ncm = lax.broadcasted_iota(jnp.int32, (n_cmp, tq), 0)
    tcm = q0 + lax.broadcasted_iota(jnp.int32, (n_cmp, tq), 1)
    valid_c = ncm * NSA_CMP_STRIDE + (NSA_CMP_LEN - 1) <= tcm
    jm = lax.broadcasted_iota(jnp.int32, (n_sel, tq), 0)
    cur = (q0 + lax.broadcasted_iota(jnp.int32, (n_sel, tq), 1)) // NSA_SEL_LEN
    causal_sel = jm <= cur
    forced = (jm == 0) | (jm == cur) | (jm == cur - 1)
    pad_rows = jnp.zeros((NSA_QROWS - d - NSA_BIAS_ROWS - n_sel, tq), BF16)

    o_cmp = []
    for g in range(groups):
        heads = []
        for r in range(rep):
            h = g * rep + r
            slope = _alibi_slope(h, N_HEADS)
            pieces = _split3_bf16(slope * LANES) + _split3_bf16(slope)
            v = -np.float32(slope) * tpos
            v_hi = v.astype(BF16).astype(F32)
            v_mid = (v - v_hi).astype(BF16).astype(F32)
            v_lo = v - v_hi - v_mid
            bias = jnp.zeros((NSA_BIAS_ROWS, tq), F32)
            for k, val in enumerate(pieces + [v_hi, v_mid, v_lo]):
                bias = jnp.where(brow == k, val, bias)
            heads.append((qt_ref[h * d:(h + 1) * d, :], bias.astype(BF16)))

        def build(mask_rows):
            cols = [jnp.concatenate([qh, bias, mask_rows, pad_rows], axis=0) for qh, bias in heads]
            return jnp.concatenate(cols, axis=1)

        qt_scr[2 * g] = build(jnp.zeros((n_sel, tq), BF16))

        s_c = _dot(kc_ref[0, g], qt_scr[2 * g])
        p_sum = jnp.zeros((n_cmp, tq), F32)
        ps = []
        for r in range(rep):
            sr = jnp.where(valid_c, s_c[:, r * tq:(r + 1) * tq], NEG_INF)
            m = jnp.max(sr, axis=0, keepdims=True)
            e = jnp.where(valid_c, jnp.exp(sr - m), 0.0)
            den = jnp.sum(e, axis=0, keepdims=True)
            p = e * jnp.where(den > 0.0, 1.0 / den, 0.0)
            p_sum = p_sum + p
            ps.append(p.astype(BF16))
        o_cmp.append(_dot(vct_ref[0, g * d:(g + 1) * d, :], jnp.concatenate(ps, axis=1)))

        p_hi = p_sum.astype(BF16)
        p_lo = (p_sum - p_hi.astype(F32)).astype(BF16)
        ovt = ovt_ref[...]
        imp_t = _dot(ovt, p_hi) + _dot(ovt, p_lo)
        score = jnp.where(causal_sel, imp_t + jnp.where(forced, NSA_FORCE_BONUS, 0.0), NEG_INF)
        rank = jnp.zeros((n_sel, tq), F32)
        for mp in range(n_sel):
            row = score[mp:mp + 1, :]
            ahead = (row > score) | ((row == score) & (mp < jm))
            rank = rank + jnp.where(ahead, 1.0, 0.0)
        selected = (rank < NSA_TOPK) & causal_sel
        qt_scr[2 * g + 1] = build(jnp.where(selected, 0.0, -NSA_MASK).astype(BF16))

    def score(chains, kt, slot):
        k0 = pl.multiple_of(kt * tq, tq)
        for ci, (k_ref, g, _, qi) in enumerate(chains):
            slot[ci] = _dot(k_ref[0, g, pl.ds(k0, tq), :], qt_scr[qi])

    def absorb(chains, kt, add, slot, carries):
        out = []
        for ci, ((_, _, v_row0, _), (m_prev, l_prev, acc)) in enumerate(zip(chains, carries)):
            s = slot[ci]
            if add is not None:
                s = s + add
            m_new = jnp.maximum(m_prev, jnp.max(s, axis=0, keepdims=True))
            e = jnp.exp(s - m_new)
            alpha = jnp.exp(m_prev - m_new)
            l_new = alpha * l_prev + jnp.sum(e, axis=0, keepdims=True)
            pv = _dot(vt_ref[0, kt, v_row0:v_row0 + d, :], e.astype(BF16))
            out.append((m_new, l_new, alpha * acc + pv))
        return tuple(out)

    def sweep(chains, lo, carries):
        n = t - lo

        def pair(j, carries):
            a = lo + 2 * j
            score(chains, a + 1, sb_scr)
            carries = absorb(chains, a, None, sa_scr, carries)
            score(chains, a + 2, sa_scr)
            return absorb(chains, a + 1, None, sb_scr, carries)

        carries = lax.fori_loop(0, n // 2, pair, carries)

        def odd_tail(carries):
            score(chains, t, sb_scr)
            carries = absorb(chains, t - 1, None, sa_scr, carries)
            return absorb(chains, t, diag_add, sb_scr, carries)

        def even_tail(carries):
            return absorb(chains, t, diag_add, sa_scr, carries)

        carries = lax.cond(n % 2 == 1, odd_tail, even_tail, carries)
        return [acc / l for _, l, acc in carries]

    init = tuple((jnp.full((1, rep * tq), NEG_INF, F32), jnp.zeros((1, rep * tq), F32),
                  jnp.zeros((d, rep * tq), F32)) for _ in range(groups))

    sel_chains = [(ks_ref, g, g * d, 2 * g + 1) for g in range(groups)]
    score(sel_chains, 0, sa_scr)
    o_sel = sweep(sel_chains, 0, init)

    win_chains = [(kw_ref, g, (groups + g) * d, 2 * g) for g in range(groups)]
    oldest = jnp.maximum(t - win_tiles, 0)
    lo = jnp.maximum(t - win_tiles + 1, 0)
    score(win_chains, oldest, sb_scr)
    score(win_chains, jnp.minimum(lo, t), sa_scr)
    o_win = sweep(win_chains, lo, absorb(win_chains, oldest, part_add, sb_scr, init))

    for g in range(groups):
        for r in range(rep):
            h = g * rep + r
            cs = slice(r * tq, (r + 1) * tq)
            o = (gates[h:h + 1] * o_cmp[g][:, cs] + gates[N_HEADS + h:N_HEADS + h + 1] * o_sel[g][:, cs]
                 + gates[2 * N_HEADS + h:2 * N_HEADS + h + 1] * o_win[g][:, cs])
            o_ref[h * d:(h + 1) * d, :] = o.astype(o_ref.dtype)


def _nsa_proj_body(x_ref, wqt_ref, wkvc_ref, wk2_ref, wvt_ref, wglt_ref,
                   qt_ref, kvc_ref, k2_ref, vt_ref, glt_ref, *, tk):
    xb = x_ref[...].astype(BF16)
    qt_ref[...] = (_dot_nt(wqt_ref[...], xb) * HEAD_DIM ** -0.5).astype(qt_ref.dtype)
    kvc_ref[...] = _dot(xb, wkvc_ref[...])
    k2_ref[...] = _dot(xb, wk2_ref[...]).astype(k2_ref.dtype)
    vt = _dot_nt(wvt_ref[...], xb).astype(vt_ref.dtype)
    for c in range(vt_ref.shape[0]):
        vt_ref[c] = vt[:, c * tk:(c + 1) * tk]
    glt_ref[...] = _dot_nt(wglt_ref[...], xb)


def _nsa_mixer_ln(x, w_in, cmp_pos, cmp_w1, cmp_w2, w_out, ln_g, ln_b):
    b, s, d = x.shape
    hd = N_HEADS * HEAD_DIM
    groups = NSA_KV_GROUPS
    gd = groups * HEAD_DIM
    n = b * s
    tq = NSA_TQ
    n_sel = s // NSA_SEL_LEN
    n_cmp = s // NSA_CMP_STRIDE
    assert HEAD_DIM + NSA_BIAS_ROWS + n_sel <= NSA_QROWS and n_sel % 16 == 0 and NSA_WINDOW % tq == 0
    x2 = x.reshape(n, d)
    w = w_in.astype(BF16)
    c0 = hd + 2 * gd
    w_q_t = w[:, :hd].T
    w_kvc = w[:, hd:c0]
    w_k2 = jnp.concatenate([w[:, c0:c0 + gd], w[:, c0 + 2 * gd:c0 + 3 * gd]], axis=1)
    w_v_t = jnp.concatenate([w[:, c0 + gd:c0 + 2 * gd], w[:, c0 + 3 * gd:c0 + 4 * gd]], axis=1).T
    w_gl_t = _pad_cols(w[:, c0 + 4 * gd:], LANES).T
    tm = min(PROJ_TM, n)
    qt, kvc, k2, vt, glt = pl.pallas_call(
        functools.partial(_nsa_proj_body, tk=tq),
        grid=(n // tm,),
        in_specs=[pl.BlockSpec((tm, d), lambda i: (i, 0)), _full(w_q_t.shape), _full(w_kvc.shape),
                  _full(w_k2.shape), _full(w_v_t.shape), _full(w_gl_t.shape)],
        out_specs=[
            pl.BlockSpec((hd, tm), lambda i: (0, i)),
            pl.BlockSpec((tm, 2 * gd), lambda i: (i, 0)),
            pl.BlockSpec((tm, 2 * gd), lambda i: (i, 0)),
            pl.BlockSpec((tm // tq, 2 * gd, tq), lambda i: (i, 0, 0)),
            pl.BlockSpec((LANES, tm), lambda i: (0, i)),
        ],
        out_shape=[
            jax.ShapeDtypeStruct((hd, n), BF16),
            jax.ShapeDtypeStruct((n, 2 * gd), F32),
            jax.ShapeDtypeStruct((n, 2 * gd), BF16),
            jax.ShapeDtypeStruct((n // tq, 2 * gd, tq), BF16),
            jax.ShapeDtypeStruct((LANES, n), F32),
        ],
        compiler_params=_cparams("parallel"),
        name="nsa_proj",
    )(x2, w_q_t, w_kvc, w_k2, w_v_t, w_gl_t)

    cmp = _nsa_compress(kvc.reshape(b, s, 2 * gd), cmp_pos, cmp_w1, cmp_w2)

    def with_tail(k_tok, tail):
        rows = k_tok.shape[1]
        k = k_tok.reshape(b, rows, groups, HEAD_DIM).transpose(0, 2, 1, 3)
        t_b = jnp.broadcast_to(jnp.asarray(tail, BF16), (b, groups, rows, tail.shape[1]))
        return jnp.concatenate([k, t_b], axis=-1)

    tok = np.arange(s, dtype=np.float64)
    cpos = np.arange(n_cmp, dtype=np.float64) * NSA_CMP_STRIDE + (NSA_CMP_LEN - 1) / 2
    k2 = k2.reshape(b, s, 2 * gd)
    k_sel = with_tail(k2[:, :, :gd], _nsa_key_tail(tok, n_sel, NSA_SEL_LEN, True))
    k_win = with_tail(k2[:, :, gd:], _nsa_key_tail(tok, n_sel, NSA_SEL_LEN, False))
    k_cmp = with_tail(cmp[:, :, :gd], _nsa_key_tail(cpos, n_sel, NSA_SEL_LEN, False))
    v_cmp_t = cmp[:, :, gd:].transpose(0, 2, 1)
    vt = vt.reshape(b, s // tq, 2 * gd, tq)
    ovt = _nsa_overlap_t(s)
    nt = s // tq
    o_t = pl.pallas_call(
        functools.partial(_nsa_body, tq=tq),
        grid=(b, nt),
        in_specs=[
            pl.BlockSpec((hd, tq), lambda i, j: (0, i * nt + j)),
            pl.BlockSpec((1, groups, n_cmp, NSA_QROWS), lambda i, j: (i, 0, 0, 0)),
            pl.BlockSpec((1, gd, n_cmp), lambda i, j: (i, 0, 0)),
            pl.BlockSpec((1, groups, s, NSA_QROWS), lambda i, j: (i, 0, 0, 0)),
            pl.BlockSpec((1, groups, s, NSA_QROWS), lambda i, j: (i, 0, 0, 0)),
            pl.BlockSpec((1, nt, 2 * gd, tq), lambda i, j: (i, 0, 0, 0)),
            pl.BlockSpec((LANES, tq), lambda i, j: (0, i * nt + j)),
            _full(ovt.shape),
        ],
        out_specs=pl.BlockSpec((hd, tq), lambda i, j: (0, i * nt + j)),
        out_shape=jax.ShapeDtypeStruct((hd, n), BF16),
        scratch_shapes=[pltpu.VMEM((2 * groups, NSA_QROWS, (N_HEADS // groups) * tq), BF16),
                        pltpu.VMEM((groups, tq, (N_HEADS // groups) * tq), F32),
                        pltpu.VMEM((groups, tq, (N_HEADS // groups) * tq), F32)],
        compiler_params=_cparams("parallel", "parallel"),
        name="nsa_core",
    )(qt, k_cmp, v_cmp_t, k_sel, k_win, vt, glt, jnp.asarray(ovt, BF16))
    return _oproj_ln(o_t.T, w_out, x2, ln_g, ln_b).reshape(b, s, d)


def kernel(x, mem, ln_g, ln_b, ffn1_w_gate, ffn1_w_up, ffn1_w_down, ffn2_w_gate, ffn2_w_up, ffn2_w_down,
           xattn_w_q, xattn_w_kv, xattn_w_o, swa_w_in, swa_sinks, swa_w_out, nsa_w_in, nsa_cmp_pos,
           nsa_cmp_w1, nsa_cmp_w2, nsa_w_out, gla_w_in, gla_w_gate2, gla_b_gate, gla_norm_g, gla_w_out,
           pool_w, pool_scale):
    b, s, d = x.shape
    for i in range(DEPTH):
        kind = i % N_MIXERS
        j = i // N_MIXERS
        x = _ffn_ln(x.reshape(b * s, d), ffn1_w_gate[i], ffn1_w_up[i], ffn1_w_down[i],
                    ln_g[i, 0], ln_b[i, 0]).reshape(b, s, d)
        if kind == 0:
            x = _swa_mixer_ln(x, swa_w_in[j], swa_sinks[j], swa_w_out[j], ln_g[i, 1], ln_b[i, 1])
        elif kind == 1:
            x = _nsa_mixer_ln(x, nsa_w_in[j], nsa_cmp_pos[j], nsa_cmp_w1[j], nsa_cmp_w2[j], nsa_w_out[j],
                              ln_g[i, 1], ln_b[i, 1])
        elif kind == 2:
            x = _gla_mixer_ln(x, gla_w_in[j], gla_w_gate2[j], gla_b_gate[j], gla_norm_g[j], gla_w_out[j],
                              ln_g[i, 1], ln_b[i, 1])
        else:
            x = _pool_mixer_ln(x, pool_w[j], pool_scale[j], ln_g[i, 1], ln_b[i, 1])
        x = _xattn_ln(x, mem, xattn_w_q[i], xattn_w_kv[i], xattn_w_o[i], ln_g[i, 2], ln_b[i, 2])
        x = _ffn_ln(x.reshape(b * s, d), ffn2_w_gate[i], ffn2_w_up[i], ffn2_w_down[i],
                    ln_g[i, 3], ln_b[i, 3]).reshape(b, s, d)
    return x
```

```python
import functools

import numpy as np
import jax
import jax.numpy as jnp
from jax import lax
from jax.experimental import pallas as pl
from jax.experimental.pallas import tpu as pltpu

F32 = jnp.float32
BF16 = jnp.bfloat16

D_MODEL = 1024
DEPTH = 4
N_MIXERS = 4
HEAD_DIM = 64
N_HEADS = D_MODEL // HEAD_DIM
SWA_KV_HEADS = 4
SWA_WINDOW = 128
NSA_KV_GROUPS = 2
NSA_CMP_LEN = 32
NSA_CMP_STRIDE = 16
NSA_SEL_LEN = 64
NSA_TOPK = 8
NSA_WINDOW = 512
NSA_FORCE_BONUS = 1.0e4
GLA_HEADS = 4
GLA_DK = D_MODEL // 2
GLA_DV = D_MODEL
GLA_GATE_RANK = 16
GLA_TAU = 16.0
GLA_CHUNK = 64
POOL_WINDOWS = (2, 4, 8, 16)
POOL_GROUP = D_MODEL // 4
XATTN_HEADS = 4
DN_ALPHA = (2 * DEPTH) ** 0.25
LN_EPS = 1e-5
NEG_INF = -1e30

LANES = 128
V7X_VMEM_LIMIT_BYTES = 56 * 1024 * 1024

_NT = (((1,), (1,)), ((), ()))
_TN = (((0,), (0,)), ((), ()))


def _cparams(*sem):
    return pltpu.CompilerParams(dimension_semantics=sem, vmem_limit_bytes=V7X_VMEM_LIMIT_BYTES)


def _dot(a, b):
    return jnp.dot(a, b, preferred_element_type=F32)


def _dot_nt(a, b):
    return lax.dot_general(a, b, _NT, preferred_element_type=F32)


def _dot_tn(a, b):
    return lax.dot_general(a, b, _TN, preferred_element_type=F32)


def _layer_norm(y, g, b):
    mu = jnp.mean(y, axis=-1, keepdims=True)
    yc = y - mu
    var = jnp.mean(yc * yc, axis=-1, keepdims=True)
    return yc * lax.rsqrt(var + LN_EPS) * g + b


def _alibi_slope(h, n):
    return float(2.0 ** (-8.0 * (h + 1) / n))


def _full(shape):
    nd = len(shape)
    return pl.BlockSpec(shape, lambda *_: (0,) * nd, pipeline_mode=pl.Buffered(1))


FFN_TM = 512
FFN_TF = 256


def _ffn_body(x_ref, wg_ref, wu_ref, wd_ref, g_ref, b_ref, o_ref, acc_ref, *, nf):
    x = x_ref[...]
    xb = x.astype(BF16)
    for c in range(nf):
        gate = _dot(xb, wg_ref[c])
        up = _dot(xb, wu_ref[c])
        h = (gate * jax.nn.sigmoid(gate) * up).astype(BF16)
        d = _dot(h, wd_ref[c])
        if c == 0:
            acc_ref[...] = d
        else:
            acc_ref[...] += d
    y = DN_ALPHA * x + 0.5 * acc_ref[...]
    o_ref[...] = _layer_norm(y, g_ref[...], b_ref[...])


def _ffn_ln(x2, w_gate, w_up, w_down, ln_g, ln_b):
    n, d = x2.shape
    f = w_gate.shape[1]
    nf = f // FFN_TF
    wg = w_gate.astype(BF16).reshape(d, nf, FFN_TF).transpose(1, 0, 2)
    wu = w_up.astype(BF16).reshape(d, nf, FFN_TF).transpose(1, 0, 2)
    wd = w_down.astype(BF16).reshape(nf, FFN_TF, d)
    tm = min(FFN_TM, n)
    return pl.pallas_call(
        functools.partial(_ffn_body, nf=nf),
        grid=(n // tm,),
        in_specs=[
            pl.BlockSpec((tm, d), lambda i: (i, 0)),
            _full((nf, d, FFN_TF)),
            _full((nf, d, FFN_TF)),
            _full((nf, FFN_TF, d)),
            _full((1, d)),
            _full((1, d)),
        ],
        out_specs=pl.BlockSpec((tm, d), lambda i: (i, 0)),
        out_shape=jax.ShapeDtypeStruct((n, d), F32),
        scratch_shapes=[pltpu.VMEM((tm, d), F32)],
        compiler_params=_cparams("parallel"),
        name="ffn_ln",
    )(x2, wg, wu, wd, ln_g.reshape(1, d), ln_b.reshape(1, d))


PROJ_TM = 512


def _proj_body(x_ref, *refs, scales):
    n = len(scales)
    xb = x_ref[...].astype(BF16)
    for w_ref, o_ref, s in zip(refs[:n], refs[n:], scales):
        r = _dot(xb, w_ref[...])
        if s != 1.0:
            r = r * s
        o_ref[...] = r.astype(o_ref.dtype)


def _proj(x2, weights, dtypes, scales=None):
    n, k = x2.shape
    scales = tuple(scales) if scales is not None else (1.0,) * len(weights)
    tm = min(PROJ_TM, n)
    in_specs = [pl.BlockSpec((tm, k), lambda i: (i, 0))] + [_full(w.shape) for w in weights]
    out_specs = [pl.BlockSpec((tm, w.shape[1]), lambda i: (i, 0)) for w in weights]
    out_shape = [jax.ShapeDtypeStruct((n, w.shape[1]), dt) for w, dt in zip(weights, dtypes)]
    return pl.pallas_call(
        functools.partial(_proj_body, scales=scales),
        grid=(n // tm,),
        in_specs=in_specs,
        out_specs=out_specs,
        out_shape=out_shape,
        compiler_params=_cparams("parallel"),
        name="proj",
    )(x2, *weights)


def _pad_cols(w, m):
    return jnp.pad(w, ((0, 0), (0, m - w.shape[1])))


def _oproj_body(o_ref, w_ref, x_ref, g_ref, b_ref, out_ref, *, feature_major):
    y = _dot_tn(o_ref[...], w_ref[...]) if feature_major else _dot(o_ref[...], w_ref[...])
    out_ref[...] = _layer_norm(DN_ALPHA * x_ref[...] + y, g_ref[...], b_ref[...])


def _oproj_ln(o2, w_out, x2, ln_g, ln_b, feature_major=False):
    n, d = x2.shape
    k = w_out.shape[0]
    tm = min(PROJ_TM, n)
    return pl.pallas_call(
        functools.partial(_oproj_body, feature_major=feature_major),
        grid=(n // tm,),
        in_specs=[
            pl.BlockSpec((k, tm), lambda i: (0, i)) if feature_major else pl.BlockSpec((tm, k), lambda i: (i, 0)),
            _full((k, d)),
            pl.BlockSpec((tm, d), lambda i: (i, 0)),
            _full((1, d)),
            _full((1, d)),
        ],
        out_specs=pl.BlockSpec((tm, d), lambda i: (i, 0)),
        out_shape=jax.ShapeDtypeStruct((n, d), F32),
        compiler_params=_cparams("parallel"),
        name="oproj_ln",
    )(o2, w_out.astype(BF16), x2, ln_g.reshape(1, d), ln_b.reshape(1, d))


XATTN_TQ = 512


def _xattn_body(x_ref, kv_ref, wq_ref, wo_ref, g_ref, b_ref, o_ref, *, heads):
    x = x_ref[0]
    d = x.shape[-1]
    dh = d // heads
    q = (_dot(x.astype(BF16), wq_ref[...]) * dh ** -0.5).astype(BF16)
    outs = []
    for h in range(heads):
        qh = q[:, h * dh:(h + 1) * dh]
        kh = kv_ref[0, :, h * dh:(h + 1) * dh]
        vh = kv_ref[0, :, d + h * dh:d + (h + 1) * dh]
        s = _dot_nt(qh, kh)
        m = jnp.max(s, axis=-1, keepdims=True)
        e = jnp.exp(s - m)
        p = e / jnp.sum(e, axis=-1, keepdims=True)
        outs.append(_dot(p.astype(BF16), vh).astype(BF16))
    o = jnp.concatenate(outs, axis=-1)
    y = _dot(o, wo_ref[...])
    o_ref[0] = _layer_norm(DN_ALPHA * x + y, g_ref[...], b_ref[...])


def _xattn_ln(x, mem, w_q, w_kv, w_o, ln_g, ln_b):
    b, s, d = x.shape
    m = mem.shape[1]
    (kv,) = _proj(mem.reshape(b * m, d), [w_kv.astype(BF16)], [BF16])
    kv = kv.reshape(b, m, 2 * d)
    tq = min(XATTN_TQ, s)
    return pl.pallas_call(
        functools.partial(_xattn_body, heads=XATTN_HEADS),
        grid=(b, s // tq),
        in_specs=[
            pl.BlockSpec((1, tq, d), lambda i, j: (i, j, 0)),
            pl.BlockSpec((1, m, 2 * d), lambda i, j: (i, 0, 0)),
            _full((d, d)),
            _full((d, d)),
            _full((1, d)),
            _full((1, d)),
        ],
        out_specs=pl.BlockSpec((1, tq, d), lambda i, j: (i, j, 0)),
        out_shape=jax.ShapeDtypeStruct((b, s, d), F32),
        compiler_params=_cparams("parallel", "parallel"),
        name="xattn_ln",
    )(x, kv, w_q.astype(BF16), w_o.astype(BF16), ln_g.reshape(1, d), ln_b.reshape(1, d))


ATT_TQ = 128
ATT_QROWS = 128
ATT_BIAS_ROWS = 16


def _split3_bf16(x):
    out = []
    r = np.float32(x)
    for _ in range(3):
        p = np.float32(np.asarray(r, np.float32).astype(BF16).astype(np.float32))
        out.append(float(p))
        r = np.float32(r - p)
    return out


def _alibi_rows(h, tpos, brow):
    slope = _alibi_slope(h, N_HEADS)
    pieces = _split3_bf16(slope * LANES) + _split3_bf16(slope)
    v = -np.float32(slope) * tpos
    v_hi = v.astype(BF16).astype(F32)
    v_mid = (v - v_hi).astype(BF16).astype(F32)
    v_lo = v - v_hi - v_mid
    bias = jnp.zeros(brow.shape, F32)
    for k, val in enumerate(pieces + [v_hi, v_mid, v_lo]):
        bias = jnp.where(brow == k, val, bias)
    return bias.astype(BF16)


def _key_tail(pos, n_blocks=0, block_len=1):
    n = pos.shape[0]
    hi = np.floor(pos / LANES)
    lo = pos - hi * LANES
    tail = np.zeros((n, ATT_QROWS - HEAD_DIM), np.float32)
    tail[:, 0:3] = hi[:, None]
    tail[:, 3:6] = lo[:, None]
    tail[:, 6:9] = 1.0
    if n_blocks:
        blk = (pos // block_len).astype(np.int64)
        tail[np.arange(n), ATT_BIAS_ROWS + blk] = 1.0
    return tail


def _keys_with_tail(k_tok, groups, tail):
    b, rows, _ = k_tok.shape
    k = k_tok.reshape(b, rows, groups, HEAD_DIM).transpose(0, 2, 1, 3)
    t_b = jnp.broadcast_to(jnp.asarray(tail, BF16), (b, groups, rows, tail.shape[1]))
    return jnp.concatenate([k, t_b], axis=-1)


def _proj_mixed_body(x_ref, *refs, kinds, scales, tq):
    n = len(kinds)
    xb = x_ref[...].astype(BF16)
    for w_ref, o_ref, kind, s in zip(refs[:n], refs[n:], kinds, scales):
        r = _dot(xb, w_ref[...]) if kind == "tok" else _dot_nt(w_ref[...], xb)
        if s != 1.0:
            r = r * s
        r = r.astype(o_ref.dtype)
        if kind == "feat_tiles":
            for c in range(o_ref.shape[0]):
                o_ref[c] = r[:, c * tq:(c + 1) * tq]
        else:
            o_ref[...] = r


def _proj_mixed(x2, specs, tq=ATT_TQ):
    n, k = x2.shape
    tm = min(PROJ_TM, n)
    kinds = tuple(s[1] for s in specs)
    ws, out_specs, out_shape = [], [], []
    for w, kind, dt, _ in specs:
        m = w.shape[1]
        if kind == "tok":
            ws.append(w)
            out_specs.append(pl.BlockSpec((tm, m), lambda i: (i, 0)))
            out_shape.append(jax.ShapeDtypeStruct((n, m), dt))
        elif kind == "feat":
            ws.append(w.T)
            out_specs.append(pl.BlockSpec((m, tm), lambda i: (0, i)))
            out_shape.append(jax.ShapeDtypeStruct((m, n), dt))
        else:
            ws.append(w.T)
            out_specs.append(pl.BlockSpec((tm // tq, m, tq), lambda i: (i, 0, 0)))
            out_shape.append(jax.ShapeDtypeStruct((n // tq, m, tq), dt))
    return pl.pallas_call(
        functools.partial(_proj_mixed_body, kinds=kinds, scales=tuple(s[3] for s in specs), tq=tq),
        grid=(n // tm,),
        in_specs=[pl.BlockSpec((tm, k), lambda i: (i, 0))] + [_full(w.shape) for w in ws],
        out_specs=out_specs,
        out_shape=out_shape,
        compiler_params=_cparams("parallel"),
        name="proj_mixed",
    )(x2, *ws)


def _swa_body(sink_ref, qt_ref, k_ref, vt_ref, o_ref, *, tq):
    d = HEAD_DIM
    groups = SWA_KV_HEADS
    rep = N_HEADS // groups
    t = pl.program_id(1)
    prev = jnp.maximum(t - 1, 0)
    tpos = (t * tq + lax.broadcasted_iota(jnp.int32, (1, tq), 1)).astype(F32)
    brow = lax.broadcasted_iota(jnp.int32, (ATT_BIAS_ROWS, tq), 0)
    ci = lax.broadcasted_iota(jnp.int32, (tq, tq), 0)
    ai = lax.broadcasted_iota(jnp.int32, (tq, tq), 1)
    tile_r = lambda m: jnp.concatenate([m] * rep, axis=1)
    diag_add = tile_r(jnp.where(ci <= ai, 0.0, NEG_INF))
    prev_add = tile_r(jnp.where((ci > ai) & (t > 0), 0.0, NEG_INF))
    pad_rows = jnp.zeros((ATT_QROWS - d - ATT_BIAS_ROWS, tq), BF16)
    k0 = pl.multiple_of(t * tq, tq)
    kp = pl.multiple_of(prev * tq, tq)
    scores = []
    for g in range(groups):
        qt = jnp.concatenate(
            [jnp.concatenate([qt_ref[h * d:(h + 1) * d, :], _alibi_rows(h, tpos, brow), pad_rows], axis=0)
             for h in range(g * rep, (g + 1) * rep)], axis=1)
        scores.append((_dot(k_ref[0, g, pl.ds(kp, tq), :], qt), _dot(k_ref[0, g, pl.ds(k0, tq), :], qt)))
    for g in range(groups):
        s_prev = scores[g][0] + prev_add
        s_diag = scores[g][1] + diag_add
        sink = jnp.concatenate([jnp.full((1, tq), sink_ref[g * rep + r], F32) for r in range(rep)], axis=1)
        m = jnp.maximum(jnp.maximum(jnp.max(s_prev, axis=0, keepdims=True),
                                    jnp.max(s_diag, axis=0, keepdims=True)), sink)
        e_prev = jnp.exp(s_prev - m)
        e_diag = jnp.exp(s_diag - m)
        den = jnp.sum(e_prev, axis=0, keepdims=True) + jnp.sum(e_diag, axis=0, keepdims=True) + jnp.exp(sink - m)
        acc = (_dot(vt_ref[0, prev, g * d:(g + 1) * d, :], e_prev.astype(BF16))
               + _dot(vt_ref[0, t, g * d:(g + 1) * d, :], e_diag.astype(BF16)))
        o = acc / den
        for r in range(rep):
            h = g * rep + r
            o_ref[h * d:(h + 1) * d, :] = o[:, r * tq:(r + 1) * tq].astype(o_ref.dtype)


def _swa_mixer_ln(x, w_in, sinks, w_out, ln_g, ln_b):
    b, s, d = x.shape
    hd = N_HEADS * HEAD_DIM
    groups = SWA_KV_HEADS
    gd = groups * HEAD_DIM
    n = b * s
    tq = ATT_TQ
    nt = s // tq
    assert SWA_WINDOW == tq
    x2 = x.reshape(n, d)
    w = w_in.astype(BF16)
    qt, k, vt = _proj_mixed(x2, [(w[:, :hd], "feat", BF16, HEAD_DIM ** -0.5),
                                 (w[:, hd:hd + gd], "tok", BF16, 1.0),
                                 (w[:, hd + gd:], "feat_tiles", BF16, 1.0)])
    kk = _keys_with_tail(k.reshape(b, s, gd), groups, _key_tail(np.arange(s, dtype=np.float64)))
    o_t = pl.pallas_call(
        functools.partial(_swa_body, tq=tq),
        grid=(b, nt),
        in_specs=[
            pl.BlockSpec(memory_space=pltpu.SMEM),
            pl.BlockSpec((hd, tq), lambda i, j: (0, i * nt + j)),
            pl.BlockSpec((1, groups, s, ATT_QROWS), lambda i, j: (i, 0, 0, 0)),
            pl.BlockSpec((1, nt, gd, tq), lambda i, j: (i, 0, 0, 0)),
        ],
        out_specs=pl.BlockSpec((hd, tq), lambda i, j: (0, i * nt + j)),
        out_shape=jax.ShapeDtypeStruct((hd, n), BF16),
        compiler_params=_cparams("parallel", "parallel"),
        name="swa_core",
    )(sinks.astype(F32), qt, kk, vt.reshape(b, nt, gd, tq))
    return _oproj_ln(o_t, w_out, x2, ln_g, ln_b, feature_major=True).reshape(b, s, d)


POOL_TM = 512
POOL_HALO = 16


def _pool_body(x_ref, halo_ref, w_ref, sc_ref, g_ref, b_ref, o_ref, ext_ref, *, tm):
    t = pl.program_id(1)
    x = x_ref[0]
    ext_ref[0:POOL_HALO, :] = jnp.where(t > 0, halo_ref[0], 0.0)
    ext_ref[POOL_HALO:, :] = x
    row = t * tm + lax.broadcasted_iota(jnp.int32, (tm, 1), 0)
    ys = []
    for gi, w in enumerate(POOL_WINDOWS):
        cs = slice(gi * POOL_GROUP, (gi + 1) * POOL_GROUP)
        xg = x[:, cs]
        acc = xg
        for k in range(1, w):
            acc = acc + ext_ref[POOL_HALO - k:POOL_HALO - k + tm, cs]
        cnt = jnp.minimum(row + 1, w).astype(F32)
        diff = (acc / cnt - xg).astype(BF16)
        ys.append(_dot(diff, w_ref[gi]))
    y = jnp.concatenate(ys, axis=-1) * sc_ref[...]
    o_ref[0] = _layer_norm(DN_ALPHA * x + y, g_ref[...], b_ref[...])


def _pool_mixer_ln(x, w_grp, scale, ln_g, ln_b):
    b, s, d = x.shape
    tm = min(POOL_TM, s)
    hb = tm // POOL_HALO
    ng = len(POOL_WINDOWS)
    return pl.pallas_call(
        functools.partial(_pool_body, tm=tm),
        grid=(b, s // tm),
        in_specs=[
            pl.BlockSpec((1, tm, d), lambda i, j: (i, j, 0)),
            pl.BlockSpec((1, POOL_HALO, d), lambda i, j: (i, jnp.maximum(j * hb - 1, 0), 0)),
            _full((ng, POOL_GROUP, POOL_GROUP)),
            _full((1, d)),
            _full((1, d)),
            _full((1, d)),
        ],
        out_specs=pl.BlockSpec((1, tm, d), lambda i, j: (i, j, 0)),
        out_shape=jax.ShapeDtypeStruct((b, s, d), F32),
        scratch_shapes=[pltpu.VMEM((tm + POOL_HALO, d), F32)],
        compiler_params=_cparams("parallel", "parallel"),
        name="pool_ln",
    )(x, x, w_grp.astype(BF16), scale.reshape(1, d), ln_g.reshape(1, d), ln_b.reshape(1, d))


GLA_STEP_CHUNKS = 4


def _gla_tables(c):
    levels = []
    s = c
    while s >= 1:
        levels.append(s)
        s //= 2
    rows, masks = [], []
    idx = np.arange(c)
    for s in levels:
        dq = np.zeros((c, c), np.float32)
        dk = np.zeros((c, c), np.float32)
        for i in range(c):
            blk = i // s
            if s == c or blk % 2 == 1:
                dq[i, blk * s:i + 1] = 1.0
            dk[i, i + 1:blk * s + s] = 1.0
        rows.append(dq)
        if s > 1:
            rows.append(dk)
        if s < c:
            masks.append(((idx[:, None] // (2 * s) == idx[None, :] // (2 * s))
                          & ((idx[:, None] // s) % 2 == 1) & ((idx[None, :] // s) % 2 == 0)))
    masks.append(np.eye(c, dtype=bool))
    return np.concatenate(rows, 0), np.stack(masks).astype(np.float32), len(levels)


def _gla_body(q_ref, k_ref, v_ref, glr_ref, r_ref, wg2_ref, bg_ref, ng_ref, dall_ref, mask_ref,
              o_ref, st_ref, *, nlev, nck):
    c = GLA_CHUNK
    heads = GLA_HEADS
    dk = GLA_DK // heads
    dv = GLA_DV // heads

    @pl.when(pl.program_id(1) == 0)
    def _():
        st_ref[...] = jnp.zeros_like(st_ref)

    z = _dot(glr_ref[0].astype(BF16), wg2_ref[...]) + bg_ref[...]
    log_a = (jnp.minimum(z, 0.0) - jnp.log1p(jnp.exp(-jnp.abs(z)))) * (1.0 / GLA_TAU)
    hi = log_a.astype(BF16)
    lo = (log_a - hi.astype(F32)).astype(BF16)
    parts = []
    for ck in range(nck):
        parts += [hi[ck * c:(ck + 1) * c], lo[ck * c:(ck + 1) * c]]
    e2 = _dot(dall_ref[...], jnp.concatenate(parts, axis=-1))
    q = q_ref[0]
    k = k_ref[0]
    pre = []
    for ck in range(nck):
        rows = slice(ck * c, (ck + 1) * c)
        decay = jnp.exp(e2[:, 2 * ck * GLA_DK:(2 * ck + 1) * GLA_DK]
                        + e2[:, (2 * ck + 1) * GLA_DK:(2 * ck + 2) * GLA_DK])
        for h in range(heads):
            ks = slice(h * dk, (h + 1) * dk)
            qh = q[rows, ks]
            kh = k[rows, ks]
            tbl = lambda i: decay[i * c:(i + 1) * c, ks]
            q_in = (qh * tbl(0)).astype(BF16)
            k_out = (kh * tbl(1)).astype(BF16)
            khb = kh.astype(BF16)
            att = mask_ref[nlev - 1] * _dot_nt(qh.astype(BF16), khb)
            for li in range(nlev - 1):
                ql = (qh * tbl(2 + 2 * li)).astype(BF16)
                kl = (kh * tbl(3 + 2 * li)).astype(BF16) if li < nlev - 2 else khb
                att = att + mask_ref[li] * _dot_nt(ql, kl)
            total = decay[c - 1:c, ks]
            pre.append((att.astype(BF16), q_in, k_out, total))
    states = [st_ref[h] for h in range(heads)]
    r = r_ref[0]
    gate = r * jax.nn.sigmoid(r)
    for ck in range(nck):
        rows = slice(ck * c, (ck + 1) * c)
        outs = []
        for h in range(heads):
            att, q_in, k_out, total = pre[ck * heads + h]
            vh = v_ref[0, rows, h * dv:(h + 1) * dv]
            out = _dot(att, vh) + _dot_nt(q_in, states[h].astype(BF16))
            states[h] = states[h] * total + _dot_tn(vh, k_out)
            outs.append(out * lax.rsqrt(jnp.mean(out * out, axis=-1, keepdims=True) + LN_EPS) * ng_ref[...])
        o_ref[0, rows, :] = (jnp.concatenate(outs, axis=-1) * gate[rows]).astype(o_ref.dtype)
    for h in range(heads):
        st_ref[h] = states[h]


def _gla_mixer_ln(x, w_in, w_gate2, b_gate, norm_g, w_out, ln_g, ln_b):
    b, s, d = x.shape
    c = GLA_CHUNK
    heads = GLA_HEADS
    dk = GLA_DK // heads
    dv = GLA_DV // heads
    x2 = x.reshape(b * s, d)
    w = w_in.astype(BF16)
    o0, o1, o2, o3 = GLA_DK, 2 * GLA_DK, 2 * GLA_DK + GLA_DV, 2 * GLA_DK + GLA_DV + GLA_GATE_RANK
    q, k, v, glr, r = _proj(
        x2, [w[:, :o0], w[:, o0:o1], w[:, o1:o2], _pad_cols(w[:, o2:o3], LANES), w[:, o3:]],
        [F32, F32, BF16, F32, F32], scales=(dk ** -0.5, 1.0, 1.0, 1.0, 1.0))
    wg2 = jnp.pad(w_gate2.astype(BF16), ((0, LANES - GLA_GATE_RANK), (0, 0)))
    dall, masks, nlev = _gla_tables(c)
    nck = GLA_STEP_CHUNKS
    tok = lambda width: pl.BlockSpec((1, nck * c, width), lambda i, j: (i, j, 0))
    o = pl.pallas_call(
        functools.partial(_gla_body, nlev=nlev, nck=nck),
        grid=(b, s // (nck * c)),
        in_specs=[
            tok(GLA_DK), tok(GLA_DK), tok(GLA_DV), tok(LANES), tok(GLA_DV),
            _full((LANES, GLA_DK)),
            _full((1, GLA_DK)),
            _full((1, dv)),
            _full(dall.shape),
            _full(masks.shape),
        ],
        out_specs=tok(GLA_DV),
        out_shape=jax.ShapeDtypeStruct((b, s, GLA_DV), BF16),
        scratch_shapes=[pltpu.VMEM((heads, dv, dk), F32)],
        compiler_params=_cparams("parallel", "arbitrary"),
        name="gla_core",
    )(q.reshape(b, s, -1), k.reshape(b, s, -1), v.reshape(b, s, -1), glr.reshape(b, s, -1),
      r.reshape(b, s, -1), wg2, b_gate.reshape(1, -1), norm_g.reshape(1, -1),
      jnp.asarray(dall, BF16), jnp.asarray(masks, F32))
    return _oproj_ln(o.reshape(b * s, GLA_DV), w_out, x2, ln_g, ln_b).reshape(b, s, d)


def _gelu_tanh(x):
    return 0.5 * x * (1.0 + jnp.tanh(np.sqrt(2.0 / np.pi).astype(np.float32) * (x + 0.044715 * (x * x * x))))


def _cmp_body(c_ref, plo_ref, phi_ref, w1a_ref, w1b_ref, w2_ref, o_ref):
    c = c_ref[0]
    a = _dot((c + plo_ref[...]).astype(BF16), w1a_ref[...])
    bm = _dot((c + phi_ref[...]).astype(BF16), w1b_ref[...])
    nxt = jnp.concatenate([bm[1:], jnp.zeros_like(bm[:1])], axis=0)
    row = lax.broadcasted_iota(jnp.int32, a.shape, 0)
    pre = jnp.where(row < a.shape[0] - 1, a + nxt, 0.0)
    o_ref[0] = _dot(_gelu_tanh(pre).astype(BF16), w2_ref[...]).astype(o_ref.dtype)


def _nsa_compress(kvc, cmp_pos, cmp_w1, cmp_w2):
    b, s, _ = kvc.shape
    st, g, d = NSA_CMP_STRIDE, NSA_KV_GROUPS, HEAD_DIM
    nch = s // st
    width = 2 * g * d
    c = kvc.reshape(b, nch, st * width)
    eye = jnp.eye(2 * g, dtype=F32)
    sel = jnp.repeat(jnp.eye(2, dtype=F32), g, axis=1)

    def expand(w1_half):
        wj = jnp.einsum('jlde,jc->lcde', w1_half, sel)
        return jnp.einsum('lcde,cf->lcdfe', wj, eye).reshape(st * width, width).astype(BF16)

    w1a = expand(cmp_w1[:, :st])
    w1b = expand(cmp_w1[:, st:])
    w2 = jnp.einsum('cde,cf->cdfe', jnp.einsum('jde,jc->cde', cmp_w2, sel), eye).reshape(width, width).astype(BF16)

    def pos_row(p_half):
        return jnp.einsum('jld,jc->lcd', p_half, sel).reshape(1, st * width)

    return pl.pallas_call(
        _cmp_body,
        grid=(b,),
        in_specs=[
            pl.BlockSpec((1, nch, st * width), lambda i: (i, 0, 0)),
            _full((1, st * width)),
            _full((1, st * width)),
            _full((st * width, width)),
            _full((st * width, width)),
            _full((width, width)),
        ],
        out_specs=pl.BlockSpec((1, nch, width), lambda i: (i, 0, 0)),
        out_shape=jax.ShapeDtypeStruct((b, nch, width), BF16),
        compiler_params=_cparams("parallel"),
        name="nsa_compress",
    )(c, pos_row(cmp_pos[:, :st]), pos_row(cmp_pos[:, st:]), w1a, w1b, w2)


def _nsa_overlap_t(s):
    n_cmp_pad = s // NSA_CMP_STRIDE
    n_sel = s // NSA_SEL_LEN
    blk = np.arange(n_cmp_pad) * NSA_CMP_STRIDE
    sel_start = np.arange(n_sel) * NSA_SEL_LEN
    ov = (blk[None, :] < sel_start[:, None] + NSA_SEL_LEN) & (blk[None, :] + NSA_CMP_LEN > sel_start[:, None])
    return ov.astype(np.float32)


NSA_MASK = 1.0e30


def _nsa_body(qt_ref, kc_ref, vct_ref, ks_ref, kw_ref, vt_ref, glt_ref, ovt_ref, o_ref, qt_scr, sa_scr, sb_scr, *, tq):
    d = HEAD_DIM
    groups = NSA_KV_GROUPS
    rep = N_HEADS // groups
    n_cmp = kc_ref.shape[2]
    n_sel = ovt_ref.shape[0]
    t = pl.program_id(1)
    q0 = t * tq
    win_tiles = NSA_WINDOW // tq
    gates = jax.nn.sigmoid(glt_ref[...])

    tpos = (q0 + lax.broadcasted_iota(jnp.int32, (1, tq), 1)).astype(F32)
    brow = lax.broadcasted_iota(jnp.int32, (ATT_BIAS_ROWS, tq), 0)
    ci = lax.broadcasted_iota(jnp.int32, (tq, tq), 0)
    ai = lax.broadcasted_iota(jnp.int32, (tq, tq), 1)
    tile8 = lambda m: jnp.concatenate([m] * rep, axis=1)
    diag_add = tile8(jnp.where(ci <= ai, 0.0, NEG_INF))
    part_add = tile8(jnp.where((ci > ai) & (t >= win_tiles), 0.0, NEG_INF))
    ncm = lax.broadcasted_iota(jnp.int32, (n_cmp, tq), 0)
    tcm = q0 + lax.broadcasted_iota(jnp.int32, (n_cmp, tq), 1)
    valid_c = ncm * NSA_CMP_STRIDE + (NSA_CMP_LEN - 1) <= tcm
    jm = lax.broadcasted_iota(jnp.int32, (n_sel, tq), 0)
    cur = (q0 + lax.broadcasted_iota(jnp.int32, (n_sel, tq), 1)) // NSA_SEL_LEN
    causal_sel = jm <= cur
    forced = (jm == 0) | (jm == cur) | (jm == cur - 1)
    pad_rows = jnp.zeros((ATT_QROWS - d - ATT_BIAS_ROWS - n_sel, tq), BF16)

    o_cmp = []
    for g in range(groups):
        heads = [(qt_ref[h * d:(h + 1) * d, :], _alibi_rows(h, tpos, brow)) for h in range(g * rep, (g + 1) * rep)]

        def build(mask_rows):
            cols = [jnp.concatenate([qh, bias, mask_rows, pad_rows], axis=0) for qh, bias in heads]
            return jnp.concatenate(cols, axis=1)

        qt_scr[2 * g] = build(jnp.zeros((n_sel, tq), BF16))

        s_c = _dot(kc_ref[0, g], qt_scr[2 * g])
        p_sum = jnp.zeros((n_cmp, tq), F32)
        ps = []
        for r in range(rep):
            sr = jnp.where(valid_c, s_c[:, r * tq:(r + 1) * tq], NEG_INF)
            m = jnp.max(sr, axis=0, keepdims=True)
            e = jnp.where(valid_c, jnp.exp(sr - m), 0.0)
            den = jnp.sum(e, axis=0, keepdims=True)
            p = e * jnp.where(den > 0.0, 1.0 / den, 0.0)
            p_sum = p_sum + p
            ps.append(p.astype(BF16))
        o_cmp.append(_dot(vct_ref[0, g * d:(g + 1) * d, :], jnp.concatenate(ps, axis=1)))

        p_hi = p_sum.astype(BF16)
        p_lo = (p_sum - p_hi.astype(F32)).astype(BF16)
        ovt = ovt_ref[...]
        imp_t = _dot(ovt, p_hi) + _dot(ovt, p_lo)
        score = jnp.where(causal_sel, imp_t + jnp.where(forced, NSA_FORCE_BONUS, 0.0), NEG_INF)
        rank = jnp.zeros((n_sel, tq), F32)
        for mp in range(n_sel):
            row = score[mp:mp + 1, :]
            ahead = (row > score) | ((row == score) & (mp < jm))
            rank = rank + jnp.where(ahead, 1.0, 0.0)
        selected = (rank < NSA_TOPK) & causal_sel
        qt_scr[2 * g + 1] = build(jnp.where(selected, 0.0, -NSA_MASK).astype(BF16))

    def score(chains, kt, slot):
        k0 = pl.multiple_of(kt * tq, tq)
        for ci, (k_ref, g, _, qi) in enumerate(chains):
            slot[ci] = _dot(k_ref[0, g, pl.ds(k0, tq), :], qt_scr[qi])

    def absorb(chains, kt, add, slot, carries):
        out = []
        for ci, ((_, _, v_row0, _), (m_prev, l_prev, acc)) in enumerate(zip(chains, carries)):
            s = slot[ci]
            if add is not None:
                s = s + add
            m_new = jnp.maximum(m_prev, jnp.max(s, axis=0, keepdims=True))
            e = jnp.exp(s - m_new)
            alpha = jnp.exp(m_prev - m_new)
            l_new = alpha * l_prev + jnp.sum(e, axis=0, keepdims=True)
            pv = _dot(vt_ref[0, kt, v_row0:v_row0 + d, :], e.astype(BF16))
            out.append((m_new, l_new, alpha * acc + pv))
        return tuple(out)

    def sweep(chains, lo, carries):
        n = t - lo

        def pair(j, carries):
            a = lo + 2 * j
            score(chains, a + 1, sb_scr)
            carries = absorb(chains, a, None, sa_scr, carries)
            score(chains, a + 2, sa_scr)
            return absorb(chains, a + 1, None, sb_scr, carries)

        carries = lax.fori_loop(0, n // 2, pair, carries)

        def odd_tail(carries):
            score(chains, t, sb_scr)
            carries = absorb(chains, t - 1, None, sa_scr, carries)
            return absorb(chains, t, diag_add, sb_scr, carries)

        def even_tail(carries):
            return absorb(chains, t, diag_add, sa_scr, carries)

        carries = lax.cond(n % 2 == 1, odd_tail, even_tail, carries)
        return [acc / l for _, l, acc in carries]

    init = tuple((jnp.full((1, rep * tq), NEG_INF, F32), jnp.zeros((1, rep * tq), F32),
                  jnp.zeros((d, rep * tq), F32)) for _ in range(groups))

    sel_chains = [(ks_ref, g, g * d, 2 * g + 1) for g in range(groups)]
    score(sel_chains, 0, sa_scr)
    o_sel = sweep(sel_chains, 0, init)

    win_chains = [(kw_ref, g, (groups + g) * d, 2 * g) for g in range(groups)]
    oldest = jnp.maximum(t - win_tiles, 0)
    lo = jnp.maximum(t - win_tiles + 1, 0)
    score(win_chains, oldest, sb_scr)
    score(win_chains, jnp.minimum(lo, t), sa_scr)
    o_win = sweep(win_chains, lo, absorb(win_chains, oldest, part_add, sb_scr, init))

    for g in range(groups):
        for r in range(rep):
            h = g * rep + r
            cs = slice(r * tq, (r + 1) * tq)
            o = (gates[h:h + 1] * o_cmp[g][:, cs] + gates[N_HEADS + h:N_HEADS + h + 1] * o_sel[g][:, cs]
                 + gates[2 * N_HEADS + h:2 * N_HEADS + h + 1] * o_win[g][:, cs])
            o_ref[h * d:(h + 1) * d, :] = o.astype(o_ref.dtype)


def _nsa_mixer_ln(x, w_in, cmp_pos, cmp_w1, cmp_w2, w_out, ln_g, ln_b):
    b, s, d = x.shape
    hd = N_HEADS * HEAD_DIM
    groups = NSA_KV_GROUPS
    gd = groups * HEAD_DIM
    n = b * s
    tq = ATT_TQ
    n_sel = s // NSA_SEL_LEN
    n_cmp = s // NSA_CMP_STRIDE
    assert HEAD_DIM + ATT_BIAS_ROWS + n_sel <= ATT_QROWS and n_sel % 16 == 0 and NSA_WINDOW % tq == 0
    x2 = x.reshape(n, d)
    w = w_in.astype(BF16)
    c0 = hd + 2 * gd
    w_k2 = jnp.concatenate([w[:, c0:c0 + gd], w[:, c0 + 2 * gd:c0 + 3 * gd]], axis=1)
    w_v2 = jnp.concatenate([w[:, c0 + gd:c0 + 2 * gd], w[:, c0 + 3 * gd:c0 + 4 * gd]], axis=1)
    qt, kvc, k2, vt, glt = _proj_mixed(x2, [(w[:, :hd], "feat", BF16, HEAD_DIM ** -0.5),
                                            (w[:, hd:c0], "tok", F32, 1.0),
                                            (w_k2, "tok", BF16, 1.0),
                                            (w_v2, "feat_tiles", BF16, 1.0),
                                            (_pad_cols(w[:, c0 + 4 * gd:], LANES), "feat", F32, 1.0)])

    cmp = _nsa_compress(kvc.reshape(b, s, 2 * gd), cmp_pos, cmp_w1, cmp_w2)
    tok = np.arange(s, dtype=np.float64)
    cpos = np.arange(n_cmp, dtype=np.float64) * NSA_CMP_STRIDE + (NSA_CMP_LEN - 1) / 2
    k2 = k2.reshape(b, s, 2 * gd)
    k_sel = _keys_with_tail(k2[:, :, :gd], groups, _key_tail(tok, n_sel, NSA_SEL_LEN))
    k_win = _keys_with_tail(k2[:, :, gd:], groups, _key_tail(tok))
    k_cmp = _keys_with_tail(cmp[:, :, :gd], groups, _key_tail(cpos))
    v_cmp_t = cmp[:, :, gd:].transpose(0, 2, 1)
    vt = vt.reshape(b, s // tq, 2 * gd, tq)
    ovt = _nsa_overlap_t(s)
    nt = s // tq
    o_t = pl.pallas_call(
        functools.partial(_nsa_body, tq=tq),
        grid=(b, nt),
        in_specs=[
            pl.BlockSpec((hd, tq), lambda i, j: (0, i * nt + j)),
            pl.BlockSpec((1, groups, n_cmp, ATT_QROWS), lambda i, j: (i, 0, 0, 0)),
            pl.BlockSpec((1, gd, n_cmp), lambda i, j: (i, 0, 0)),
            pl.BlockSpec((1, groups, s, ATT_QROWS), lambda i, j: (i, 0, 0, 0)),
            pl.BlockSpec((1, groups, s, ATT_QROWS), lambda i, j: (i, 0, 0, 0)),
            pl.BlockSpec((1, nt, 2 * gd, tq), lambda i, j: (i, 0, 0, 0)),
            pl.BlockSpec((LANES, tq), lambda i, j: (0, i * nt + j)),
            _full(ovt.shape),
        ],
        out_specs=pl.BlockSpec((hd, tq), lambda i, j: (0, i * nt + j)),
        out_shape=jax.ShapeDtypeStruct((hd, n), BF16),
        scratch_shapes=[pltpu.VMEM((2 * groups, ATT_QROWS, (N_HEADS // groups) * tq), BF16),
                        pltpu.VMEM((groups, tq, (N_HEADS // groups) * tq), F32),
                        pltpu.VMEM((groups, tq, (N_HEADS // groups) * tq), F32)],
        compiler_params=_cparams("parallel", "parallel"),
        name="nsa_core",
    )(qt, k_cmp, v_cmp_t, k_sel, k_win, vt, glt, jnp.asarray(ovt, BF16))
    return _oproj_ln(o_t, w_out, x2, ln_g, ln_b, feature_major=True).reshape(b, s, d)


def kernel(x, mem, ln_g, ln_b, ffn1_w_gate, ffn1_w_up, ffn1_w_down, ffn2_w_gate, ffn2_w_up, ffn2_w_down,
           xattn_w_q, xattn_w_kv, xattn_w_o, swa_w_in, swa_sinks, swa_w_out, nsa_w_in, nsa_cmp_pos,
           nsa_cmp_w1, nsa_cmp_w2, nsa_w_out, gla_w_in, gla_w_gate2, gla_b_gate, gla_norm_g, gla_w_out,
           pool_w, pool_scale):
    b, s, d = x.shape
    for i in range(DEPTH):
        kind = i % N_MIXERS
        j = i // N_MIXERS
        x = _ffn_ln(x.reshape(b * s, d), ffn1_w_gate[i], ffn1_w_up[i], ffn1_w_down[i],
                    ln_g[i, 0], ln_b[i, 0]).reshape(b, s, d)
        if kind == 0:
            x = _swa_mixer_ln(x, swa_w_in[j], swa_sinks[j], swa_w_out[j], ln_g[i, 1], ln_b[i, 1])
        elif kind == 1:
            x = _nsa_mixer_ln(x, nsa_w_in[j], nsa_cmp_pos[j], nsa_cmp_w1[j], nsa_cmp_w2[j], nsa_w_out[j],
                              ln_g[i, 1], ln_b[i, 1])
        elif kind == 2:
            x = _gla_mixer_ln(x, gla_w_in[j], gla_w_gate2[j], gla_b_gate[j], gla_norm_g[j], gla_w_out[j],
                              ln_g[i, 1], ln_b[i, 1])
        else:
            x = _pool_mixer_ln(x, pool_w[j], pool_scale[j], ln_g[i, 1], ln_b[i, 1])
        x = _xattn_ln(x, mem, xattn_w_q[i], xattn_w_kv[i], xattn_w_o[i], ln_g[i, 2], ln_b[i, 2])
        x = _ffn_ln(x.reshape(b * s, d), ffn2_w_gate[i], ffn2_w_up[i], ffn2_w_down[i],
                    ln_g[i, 3], ln_b[i, 3]).reshape(b, s, d)
    return x
```

```python
import functools

import numpy as np
import jax
import jax.numpy as jnp
from jax import lax
from jax.experimental import pallas as pl
from jax.experimental.pallas import tpu as pltpu

F32 = jnp.float32
BF16 = jnp.bfloat16

D_MODEL = 1024
DEPTH = 4
N_MIXERS = 4
HEAD_DIM = 64
N_HEADS = D_MODEL // HEAD_DIM
SWA_KV_HEADS = 4
SWA_WINDOW = 128
NSA_KV_GROUPS = 2
NSA_CMP_LEN = 32
NSA_CMP_STRIDE = 16
NSA_SEL_LEN = 64
NSA_TOPK = 8
NSA_WINDOW = 512
NSA_FORCE_BONUS = 1.0e4
GLA_HEADS = 4
GLA_DK = D_MODEL // 2
GLA_DV = D_MODEL
GLA_GATE_RANK = 16
GLA_TAU = 16.0
GLA_CHUNK = 64
POOL_WINDOWS = (2, 4, 8, 16)
POOL_GROUP = D_MODEL // 4
XATTN_HEADS = 4
DN_ALPHA = (2 * DEPTH) ** 0.25
LN_EPS = 1e-5
NEG_INF = -1e30
LOG2E = float(np.log2(np.e))

LANES = 128
V7X_VMEM_LIMIT_BYTES = 56 * 1024 * 1024

_NT = (((1,), (1,)), ((), ()))
_TN = (((0,), (0,)), ((), ()))


def _cparams(*sem):
    return pltpu.CompilerParams(dimension_semantics=sem, vmem_limit_bytes=V7X_VMEM_LIMIT_BYTES)


def _dot(a, b):
    return jnp.dot(a, b, preferred_element_type=F32)


def _dot_nt(a, b):
    return lax.dot_general(a, b, _NT, preferred_element_type=F32)


def _dot_tn(a, b):
    return lax.dot_general(a, b, _TN, preferred_element_type=F32)


def _layer_norm(y, g, b):
    mu = jnp.mean(y, axis=-1, keepdims=True)
    yc = y - mu
    var = jnp.mean(yc * yc, axis=-1, keepdims=True)
    return yc * lax.rsqrt(var + LN_EPS) * g + b


def _alibi_slope(h, n):
    return float(2.0 ** (-8.0 * (h + 1) / n))


def _full(shape):
    nd = len(shape)
    return pl.BlockSpec(shape, lambda *_: (0,) * nd, pipeline_mode=pl.Buffered(1))


FFN_TM = 512
FFN_TF = 256


def _ffn_body(x_ref, wg_ref, wu_ref, wd_ref, g_ref, b_ref, o_ref, acc_ref, *, nf):
    x = x_ref[...]
    xb = x.astype(BF16)
    for c in range(nf):
        gate = _dot(xb, wg_ref[c])
        up = _dot(xb, wu_ref[c])
        h = (gate * jax.nn.sigmoid(gate) * up).astype(BF16)
        d = _dot(h, wd_ref[c])
        if c == 0:
            acc_ref[...] = d
        else:
            acc_ref[...] += d
    y = DN_ALPHA * x + 0.5 * acc_ref[...]
    o_ref[...] = _layer_norm(y, g_ref[...], b_ref[...])


def _ffn_ln(x2, w_gate, w_up, w_down, ln_g, ln_b):
    n, d = x2.shape
    f = w_gate.shape[1]
    nf = f // FFN_TF
    wg = w_gate.astype(BF16).reshape(d, nf, FFN_TF).transpose(1, 0, 2)
    wu = w_up.astype(BF16).reshape(d, nf, FFN_TF).transpose(1, 0, 2)
    wd = w_down.astype(BF16).reshape(nf, FFN_TF, d)
    tm = min(FFN_TM, n)
    return pl.pallas_call(
        functools.partial(_ffn_body, nf=nf),
        grid=(n // tm,),
        in_specs=[
            pl.BlockSpec((tm, d), lambda i: (i, 0)),
            _full((nf, d, FFN_TF)),
            _full((nf, d, FFN_TF)),
            _full((nf, FFN_TF, d)),
            _full((1, d)),
            _full((1, d)),
        ],
        out_specs=pl.BlockSpec((tm, d), lambda i: (i, 0)),
        out_shape=jax.ShapeDtypeStruct((n, d), F32),
        scratch_shapes=[pltpu.VMEM((tm, d), F32)],
        compiler_params=_cparams("parallel"),
        name="ffn_ln",
    )(x2, wg, wu, wd, ln_g.reshape(1, d), ln_b.reshape(1, d))


PROJ_TM = 512


def _proj_body(x_ref, *refs, scales):
    n = len(scales)
    xb = x_ref[...].astype(BF16)
    for w_ref, o_ref, s in zip(refs[:n], refs[n:], scales):
        r = _dot(xb, w_ref[...])
        if s != 1.0:
            r = r * s
        o_ref[...] = r.astype(o_ref.dtype)


def _proj(x2, weights, dtypes, scales=None):
    n, k = x2.shape
    scales = tuple(scales) if scales is not None else (1.0,) * len(weights)
    tm = min(PROJ_TM, n)
    in_specs = [pl.BlockSpec((tm, k), lambda i: (i, 0))] + [_full(w.shape) for w in weights]
    out_specs = [pl.BlockSpec((tm, w.shape[1]), lambda i: (i, 0)) for w in weights]
    out_shape = [jax.ShapeDtypeStruct((n, w.shape[1]), dt) for w, dt in zip(weights, dtypes)]
    return pl.pallas_call(
        functools.partial(_proj_body, scales=scales),
        grid=(n // tm,),
        in_specs=in_specs,
        out_specs=out_specs,
        out_shape=out_shape,
        compiler_params=_cparams("parallel"),
        name="proj",
    )(x2, *weights)


def _pad_cols(w, m):
    return jnp.pad(w, ((0, 0), (0, m - w.shape[1])))


def _oproj_body(o_ref, w_ref, x_ref, g_ref, b_ref, out_ref, *, feature_major):
    y = _dot_tn(o_ref[...], w_ref[...]) if feature_major else _dot(o_ref[...], w_ref[...])
    out_ref[...] = _layer_norm(DN_ALPHA * x_ref[...] + y, g_ref[...], b_ref[...])


def _oproj_ln(o2, w_out, x2, ln_g, ln_b, feature_major=False):
    n, d = x2.shape
    k = w_out.shape[0]
    tm = min(PROJ_TM, n)
    return pl.pallas_call(
        functools.partial(_oproj_body, feature_major=feature_major),
        grid=(n // tm,),
        in_specs=[
            pl.BlockSpec((k, tm), lambda i: (0, i)) if feature_major else pl.BlockSpec((tm, k), lambda i: (i, 0)),
            _full((k, d)),
            pl.BlockSpec((tm, d), lambda i: (i, 0)),
            _full((1, d)),
            _full((1, d)),
        ],
        out_specs=pl.BlockSpec((tm, d), lambda i: (i, 0)),
        out_shape=jax.ShapeDtypeStruct((n, d), F32),
        compiler_params=_cparams("parallel"),
        name="oproj_ln",
    )(o2, w_out.astype(BF16), x2, ln_g.reshape(1, d), ln_b.reshape(1, d))


XATTN_TQ = 1024


def _xattn_body(x_ref, kv_ref, wq_ref, wo_ref, g_ref, b_ref, o_ref, *, heads):
    x = x_ref[0]
    d = x.shape[-1]
    dh = d // heads
    q = (_dot(x.astype(BF16), wq_ref[...]) * dh ** -0.5).astype(BF16)
    scores = [_dot_nt(q[:, h * dh:(h + 1) * dh], kv_ref[0, :, h * dh:(h + 1) * dh]) for h in range(heads)]
    outs = []
    for h in range(heads):
        vh = kv_ref[0, :, d + h * dh:d + (h + 1) * dh]
        s = scores[h]
        m = jnp.max(s, axis=-1, keepdims=True)
        e = jnp.exp(s - m)
        p = e / jnp.sum(e, axis=-1, keepdims=True)
        outs.append(_dot(p.astype(BF16), vh).astype(BF16))
    o = jnp.concatenate(outs, axis=-1)
    y = _dot(o, wo_ref[...])
    o_ref[0] = _layer_norm(DN_ALPHA * x + y, g_ref[...], b_ref[...])


def _xattn_ln(x, mem, w_q, w_kv, w_o, ln_g, ln_b):
    b, s, d = x.shape
    m = mem.shape[1]
    (kv,) = _proj(mem.reshape(b * m, d), [w_kv.astype(BF16)], [BF16])
    kv = kv.reshape(b, m, 2 * d)
    tq = min(XATTN_TQ, s)
    return pl.pallas_call(
        functools.partial(_xattn_body, heads=XATTN_HEADS),
        grid=(b, s // tq),
        in_specs=[
            pl.BlockSpec((1, tq, d), lambda i, j: (i, j, 0)),
            pl.BlockSpec((1, m, 2 * d), lambda i, j: (i, 0, 0)),
            _full((d, d)),
            _full((d, d)),
            _full((1, d)),
            _full((1, d)),
        ],
        out_specs=pl.BlockSpec((1, tq, d), lambda i, j: (i, j, 0)),
        out_shape=jax.ShapeDtypeStruct((b, s, d), F32),
        compiler_params=_cparams("parallel", "parallel"),
        name="xattn_ln",
    )(x, kv, w_q.astype(BF16), w_o.astype(BF16), ln_g.reshape(1, d), ln_b.reshape(1, d))


ATT_TQ = 128
ATT_QROWS = 128
ATT_BIAS_ROWS = 16


def _split3_bf16(x):
    out = []
    r = np.float32(x)
    for _ in range(3):
        p = np.float32(np.asarray(r, np.float32).astype(BF16).astype(np.float32))
        out.append(float(p))
        r = np.float32(r - p)
    return out


def _alibi_rows(h, tpos, brow):
    slope = _alibi_slope(h, N_HEADS) * LOG2E
    pieces = _split3_bf16(slope * LANES) + _split3_bf16(slope)
    v = -np.float32(slope) * tpos
    v_hi = v.astype(BF16).astype(F32)
    v_mid = (v - v_hi).astype(BF16).astype(F32)
    v_lo = v - v_hi - v_mid
    bias = jnp.zeros(brow.shape, F32)
    for k, val in enumerate(pieces + [v_hi, v_mid, v_lo]):
        bias = jnp.where(brow == k, val, bias)
    return bias.astype(BF16)


def _key_tail(pos, n_blocks=0, block_len=1):
    n = pos.shape[0]
    hi = np.floor(pos / LANES)
    lo = pos - hi * LANES
    tail = np.zeros((n, ATT_QROWS - HEAD_DIM), np.float32)
    tail[:, 0:3] = hi[:, None]
    tail[:, 3:6] = lo[:, None]
    tail[:, 6:9] = 1.0
    if n_blocks:
        blk = (pos // block_len).astype(np.int64)
        tail[np.arange(n), ATT_BIAS_ROWS + blk] = 1.0
    return tail


def _keys_with_tail(k_tok, groups, tail):
    b, rows, _ = k_tok.shape
    k = k_tok.reshape(b, rows, groups, HEAD_DIM).transpose(0, 2, 1, 3)
    t_b = jnp.broadcast_to(jnp.asarray(tail, BF16), (b, groups, rows, tail.shape[1]))
    return jnp.concatenate([k, t_b], axis=-1)


def _proj_mixed_body(x_ref, *refs, kinds, scales, tq):
    n = len(kinds)
    xb = x_ref[...].astype(BF16)
    for w_ref, o_ref, kind, s in zip(refs[:n], refs[n:], kinds, scales):
        r = _dot(xb, w_ref[...]) if kind == "tok" else _dot_nt(w_ref[...], xb)
        if s != 1.0:
            r = r * s
        r = r.astype(o_ref.dtype)
        if kind == "feat_tiles":
            for c in range(o_ref.shape[0]):
                o_ref[c] = r[:, c * tq:(c + 1) * tq]
        else:
            o_ref[...] = r


def _proj_mixed(x2, specs, tq=ATT_TQ):
    n, k = x2.shape
    tm = min(PROJ_TM, n)
    kinds = tuple(s[1] for s in specs)
    ws, out_specs, out_shape = [], [], []
    for w, kind, dt, _ in specs:
        m = w.shape[1]
        if kind == "tok":
            ws.append(w)
            out_specs.append(pl.BlockSpec((tm, m), lambda i: (i, 0)))
            out_shape.append(jax.ShapeDtypeStruct((n, m), dt))
        elif kind == "feat":
            ws.append(w.T)
            out_specs.append(pl.BlockSpec((m, tm), lambda i: (0, i)))
            out_shape.append(jax.ShapeDtypeStruct((m, n), dt))
        else:
            ws.append(w.T)
            out_specs.append(pl.BlockSpec((tm // tq, m, tq), lambda i: (i, 0, 0)))
            out_shape.append(jax.ShapeDtypeStruct((n // tq, m, tq), dt))
    return pl.pallas_call(
        functools.partial(_proj_mixed_body, kinds=kinds, scales=tuple(s[3] for s in specs), tq=tq),
        grid=(n // tm,),
        in_specs=[pl.BlockSpec((tm, k), lambda i: (i, 0))] + [_full(w.shape) for w in ws],
        out_specs=out_specs,
        out_shape=out_shape,
        compiler_params=_cparams("parallel"),
        name="proj_mixed",
    )(x2, *ws)


def _swa_body(sink_ref, qt_ref, k_ref, vt_ref, o_ref, *, tq):
    d = HEAD_DIM
    groups = SWA_KV_HEADS
    rep = N_HEADS // groups
    t = pl.program_id(1)
    prev = jnp.maximum(t - 1, 0)
    tpos = (t * tq + lax.broadcasted_iota(jnp.int32, (1, tq), 1)).astype(F32)
    brow = lax.broadcasted_iota(jnp.int32, (ATT_BIAS_ROWS, tq), 0)
    ones_rows = jnp.where(brow == 0, 1.0, 0.0).astype(BF16)
    ci = lax.broadcasted_iota(jnp.int32, (tq, tq), 0)
    ai = lax.broadcasted_iota(jnp.int32, (tq, tq), 1)
    tile_r = lambda m: jnp.concatenate([m] * rep, axis=1)
    diag_add = tile_r(jnp.where(ci <= ai, 0.0, NEG_INF))
    prev_add = tile_r(jnp.where((ci > ai) & (t > 0), 0.0, NEG_INF))
    pad_rows = jnp.zeros((ATT_QROWS - d - ATT_BIAS_ROWS, tq), BF16)
    k0 = pl.multiple_of(t * tq, tq)
    kp = pl.multiple_of(prev * tq, tq)
    scores = []
    for g in range(groups):
        qt = jnp.concatenate(
            [jnp.concatenate([qt_ref[h * d:(h + 1) * d, :], _alibi_rows(h, tpos, brow), pad_rows], axis=0)
             for h in range(g * rep, (g + 1) * rep)], axis=1)
        scores.append((_dot(k_ref[0, g, pl.ds(kp, tq), :], qt), _dot(k_ref[0, g, pl.ds(k0, tq), :], qt)))
    for g in range(groups):
        s_prev = scores[g][0] + prev_add
        s_diag = scores[g][1] + diag_add
        sink = jnp.concatenate([jnp.full((1, tq), sink_ref[g * rep + r] * LOG2E, F32) for r in range(rep)], axis=1)
        m = jnp.maximum(jnp.maximum(jnp.max(s_prev, axis=0, keepdims=True),
                                    jnp.max(s_diag, axis=0, keepdims=True)), sink)
        e_prev = jnp.exp2(s_prev - m)
        e_diag = jnp.exp2(s_diag - m)
        acc = (_dot(jnp.concatenate([vt_ref[0, prev, g * d:(g + 1) * d, :], ones_rows], axis=0), e_prev.astype(BF16))
               + _dot(jnp.concatenate([vt_ref[0, t, g * d:(g + 1) * d, :], ones_rows], axis=0), e_diag.astype(BF16)))
        o = acc[:d] / (acc[d:d + 1] + jnp.exp2(sink - m))
        for r in range(rep):
            h = g * rep + r
            o_ref[h * d:(h + 1) * d, :] = o[:, r * tq:(r + 1) * tq].astype(o_ref.dtype)


def _swa_mixer_ln(x, w_in, sinks, w_out, ln_g, ln_b):
    b, s, d = x.shape
    hd = N_HEADS * HEAD_DIM
    groups = SWA_KV_HEADS
    gd = groups * HEAD_DIM
    n = b * s
    tq = ATT_TQ
    nt = s // tq
    assert SWA_WINDOW == tq
    x2 = x.reshape(n, d)
    w = w_in.astype(BF16)
    qt, k, vt = _proj_mixed(x2, [(w[:, :hd], "feat", BF16, HEAD_DIM ** -0.5 * LOG2E),
                                 (w[:, hd:hd + gd], "tok", BF16, 1.0),
                                 (w[:, hd + gd:], "feat_tiles", BF16, 1.0)])
    kk = _keys_with_tail(k.reshape(b, s, gd), groups, _key_tail(np.arange(s, dtype=np.float64)))
    o_t = pl.pallas_call(
        functools.partial(_swa_body, tq=tq),
        grid=(b, nt),
        in_specs=[
            pl.BlockSpec(memory_space=pltpu.SMEM),
            pl.BlockSpec((hd, tq), lambda i, j: (0, i * nt + j)),
            pl.BlockSpec((1, groups, s, ATT_QROWS), lambda i, j: (i, 0, 0, 0)),
            pl.BlockSpec((1, nt, gd, tq), lambda i, j: (i, 0, 0, 0)),
        ],
        out_specs=pl.BlockSpec((hd, tq), lambda i, j: (0, i * nt + j)),
        out_shape=jax.ShapeDtypeStruct((hd, n), BF16),
        compiler_params=_cparams("parallel", "parallel"),
        name="swa_core",
    )(sinks.astype(F32), qt, kk, vt.reshape(b, nt, gd, tq))
    return _oproj_ln(o_t, w_out, x2, ln_g, ln_b, feature_major=True).reshape(b, s, d)


POOL_TM = 512
POOL_HALO = 16


def _pool_body(x_ref, halo_ref, w_ref, sc_ref, g_ref, b_ref, o_ref, ext_ref, *, tm):
    t = pl.program_id(1)
    x = x_ref[0]
    ext_ref[0:POOL_HALO, :] = jnp.where(t > 0, halo_ref[0], 0.0)
    ext_ref[POOL_HALO:, :] = x
    row = t * tm + lax.broadcasted_iota(jnp.int32, (tm, 1), 0)
    ys = []
    for gi, w in enumerate(POOL_WINDOWS):
        cs = slice(gi * POOL_GROUP, (gi + 1) * POOL_GROUP)
        xg = x[:, cs]
        acc = xg
        for k in range(1, w):
            acc = acc + ext_ref[POOL_HALO - k:POOL_HALO - k + tm, cs]
        cnt = jnp.minimum(row + 1, w).astype(F32)
        diff = (acc / cnt - xg).astype(BF16)
        ys.append(_dot(diff, w_ref[gi]))
    y = jnp.concatenate(ys, axis=-1) * sc_ref[...]
    o_ref[0] = _layer_norm(DN_ALPHA * x + y, g_ref[...], b_ref[...])


def _pool_mixer_ln(x, w_grp, scale, ln_g, ln_b):
    b, s, d = x.shape
    tm = min(POOL_TM, s)
    hb = tm // POOL_HALO
    ng = len(POOL_WINDOWS)
    return pl.pallas_call(
        functools.partial(_pool_body, tm=tm),
        grid=(b, s // tm),
        in_specs=[
            pl.BlockSpec((1, tm, d), lambda i, j: (i, j, 0)),
            pl.BlockSpec((1, POOL_HALO, d), lambda i, j: (i, jnp.maximum(j * hb - 1, 0), 0)),
            _full((ng, POOL_GROUP, POOL_GROUP)),
            _full((1, d)),
            _full((1, d)),
            _full((1, d)),
        ],
        out_specs=pl.BlockSpec((1, tm, d), lambda i, j: (i, j, 0)),
        out_shape=jax.ShapeDtypeStruct((b, s, d), F32),
        scratch_shapes=[pltpu.VMEM((tm + POOL_HALO, d), F32)],
        compiler_params=_cparams("parallel", "parallel"),
        name="pool_ln",
    )(x, x, w_grp.astype(BF16), scale.reshape(1, d), ln_g.reshape(1, d), ln_b.reshape(1, d))


GLA_STEP_CHUNKS = 4


def _gla_tables(c):
    levels = []
    s = c
    while s >= 1:
        levels.append(s)
        s //= 2
    rows, masks = [], []
    idx = np.arange(c)
    for s in levels:
        dq = np.zeros((c, c), np.float32)
        dk = np.zeros((c, c), np.float32)
        for i in range(c):
            blk = i // s
            if s == c or blk % 2 == 1:
                dq[i, blk * s:i + 1] = 1.0
            dk[i, i + 1:blk * s + s] = 1.0
        rows.append(dq)
        if s > 1:
            rows.append(dk)
        if s < c:
            masks.append(((idx[:, None] // (2 * s) == idx[None, :] // (2 * s))
                          & ((idx[:, None] // s) % 2 == 1) & ((idx[None, :] // s) % 2 == 0)))
    masks.append(np.eye(c, dtype=bool))
    return np.concatenate(rows, 0), np.stack(masks).astype(np.float32), len(levels)


def _gla_body(q_ref, k_ref, v_ref, glr_ref, r_ref, wg2_ref, bg_ref, ng_ref, dall_ref, mask_ref,
              o_ref, st_ref, *, nlev, nck):
    c = GLA_CHUNK
    heads = GLA_HEADS
    dk = GLA_DK // heads
    dv = GLA_DV // heads

    @pl.when(pl.program_id(1) == 0)
    def _():
        st_ref[...] = jnp.zeros_like(st_ref)

    z = _dot(glr_ref[0].astype(BF16), wg2_ref[...]) + bg_ref[...]
    log_a = (jnp.minimum(z, 0.0) - jnp.log1p(jnp.exp(-jnp.abs(z)))) * (1.0 / GLA_TAU)
    hi = log_a.astype(BF16)
    lo = (log_a - hi.astype(F32)).astype(BF16)
    parts = []
    for ck in range(nck):
        parts += [hi[ck * c:(ck + 1) * c], lo[ck * c:(ck + 1) * c]]
    e2 = _dot(dall_ref[...], jnp.concatenate(parts, axis=-1))
    q = q_ref[0]
    k = k_ref[0]
    pre = []
    for ck in range(nck):
        rows = slice(ck * c, (ck + 1) * c)
        decay = jnp.exp(e2[:, 2 * ck * GLA_DK:(2 * ck + 1) * GLA_DK]
                        + e2[:, (2 * ck + 1) * GLA_DK:(2 * ck + 2) * GLA_DK])
        for h in range(heads):
            ks = slice(h * dk, (h + 1) * dk)
            qh = q[rows, ks]
            kh = k[rows, ks]
            tbl = lambda i: decay[i * c:(i + 1) * c, ks]
            q_in = (qh * tbl(0)).astype(BF16)
            k_out = (kh * tbl(1)).astype(BF16)
            khb = kh.astype(BF16)
            att = mask_ref[nlev - 1] * _dot_nt(qh.astype(BF16), khb)
            for li in range(nlev - 1):
                ql = (qh * tbl(2 + 2 * li)).astype(BF16)
                kl = (kh * tbl(3 + 2 * li)).astype(BF16) if li < nlev - 2 else khb
                att = att + mask_ref[li] * _dot_nt(ql, kl)
            total = decay[c - 1:c, ks]
            pre.append((att.astype(BF16), q_in, k_out, total))
    states = [st_ref[h] for h in range(heads)]
    r = r_ref[0]
    gate = r * jax.nn.sigmoid(r)
    for ck in range(nck):
        rows = slice(ck * c, (ck + 1) * c)
        outs = []
        for h in range(heads):
            att, q_in, k_out, total = pre[ck * heads + h]
            vh = v_ref[0, rows, h * dv:(h + 1) * dv]
            out = _dot(att, vh) + _dot_nt(q_in, states[h].astype(BF16))
            states[h] = states[h] * total + _dot_tn(vh, k_out)
            outs.append(out * lax.rsqrt(jnp.mean(out * out, axis=-1, keepdims=True) + LN_EPS) * ng_ref[...])
        o_ref[0, rows, :] = (jnp.concatenate(outs, axis=-1) * gate[rows]).astype(o_ref.dtype)
    for h in range(heads):
        st_ref[h] = states[h]


def _gla_mixer_ln(x, w_in, w_gate2, b_gate, norm_g, w_out, ln_g, ln_b):
    b, s, d = x.shape
    c = GLA_CHUNK
    heads = GLA_HEADS
    dk = GLA_DK // heads
    dv = GLA_DV // heads
    x2 = x.reshape(b * s, d)
    w = w_in.astype(BF16)
    o0, o1, o2, o3 = GLA_DK, 2 * GLA_DK, 2 * GLA_DK + GLA_DV, 2 * GLA_DK + GLA_DV + GLA_GATE_RANK
    q, k, v, glr, r = _proj(
        x2, [w[:, :o0], w[:, o0:o1], w[:, o1:o2], _pad_cols(w[:, o2:o3], LANES), w[:, o3:]],
        [F32, F32, BF16, F32, F32], scales=(dk ** -0.5, 1.0, 1.0, 1.0, 1.0))
    wg2 = jnp.pad(w_gate2.astype(BF16), ((0, LANES - GLA_GATE_RANK), (0, 0)))
    dall, masks, nlev = _gla_tables(c)
    nck = GLA_STEP_CHUNKS
    tok = lambda width: pl.BlockSpec((1, nck * c, width), lambda i, j: (i, j, 0))
    o = pl.pallas_call(
        functools.partial(_gla_body, nlev=nlev, nck=nck),
        grid=(b, s // (nck * c)),
        in_specs=[
            tok(GLA_DK), tok(GLA_DK), tok(GLA_DV), tok(LANES), tok(GLA_DV),
            _full((LANES, GLA_DK)),
            _full((1, GLA_DK)),
            _full((1, dv)),
            _full(dall.shape),
            _full(masks.shape),
        ],
        out_specs=tok(GLA_DV),
        out_shape=jax.ShapeDtypeStruct((b, s, GLA_DV), BF16),
        scratch_shapes=[pltpu.VMEM((heads, dv, dk), F32)],
        compiler_params=_cparams("parallel", "arbitrary"),
        name="gla_core",
    )(q.reshape(b, s, -1), k.reshape(b, s, -1), v.reshape(b, s, -1), glr.reshape(b, s, -1),
      r.reshape(b, s, -1), wg2, b_gate.reshape(1, -1), norm_g.reshape(1, -1),
      jnp.asarray(dall, BF16), jnp.asarray(masks, F32))
    return _oproj_ln(o.reshape(b * s, GLA_DV), w_out, x2, ln_g, ln_b).reshape(b, s, d)


def _gelu_tanh(x):
    return 0.5 * x * (1.0 + jnp.tanh(np.sqrt(2.0 / np.pi).astype(np.float32) * (x + 0.044715 * (x * x * x))))


def _cmp_body(c_ref, plo_ref, phi_ref, w1a_ref, w1b_ref, w2_ref, o_ref):
    c = c_ref[0]
    a = _dot((c + plo_ref[...]).astype(BF16), w1a_ref[...])
    bm = _dot((c + phi_ref[...]).astype(BF16), w1b_ref[...])
    nxt = jnp.concatenate([bm[1:], jnp.zeros_like(bm[:1])], axis=0)
    row = lax.broadcasted_iota(jnp.int32, a.shape, 0)
    pre = jnp.where(row < a.shape[0] - 1, a + nxt, 0.0)
    o_ref[0] = _dot(_gelu_tanh(pre).astype(BF16), w2_ref[...]).astype(o_ref.dtype)


def _nsa_compress(kvc, cmp_pos, cmp_w1, cmp_w2):
    b, s, _ = kvc.shape
    st, g, d = NSA_CMP_STRIDE, NSA_KV_GROUPS, HEAD_DIM
    nch = s // st
    width = 2 * g * d
    c = kvc.reshape(b, nch, st * width)
    eye = jnp.eye(2 * g, dtype=F32)
    sel = jnp.repeat(jnp.eye(2, dtype=F32), g, axis=1)

    def expand(w1_half):
        wj = jnp.einsum('jlde,jc->lcde', w1_half, sel)
        return jnp.einsum('lcde,cf->lcdfe', wj, eye).reshape(st * width, width).astype(BF16)

    w1a = expand(cmp_w1[:, :st])
    w1b = expand(cmp_w1[:, st:])
    w2 = jnp.einsum('cde,cf->cdfe', jnp.einsum('jde,jc->cde', cmp_w2, sel), eye).reshape(width, width).astype(BF16)

    def pos_row(p_half):
        return jnp.einsum('jld,jc->lcd', p_half, sel).reshape(1, st * width)

    return pl.pallas_call(
        _cmp_body,
        grid=(b,),
        in_specs=[
            pl.BlockSpec((1, nch, st * width), lambda i: (i, 0, 0)),
            _full((1, st * width)),
            _full((1, st * width)),
            _full((st * width, width)),
            _full((st * width, width)),
            _full((width, width)),
        ],
        out_specs=pl.BlockSpec((1, nch, width), lambda i: (i, 0, 0)),
        out_shape=jax.ShapeDtypeStruct((b, nch, width), BF16),
        compiler_params=_cparams("parallel"),
        name="nsa_compress",
    )(c, pos_row(cmp_pos[:, :st]), pos_row(cmp_pos[:, st:]), w1a, w1b, w2)


def _nsa_overlap_t(s):
    n_cmp_pad = s // NSA_CMP_STRIDE
    n_sel = s // NSA_SEL_LEN
    blk = np.arange(n_cmp_pad) * NSA_CMP_STRIDE
    sel_start = np.arange(n_sel) * NSA_SEL_LEN
    ov = (blk[None, :] < sel_start[:, None] + NSA_SEL_LEN) & (blk[None, :] + NSA_CMP_LEN > sel_start[:, None])
    return ov.astype(np.float32)


NSA_MASK = 1.0e30


def _nsa_body(qt_ref, kc_ref, vct_ref, ks_ref, kw_ref, vt_ref, glt_ref, ovt_ref, o_ref, qt_scr, sa_scr, sb_scr, *, tq):
    d = HEAD_DIM
    groups = NSA_KV_GROUPS
    rep = N_HEADS // groups
    n_cmp = kc_ref.shape[2]
    n_sel = ovt_ref.shape[0]
    t = pl.program_id(1)
    q0 = t * tq
    win_tiles = NSA_WINDOW // tq
    gates = jax.nn.sigmoid(glt_ref[...])

    tpos = (q0 + lax.broadcasted_iota(jnp.int32, (1, tq), 1)).astype(F32)
    brow = lax.broadcasted_iota(jnp.int32, (ATT_BIAS_ROWS, tq), 0)
    ones_rows = jnp.where(brow == 0, 1.0, 0.0).astype(BF16)
    ci = lax.broadcasted_iota(jnp.int32, (tq, tq), 0)
    ai = lax.broadcasted_iota(jnp.int32, (tq, tq), 1)
    tile8 = lambda m: jnp.concatenate([m] * rep, axis=1)
    diag_add = tile8(jnp.where(ci <= ai, 0.0, NEG_INF))
    part_add = tile8(jnp.where((ci > ai) & (t >= win_tiles), 0.0, NEG_INF))
    ncm = lax.broadcasted_iota(jnp.int32, (n_cmp, tq), 0)
    tcm = q0 + lax.broadcasted_iota(jnp.int32, (n_cmp, tq), 1)
    valid_c = ncm * NSA_CMP_STRIDE + (NSA_CMP_LEN - 1) <= tcm
    jm = lax.broadcasted_iota(jnp.int32, (n_sel, tq), 0)
    cur = (q0 + lax.broadcasted_iota(jnp.int32, (n_sel, tq), 1)) // NSA_SEL_LEN
    causal_sel = jm <= cur
    forced = (jm == 0) | (jm == cur) | (jm == cur - 1)
    pad_rows = jnp.zeros((ATT_QROWS - d - ATT_BIAS_ROWS - n_sel, tq), BF16)

    o_cmp = []
    for g in range(groups):
        heads = [(qt_ref[h * d:(h + 1) * d, :], _alibi_rows(h, tpos, brow)) for h in range(g * rep, (g + 1) * rep)]

        def build(mask_rows):
            cols = [jnp.concatenate([qh, bias, mask_rows, pad_rows], axis=0) for qh, bias in heads]
            return jnp.concatenate(cols, axis=1)

        qt_scr[2 * g] = build(jnp.zeros((n_sel, tq), BF16))

        s_c = _dot(kc_ref[0, g], qt_scr[2 * g])
        p_sum = jnp.zeros((n_cmp, tq), F32)
        ps = []
        for r in range(rep):
            sr = jnp.where(valid_c, s_c[:, r * tq:(r + 1) * tq], NEG_INF)
            m = jnp.max(sr, axis=0, keepdims=True)
            e = jnp.where(valid_c, jnp.exp2(sr - m), 0.0)
            den = jnp.sum(e, axis=0, keepdims=True)
            p = e * jnp.where(den > 0.0, 1.0 / den, 0.0)
            p_sum = p_sum + p
            ps.append(p.astype(BF16))
        o_cmp.append(_dot(vct_ref[0, g * d:(g + 1) * d, :], jnp.concatenate(ps, axis=1)))

        p_hi = p_sum.astype(BF16)
        p_lo = (p_sum - p_hi.astype(F32)).astype(BF16)
        ovt = ovt_ref[...]
        imp_t = _dot(ovt, p_hi) + _dot(ovt, p_lo)
        score = jnp.where(causal_sel, imp_t + jnp.where(forced, NSA_FORCE_BONUS, 0.0), NEG_INF)
        rank = jnp.zeros((n_sel, tq), F32)
        for mp in range(n_sel):
            row = score[mp:mp + 1, :]
            ahead = (row > score) | ((row == score) & (mp < jm))
            rank = rank + jnp.where(ahead, 1.0, 0.0)
        selected = (rank < NSA_TOPK) & causal_sel
        qt_scr[2 * g + 1] = build(jnp.where(selected, 0.0, -NSA_MASK).astype(BF16))

    def score(chains, kt, slot):
        k0 = pl.multiple_of(kt * tq, tq)
        for ci, (k_ref, g, _, qi) in enumerate(chains):
            slot[ci] = _dot(k_ref[0, g, pl.ds(k0, tq), :], qt_scr[qi])

    def absorb(chains, kt, add, slot, carries):
        out = []
        for ci, ((_, _, v_row0, _), (m_prev, l_prev, acc)) in enumerate(zip(chains, carries)):
            s = slot[ci]
            if add is not None:
                s = s + add
            m_new = jnp.maximum(m_prev, jnp.max(s, axis=0, keepdims=True))
            e = jnp.exp2(s - m_new)
            alpha = jnp.exp2(m_prev - m_new)
            vals = jnp.concatenate([vt_ref[0, kt, v_row0:v_row0 + d, :], ones_rows], axis=0)
            pv = _dot(vals, e.astype(BF16))
            out.append((m_new, alpha * l_prev + pv[d:d + 1], alpha * acc + pv[:d]))
        return tuple(out)

    def sweep(chains, lo, carries):
        n = t - lo

        def pair(j, carries):
            a = lo + 2 * j
            score(chains, a + 1, sb_scr)
            carries = absorb(chains, a, None, sa_scr, carries)
            score(chains, a + 2, sa_scr)
            return absorb(chains, a + 1, None, sb_scr, carries)

        carries = lax.fori_loop(0, n // 2, pair, carries)

        def odd_tail(carries):
            score(chains, t, sb_scr)
            carries = absorb(chains, t - 1, None, sa_scr, carries)
            return absorb(chains, t, diag_add, sb_scr, carries)

        def even_tail(carries):
            return absorb(chains, t, diag_add, sa_scr, carries)

        carries = lax.cond(n % 2 == 1, odd_tail, even_tail, carries)
        return [acc / l for _, l, acc in carries]

    init = tuple((jnp.full((1, rep * tq), NEG_INF, F32), jnp.zeros((1, rep * tq), F32),
                  jnp.zeros((d, rep * tq), F32)) for _ in range(groups))

    sel_chains = [(ks_ref, g, g * d, 2 * g + 1) for g in range(groups)]
    score(sel_chains, 0, sa_scr)
    o_sel = sweep(sel_chains, 0, init)

    win_chains = [(kw_ref, g, (groups + g) * d, 2 * g) for g in range(groups)]
    oldest = jnp.maximum(t - win_tiles, 0)
    lo = jnp.maximum(t - win_tiles + 1, 0)
    score(win_chains, oldest, sb_scr)
    score(win_chains, jnp.minimum(lo, t), sa_scr)
    o_win = sweep(win_chains, lo, absorb(win_chains, oldest, part_add, sb_scr, init))

    for g in range(groups):
        for r in range(rep):
            h = g * rep + r
            cs = slice(r * tq, (r + 1) * tq)
            o = (gates[h:h + 1] * o_cmp[g][:, cs] + gates[N_HEADS + h:N_HEADS + h + 1] * o_sel[g][:, cs]
                 + gates[2 * N_HEADS + h:2 * N_HEADS + h + 1] * o_win[g][:, cs])
            o_ref[h * d:(h + 1) * d, :] = o.astype(o_ref.dtype)


def _nsa_mixer_ln(x, w_in, cmp_pos, cmp_w1, cmp_w2, w_out, ln_g, ln_b):
    b, s, d = x.shape
    hd = N_HEADS * HEAD_DIM
    groups = NSA_KV_GROUPS
    gd = groups * HEAD_DIM
    n = b * s
    tq = ATT_TQ
    n_sel = s // NSA_SEL_LEN
    n_cmp = s // NSA_CMP_STRIDE
    assert HEAD_DIM + ATT_BIAS_ROWS + n_sel <= ATT_QROWS and n_sel % 16 == 0 and NSA_WINDOW % tq == 0
    x2 = x.reshape(n, d)
    w = w_in.astype(BF16)
    c0 = hd + 2 * gd
    w_k2 = jnp.concatenate([w[:, c0:c0 + gd], w[:, c0 + 2 * gd:c0 + 3 * gd]], axis=1)
    w_v2 = jnp.concatenate([w[:, c0 + gd:c0 + 2 * gd], w[:, c0 + 3 * gd:c0 + 4 * gd]], axis=1)
    qt, kvc, k2, vt, glt = _proj_mixed(x2, [(w[:, :hd], "feat", BF16, HEAD_DIM ** -0.5 * LOG2E),
                                            (w[:, hd:c0], "tok", F32, 1.0),
                                            (w_k2, "tok", BF16, 1.0),
                                            (w_v2, "feat_tiles", BF16, 1.0),
                                            (_pad_cols(w[:, c0 + 4 * gd:], LANES), "feat", F32, 1.0)])

    cmp = _nsa_compress(kvc.reshape(b, s, 2 * gd), cmp_pos, cmp_w1, cmp_w2)
    tok = np.arange(s, dtype=np.float64)
    cpos = np.arange(n_cmp, dtype=np.float64) * NSA_CMP_STRIDE + (NSA_CMP_LEN - 1) / 2
    k2 = k2.reshape(b, s, 2 * gd)
    k_sel = _keys_with_tail(k2[:, :, :gd], groups, _key_tail(tok, n_sel, NSA_SEL_LEN))
    k_win = _keys_with_tail(k2[:, :, gd:], groups, _key_tail(tok))
    k_cmp = _keys_with_tail(cmp[:, :, :gd], groups, _key_tail(cpos))
    v_cmp_t = cmp[:, :, gd:].transpose(0, 2, 1)
    vt = vt.reshape(b, s // tq, 2 * gd, tq)
    ovt = _nsa_overlap_t(s)
    nt = s // tq
    o_t = pl.pallas_call(
        functools.partial(_nsa_body, tq=tq),
        grid=(b, nt),
        in_specs=[
            pl.BlockSpec((hd, tq), lambda i, j: (0, i * nt + j)),
            pl.BlockSpec((1, groups, n_cmp, ATT_QROWS), lambda i, j: (i, 0, 0, 0)),
            pl.BlockSpec((1, gd, n_cmp), lambda i, j: (i, 0, 0)),
            pl.BlockSpec((1, groups, s, ATT_QROWS), lambda i, j: (i, 0, 0, 0)),
            pl.BlockSpec((1, groups, s, ATT_QROWS), lambda i, j: (i, 0, 0, 0)),
            pl.BlockSpec((1, nt, 2 * gd, tq), lambda i, j: (i, 0, 0, 0)),
            pl.BlockSpec((LANES, tq), lambda i, j: (0, i * nt + j)),
            _full(ovt.shape),
        ],
        out_specs=pl.BlockSpec((hd, tq), lambda i, j: (0, i * nt + j)),
        out_shape=jax.ShapeDtypeStruct((hd, n), BF16),
        scratch_shapes=[pltpu.VMEM((2 * groups, ATT_QROWS, (N_HEADS // groups) * tq), BF16),
                        pltpu.VMEM((groups, tq, (N_HEADS // groups) * tq), F32),
                        pltpu.VMEM((groups, tq, (N_HEADS // groups) * tq), F32)],
        compiler_params=_cparams("parallel", "parallel"),
        name="nsa_core",
    )(qt, k_cmp, v_cmp_t, k_sel, k_win, vt, glt, jnp.asarray(ovt, BF16))
    return _oproj_ln(o_t, w_out, x2, ln_g, ln_b, feature_major=True).reshape(b, s, d)


def kernel(x, mem, ln_g, ln_b, ffn1_w_gate, ffn1_w_up, ffn1_w_down, ffn2_w_gate, ffn2_w_up, ffn2_w_down,
           xattn_w_q, xattn_w_kv, xattn_w_o, swa_w_in, swa_sinks, swa_w_out, nsa_w_in, nsa_cmp_pos,
           nsa_cmp_w1, nsa_cmp_w2, nsa_w_out, gla_w_in, gla_w_gate2, gla_b_gate, gla_norm_g, gla_w_out,
           pool_w, pool_scale):
    b, s, d = x.shape
    for i in range(DEPTH):
        kind = i % N_MIXERS
        j = i // N_MIXERS
        x = _ffn_ln(x.reshape(b * s, d), ffn1_w_gate[i], ffn1_w_up[i], ffn1_w_down[i],
                    ln_g[i, 0], ln_b[i, 0]).reshape(b, s, d)
        if kind == 0:
            x = _swa_mixer_ln(x, swa_w_in[j], swa_sinks[j], swa_w_out[j], ln_g[i, 1], ln_b[i, 1])
        elif kind == 1:
            x = _nsa_mixer_ln(x, nsa_w_in[j], nsa_cmp_pos[j], nsa_cmp_w1[j], nsa_cmp_w2[j], nsa_w_out[j],
                              ln_g[i, 1], ln_b[i, 1])
        elif kind == 2:
            x = _gla_mixer_ln(x, gla_w_in[j], gla_w_gate2[j], gla_b_gate[j], gla_norm_g[j], gla_w_out[j],
                              ln_g[i, 1], ln_b[i, 1])
        else:
            x = _pool_mixer_ln(x, pool_w[j], pool_scale[j], ln_g[i, 1], ln_b[i, 1])
        x = _xattn_ln(x, mem, xattn_w_q[i], xattn_w_kv[i], xattn_w_o[i], ln_g[i, 2], ln_b[i, 2])
        x = _ffn_ln(x.reshape(b * s, d), ffn2_w_gate[i], ffn2_w_up[i], ffn2_w_down[i],
                    ln_g[i, 3], ln_b[i, 3]).reshape(b, s, d)
    return x
```

```python
import functools

import numpy as np
import jax
import jax.numpy as jnp
from jax import lax
from jax.experimental import pallas as pl
from jax.experimental.pallas import tpu as pltpu

F32 = jnp.float32
BF16 = jnp.bfloat16

D_MODEL = 1024
DEPTH = 4
N_MIXERS = 4
HEAD_DIM = 64
N_HEADS = D_MODEL // HEAD_DIM
SWA_KV_HEADS = 4
SWA_WINDOW = 128
NSA_KV_GROUPS = 2
NSA_CMP_LEN = 32
NSA_CMP_STRIDE = 16
NSA_SEL_LEN = 64
NSA_TOPK = 8
NSA_WINDOW = 512
NSA_FORCE_BONUS = 1.0e4
GLA_HEADS = 4
GLA_DK = D_MODEL // 2
GLA_DV = D_MODEL
GLA_GATE_RANK = 16
GLA_TAU = 16.0
GLA_CHUNK = 64
POOL_WINDOWS = (2, 4, 8, 16)
POOL_GROUP = D_MODEL // 4
XATTN_HEADS = 4
DN_ALPHA = (2 * DEPTH) ** 0.25
LN_EPS = 1e-5
NEG_INF = -1e30
LOG2E = float(np.log2(np.e))

LANES = 128
V7X_VMEM_LIMIT_BYTES = 56 * 1024 * 1024

_NT = (((1,), (1,)), ((), ()))
_TN = (((0,), (0,)), ((), ()))


def _cparams(*sem, flags=None):
    return pltpu.CompilerParams(dimension_semantics=sem, vmem_limit_bytes=V7X_VMEM_LIMIT_BYTES, flags=flags)


def _dot(a, b):
    return jnp.dot(a, b, preferred_element_type=F32)


def _dot_nt(a, b):
    return lax.dot_general(a, b, _NT, preferred_element_type=F32)


def _dot_tn(a, b):
    return lax.dot_general(a, b, _TN, preferred_element_type=F32)


def _layer_norm(y, g, b):
    mu = jnp.mean(y, axis=-1, keepdims=True)
    yc = y - mu
    var = jnp.mean(yc * yc, axis=-1, keepdims=True)
    return yc * lax.rsqrt(var + LN_EPS) * g + b


def _alibi_slope(h, n):
    return float(2.0 ** (-8.0 * (h + 1) / n))


def _full(shape):
    nd = len(shape)
    return pl.BlockSpec(shape, lambda *_: (0,) * nd, pipeline_mode=pl.Buffered(1))


FFN_TM = 1024
FFN_SUB = 512
FFN_TF = 256


def _ffn_body(x_ref, wg_ref, wu_ref, wd_ref, g_ref, b_ref, o_ref, acc_ref, *, nf, sub):
    for si in range(x_ref.shape[0] // sub):
        rows = slice(si * sub, (si + 1) * sub)
        x = x_ref[rows, :]
        xb = x.astype(BF16)
        for c in range(nf):
            gate = _dot(xb, wg_ref[c])
            up = _dot(xb, wu_ref[c])
            h = (gate * jax.nn.sigmoid(gate) * up).astype(BF16)
            d = _dot(h, wd_ref[c])
            if c == 0:
                acc_ref[si] = d
            else:
                acc_ref[si] += d
        y = DN_ALPHA * x + 0.5 * acc_ref[si]
        o_ref[rows, :] = _layer_norm(y, g_ref[...], b_ref[...])


def _ffn_ln(x2, w_gate, w_up, w_down, ln_g, ln_b):
    n, d = x2.shape
    f = w_gate.shape[1]
    nf = f // FFN_TF
    wg = w_gate.astype(BF16).reshape(d, nf, FFN_TF).transpose(1, 0, 2)
    wu = w_up.astype(BF16).reshape(d, nf, FFN_TF).transpose(1, 0, 2)
    wd = w_down.astype(BF16).reshape(nf, FFN_TF, d)
    tm = min(FFN_TM, n)
    sub = min(FFN_SUB, tm)
    return pl.pallas_call(
        functools.partial(_ffn_body, nf=nf, sub=sub),
        grid=(n // tm,),
        in_specs=[
            pl.BlockSpec((tm, d), lambda i: (i, 0)),
            _full((nf, d, FFN_TF)),
            _full((nf, d, FFN_TF)),
            _full((nf, FFN_TF, d)),
            _full((1, d)),
            _full((1, d)),
        ],
        out_specs=pl.BlockSpec((tm, d), lambda i: (i, 0)),
        out_shape=jax.ShapeDtypeStruct((n, d), F32),
        scratch_shapes=[pltpu.VMEM((tm // sub, sub, d), F32)],
        compiler_params=_cparams("parallel"),
        name="ffn_ln",
    )(x2, wg, wu, wd, ln_g.reshape(1, d), ln_b.reshape(1, d))


PROJ_TM = 512


def _proj_body(x_ref, *refs, scales):
    n = len(scales)
    xb = x_ref[...].astype(BF16)
    for w_ref, o_ref, s in zip(refs[:n], refs[n:], scales):
        r = _dot(xb, w_ref[...])
        if s != 1.0:
            r = r * s
        o_ref[...] = r.astype(o_ref.dtype)


def _proj(x2, weights, dtypes, scales=None):
    n, k = x2.shape
    scales = tuple(scales) if scales is not None else (1.0,) * len(weights)
    tm = min(PROJ_TM, n)
    in_specs = [pl.BlockSpec((tm, k), lambda i: (i, 0))] + [_full(w.shape) for w in weights]
    out_specs = [pl.BlockSpec((tm, w.shape[1]), lambda i: (i, 0)) for w in weights]
    out_shape = [jax.ShapeDtypeStruct((n, w.shape[1]), dt) for w, dt in zip(weights, dtypes)]
    return pl.pallas_call(
        functools.partial(_proj_body, scales=scales),
        grid=(n // tm,),
        in_specs=in_specs,
        out_specs=out_specs,
        out_shape=out_shape,
        compiler_params=_cparams("parallel"),
        name="proj",
    )(x2, *weights)


def _pad_cols(w, m):
    return jnp.pad(w, ((0, 0), (0, m - w.shape[1])))


def _oproj_body(o_ref, w_ref, x_ref, g_ref, b_ref, out_ref, *, feature_major):
    y = _dot_tn(o_ref[...], w_ref[...]) if feature_major else _dot(o_ref[...], w_ref[...])
    out_ref[...] = _layer_norm(DN_ALPHA * x_ref[...] + y, g_ref[...], b_ref[...])


def _oproj_ln(o2, w_out, x2, ln_g, ln_b, feature_major=False):
    n, d = x2.shape
    k = w_out.shape[0]
    tm = min(PROJ_TM, n)
    return pl.pallas_call(
        functools.partial(_oproj_body, feature_major=feature_major),
        grid=(n // tm,),
        in_specs=[
            pl.BlockSpec((k, tm), lambda i: (0, i)) if feature_major else pl.BlockSpec((tm, k), lambda i: (i, 0)),
            _full((k, d)),
            pl.BlockSpec((tm, d), lambda i: (i, 0)),
            _full((1, d)),
            _full((1, d)),
        ],
        out_specs=pl.BlockSpec((tm, d), lambda i: (i, 0)),
        out_shape=jax.ShapeDtypeStruct((n, d), F32),
        compiler_params=_cparams("parallel"),
        name="oproj_ln",
    )(o2, w_out.astype(BF16), x2, ln_g.reshape(1, d), ln_b.reshape(1, d))


XATTN_TQ = 1024


def _xattn_body(x_ref, kv_ref, wq_ref, wo_ref, g_ref, b_ref, o_ref, *, heads):
    x = x_ref[0]
    d = x.shape[-1]
    dh = d // heads
    q = (_dot(x.astype(BF16), wq_ref[...]) * dh ** -0.5).astype(BF16)
    scores = [_dot_nt(q[:, h * dh:(h + 1) * dh], kv_ref[0, :, h * dh:(h + 1) * dh]) for h in range(heads)]
    outs = []
    for h in range(heads):
        vh = kv_ref[0, :, d + h * dh:d + (h + 1) * dh]
        s = scores[h]
        m = jnp.max(s, axis=-1, keepdims=True)
        e = jnp.exp(s - m)
        p = e / jnp.sum(e, axis=-1, keepdims=True)
        outs.append(_dot(p.astype(BF16), vh).astype(BF16))
    o = jnp.concatenate(outs, axis=-1)
    y = _dot(o, wo_ref[...])
    o_ref[0] = _layer_norm(DN_ALPHA * x + y, g_ref[...], b_ref[...])


def _xattn_ln(x, mem, w_q, w_kv, w_o, ln_g, ln_b):
    b, s, d = x.shape
    m = mem.shape[1]
    (kv,) = _proj(mem.reshape(b * m, d), [w_kv.astype(BF16)], [BF16])
    kv = kv.reshape(b, m, 2 * d)
    tq = min(XATTN_TQ, s)
    return pl.pallas_call(
        functools.partial(_xattn_body, heads=XATTN_HEADS),
        grid=(b, s // tq),
        in_specs=[
            pl.BlockSpec((1, tq, d), lambda i, j: (i, j, 0)),
            pl.BlockSpec((1, m, 2 * d), lambda i, j: (i, 0, 0)),
            _full((d, d)),
            _full((d, d)),
            _full((1, d)),
            _full((1, d)),
        ],
        out_specs=pl.BlockSpec((1, tq, d), lambda i, j: (i, j, 0)),
        out_shape=jax.ShapeDtypeStruct((b, s, d), F32),
        compiler_params=_cparams("parallel", "parallel"),
        name="xattn_ln",
    )(x, kv, w_q.astype(BF16), w_o.astype(BF16), ln_g.reshape(1, d), ln_b.reshape(1, d))


ATT_TQ = 128
ATT_QROWS = 128
ATT_BIAS_ROWS = 16


def _split3_bf16(x):
    out = []
    r = np.float32(x)
    for _ in range(3):
        p = np.float32(np.asarray(r, np.float32).astype(BF16).astype(np.float32))
        out.append(float(p))
        r = np.float32(r - p)
    return out


def _alibi_rows(h, tpos, brow):
    slope = _alibi_slope(h, N_HEADS) * LOG2E
    pieces = _split3_bf16(slope * LANES) + _split3_bf16(slope)
    v = -np.float32(slope) * tpos
    v_hi = v.astype(BF16).astype(F32)
    v_mid = (v - v_hi).astype(BF16).astype(F32)
    v_lo = v - v_hi - v_mid
    bias = jnp.zeros(brow.shape, F32)
    for k, val in enumerate(pieces + [v_hi, v_mid, v_lo]):
        bias = jnp.where(brow == k, val, bias)
    return bias.astype(BF16)


def _key_tail(pos, n_blocks=0, block_len=1):
    n = pos.shape[0]
    hi = np.floor(pos / LANES)
    lo = pos - hi * LANES
    tail = np.zeros((n, ATT_QROWS - HEAD_DIM), np.float32)
    tail[:, 0:3] = hi[:, None]
    tail[:, 3:6] = lo[:, None]
    tail[:, 6:9] = 1.0
    if n_blocks:
        blk = (pos // block_len).astype(np.int64)
        tail[np.arange(n), ATT_BIAS_ROWS + blk] = 1.0
    return tail


def _keys_with_tail(k_tok, groups, tail):
    b, rows, _ = k_tok.shape
    k = k_tok.reshape(b, rows, groups, HEAD_DIM).transpose(0, 2, 1, 3)
    t_b = jnp.broadcast_to(jnp.asarray(tail, BF16), (b, groups, rows, tail.shape[1]))
    return jnp.concatenate([k, t_b], axis=-1)


def _proj_mixed_body(x_ref, *refs, kinds, scales, tq):
    n = len(kinds)
    xb = x_ref[...].astype(BF16)
    for w_ref, o_ref, kind, s in zip(refs[:n], refs[n:], kinds, scales):
        r = _dot(xb, w_ref[...]) if kind == "tok" else _dot_nt(w_ref[...], xb)
        if s != 1.0:
            r = r * s
        r = r.astype(o_ref.dtype)
        if kind == "feat_tiles":
            for c in range(o_ref.shape[0]):
                o_ref[c] = r[:, c * tq:(c + 1) * tq]
        else:
            o_ref[...] = r


def _proj_mixed(x2, specs, tq=ATT_TQ):
    n, k = x2.shape
    tm = min(PROJ_TM, n)
    kinds = tuple(s[1] for s in specs)
    ws, out_specs, out_shape = [], [], []
    for w, kind, dt, _ in specs:
        m = w.shape[1]
        if kind == "tok":
            ws.append(w)
            out_specs.append(pl.BlockSpec((tm, m), lambda i: (i, 0)))
            out_shape.append(jax.ShapeDtypeStruct((n, m), dt))
        elif kind == "feat":
            ws.append(w.T)
            out_specs.append(pl.BlockSpec((m, tm), lambda i: (0, i)))
            out_shape.append(jax.ShapeDtypeStruct((m, n), dt))
        else:
            ws.append(w.T)
            out_specs.append(pl.BlockSpec((tm // tq, m, tq), lambda i: (i, 0, 0)))
            out_shape.append(jax.ShapeDtypeStruct((n // tq, m, tq), dt))
    return pl.pallas_call(
        functools.partial(_proj_mixed_body, kinds=kinds, scales=tuple(s[3] for s in specs), tq=tq),
        grid=(n // tm,),
        in_specs=[pl.BlockSpec((tm, k), lambda i: (i, 0))] + [_full(w.shape) for w in ws],
        out_specs=out_specs,
        out_shape=out_shape,
        compiler_params=_cparams("parallel"),
        name="proj_mixed",
    )(x2, *ws)


SWA_CHAIN_HEADS = 4
SWA_LOOKAHEAD = 2


def _swa_body(sink_ref, qt_ref, k_ref, vt_ref, o_ref, *, tq):
    d = HEAD_DIM
    groups = SWA_KV_HEADS
    rep = N_HEADS // groups
    t = pl.program_id(1)
    prev = jnp.maximum(t - 1, 0)
    tpos = (t * tq + lax.broadcasted_iota(jnp.int32, (1, tq), 1)).astype(F32)
    brow = lax.broadcasted_iota(jnp.int32, (ATT_BIAS_ROWS, tq), 0)
    ones_rows = jnp.where(brow == 0, 1.0, 0.0).astype(BF16)
    ci = lax.broadcasted_iota(jnp.int32, (tq, tq), 0)
    ai = lax.broadcasted_iota(jnp.int32, (tq, tq), 1)
    hb = SWA_CHAIN_HEADS
    tile_r = lambda m: jnp.concatenate([m] * hb, axis=1)
    diag_add = tile_r(jnp.where(ci <= ai, 0.0, NEG_INF))
    prev_add = tile_r(jnp.where((ci > ai) & (t > 0), 0.0, NEG_INF))
    pad_rows = jnp.zeros((ATT_QROWS - d - ATT_BIAS_ROWS, tq), BF16)
    k0 = pl.multiple_of(t * tq, tq)
    kp = pl.multiple_of(prev * tq, tq)

    def issue(c):
        g = (c * hb) // rep
        qt = jnp.concatenate(
            [jnp.concatenate([qt_ref[h * d:(h + 1) * d, :], _alibi_rows(h, tpos, brow), pad_rows], axis=0)
             for h in range(c * hb, (c + 1) * hb)], axis=1)
        return _dot(k_ref[0, g, pl.ds(kp, tq), :], qt), _dot(k_ref[0, g, pl.ds(k0, tq), :], qt)

    def finish(c, sc):
        g = (c * hb) // rep
        s_prev = sc[0] + prev_add
        s_diag = sc[1] + diag_add
        sink = jnp.concatenate([jnp.full((1, tq), sink_ref[h] * LOG2E, F32) for h in range(c * hb, (c + 1) * hb)], axis=1)
        m = jnp.maximum(jnp.maximum(jnp.max(s_prev, axis=0, keepdims=True),
                                    jnp.max(s_diag, axis=0, keepdims=True)), sink)
        e_prev = jnp.exp2(s_prev - m)
        e_diag = jnp.exp2(s_diag - m)
        acc = (_dot(jnp.concatenate([vt_ref[0, prev, g * d:(g + 1) * d, :], ones_rows], axis=0), e_prev.astype(BF16))
               + _dot(jnp.concatenate([vt_ref[0, t, g * d:(g + 1) * d, :], ones_rows], axis=0), e_diag.astype(BF16)))
        o = acc[:d] / (acc[d:d + 1] + jnp.exp2(sink - m))
        for r in range(hb):
            h = c * hb + r
            o_ref[h * d:(h + 1) * d, :] = o[:, r * tq:(r + 1) * tq].astype(o_ref.dtype)

    n_chains = N_HEADS // hb
    pending = [issue(c) for c in range(min(SWA_LOOKAHEAD, n_chains))]
    for c in range(n_chains):
        finish(c, pending[c])
        if c + SWA_LOOKAHEAD < n_chains:
            pending.append(issue(c + SWA_LOOKAHEAD))


def _swa_mixer_ln(x, w_in, sinks, w_out, ln_g, ln_b):
    b, s, d = x.shape
    hd = N_HEADS * HEAD_DIM
    groups = SWA_KV_HEADS
    gd = groups * HEAD_DIM
    n = b * s
    tq = ATT_TQ
    nt = s // tq
    assert SWA_WINDOW == tq
    x2 = x.reshape(n, d)
    w = w_in.astype(BF16)
    qt, k, vt = _proj_mixed(x2, [(w[:, :hd], "feat", BF16, HEAD_DIM ** -0.5 * LOG2E),
                                 (w[:, hd:hd + gd], "tok", BF16, 1.0),
                                 (w[:, hd + gd:], "feat_tiles", BF16, 1.0)])
    kk = _keys_with_tail(k.reshape(b, s, gd), groups, _key_tail(np.arange(s, dtype=np.float64)))
    o_t = pl.pallas_call(
        functools.partial(_swa_body, tq=tq),
        grid=(b, nt),
        in_specs=[
            pl.BlockSpec(memory_space=pltpu.SMEM),
            pl.BlockSpec((hd, tq), lambda i, j: (0, i * nt + j)),
            pl.BlockSpec((1, groups, s, ATT_QROWS), lambda i, j: (i, 0, 0, 0)),
            pl.BlockSpec((1, nt, gd, tq), lambda i, j: (i, 0, 0, 0)),
        ],
        out_specs=pl.BlockSpec((hd, tq), lambda i, j: (0, i * nt + j)),
        out_shape=jax.ShapeDtypeStruct((hd, n), BF16),
        compiler_params=_cparams("parallel", "parallel"),
        name="swa_core",
    )(sinks.astype(F32), qt, kk, vt.reshape(b, nt, gd, tq))
    return _oproj_ln(o_t, w_out, x2, ln_g, ln_b, feature_major=True).reshape(b, s, d)


POOL_TM = 512
POOL_HALO = 16


def _pool_body(x_ref, halo_ref, w_ref, sc_ref, g_ref, b_ref, o_ref, ext_ref, *, tm):
    t = pl.program_id(1)
    x = x_ref[0]
    ext_ref[0:POOL_HALO, :] = jnp.where(t > 0, halo_ref[0], 0.0)
    ext_ref[POOL_HALO:, :] = x
    row = t * tm + lax.broadcasted_iota(jnp.int32, (tm, 1), 0)
    ys = []
    for gi, w in enumerate(POOL_WINDOWS):
        cs = slice(gi * POOL_GROUP, (gi + 1) * POOL_GROUP)
        xg = x[:, cs]
        acc = xg
        for k in range(1, w):
            acc = acc + ext_ref[POOL_HALO - k:POOL_HALO - k + tm, cs]
        cnt = jnp.minimum(row + 1, w).astype(F32)
        diff = (acc / cnt - xg).astype(BF16)
        ys.append(_dot(diff, w_ref[gi]))
    y = jnp.concatenate(ys, axis=-1) * sc_ref[...]
    o_ref[0] = _layer_norm(DN_ALPHA * x + y, g_ref[...], b_ref[...])


def _pool_mixer_ln(x, w_grp, scale, ln_g, ln_b):
    b, s, d = x.shape
    tm = min(POOL_TM, s)
    hb = tm // POOL_HALO
    ng = len(POOL_WINDOWS)
    return pl.pallas_call(
        functools.partial(_pool_body, tm=tm),
        grid=(b, s // tm),
        in_specs=[
            pl.BlockSpec((1, tm, d), lambda i, j: (i, j, 0)),
            pl.BlockSpec((1, POOL_HALO, d), lambda i, j: (i, jnp.maximum(j * hb - 1, 0), 0)),
            _full((ng, POOL_GROUP, POOL_GROUP)),
            _full((1, d)),
            _full((1, d)),
            _full((1, d)),
        ],
        out_specs=pl.BlockSpec((1, tm, d), lambda i, j: (i, j, 0)),
        out_shape=jax.ShapeDtypeStruct((b, s, d), F32),
        scratch_shapes=[pltpu.VMEM((tm + POOL_HALO, d), F32)],
        compiler_params=_cparams("parallel", "parallel"),
        name="pool_ln",
    )(x, x, w_grp.astype(BF16), scale.reshape(1, d), ln_g.reshape(1, d), ln_b.reshape(1, d))


GLA_STEP_CHUNKS = 4


def _gla_tables(c):
    levels = []
    s = c
    while s >= 1:
        levels.append(s)
        s //= 2
    rows, masks = [], []
    idx = np.arange(c)
    for s in levels:
        dq = np.zeros((c, c), np.float32)
        dk = np.zeros((c, c), np.float32)
        for i in range(c):
            blk = i // s
            if s == c or blk % 2 == 1:
                dq[i, blk * s:i + 1] = 1.0
            dk[i, i + 1:blk * s + s] = 1.0
        rows.append(dq)
        if s > 1:
            rows.append(dk)
        if s < c:
            masks.append(((idx[:, None] // (2 * s) == idx[None, :] // (2 * s))
                          & ((idx[:, None] // s) % 2 == 1) & ((idx[None, :] // s) % 2 == 0)))
    masks.append(np.eye(c, dtype=bool))
    return np.concatenate(rows, 0), np.stack(masks).astype(np.float32), len(levels)


def _gla_body(q_ref, k_ref, v_ref, glr_ref, r_ref, wg2_ref, bg_ref, ng_ref, dall_ref, mask_ref,
              o_ref, st_ref, *, nlev, nck):
    c = GLA_CHUNK
    heads = GLA_HEADS
    dk = GLA_DK // heads
    dv = GLA_DV // heads

    @pl.when(pl.program_id(1) == 0)
    def _():
        st_ref[...] = jnp.zeros_like(st_ref)

    z = _dot(glr_ref[0].astype(BF16), wg2_ref[...]) + bg_ref[...]
    log_a = (jnp.minimum(z, 0.0) - jnp.log1p(jnp.exp(-jnp.abs(z)))) * (1.0 / GLA_TAU)
    hi = log_a.astype(BF16)
    lo = (log_a - hi.astype(F32)).astype(BF16)
    parts = []
    for ck in range(nck):
        parts += [hi[ck * c:(ck + 1) * c], lo[ck * c:(ck + 1) * c]]
    e2 = _dot(dall_ref[...], jnp.concatenate(parts, axis=-1))
    q = q_ref[0]
    k = k_ref[0]
    pre = []
    for ck in range(nck):
        rows = slice(ck * c, (ck + 1) * c)
        decay = jnp.exp(e2[:, 2 * ck * GLA_DK:(2 * ck + 1) * GLA_DK]
                        + e2[:, (2 * ck + 1) * GLA_DK:(2 * ck + 2) * GLA_DK])
        for h in range(heads):
            ks = slice(h * dk, (h + 1) * dk)
            qh = q[rows, ks]
            kh = k[rows, ks]
            tbl = lambda i: decay[i * c:(i + 1) * c, ks]
            q_in = (qh * tbl(0)).astype(BF16)
            k_out = (kh * tbl(1)).astype(BF16)
            khb = kh.astype(BF16)
            att = mask_ref[nlev - 1] * _dot_nt(qh.astype(BF16), khb)
            for li in range(nlev - 1):
                ql = (qh * tbl(2 + 2 * li)).astype(BF16)
                kl = (kh * tbl(3 + 2 * li)).astype(BF16) if li < nlev - 2 else khb
                att = att + mask_ref[li] * _dot_nt(ql, kl)
            total = decay[c - 1:c, ks]
            pre.append((att.astype(BF16), q_in, k_out, total))
    states = [st_ref[h] for h in range(heads)]
    r = r_ref[0]
    gate = r * jax.nn.sigmoid(r)
    for ck in range(nck):
        rows = slice(ck * c, (ck + 1) * c)
        outs = []
        for h in range(heads):
            att, q_in, k_out, total = pre[ck * heads + h]
            vh = v_ref[0, rows, h * dv:(h + 1) * dv]
            out = _dot(att, vh) + _dot_nt(q_in, states[h].astype(BF16))
            states[h] = states[h] * total + _dot_tn(vh, k_out)
            outs.append(out * lax.rsqrt(jnp.mean(out * out, axis=-1, keepdims=True) + LN_EPS) * ng_ref[...])
        o_ref[0, rows, :] = (jnp.concatenate(outs, axis=-1) * gate[rows]).astype(o_ref.dtype)
    for h in range(heads):
        st_ref[h] = states[h]


def _gla_mixer_ln(x, w_in, w_gate2, b_gate, norm_g, w_out, ln_g, ln_b):
    b, s, d = x.shape
    c = GLA_CHUNK
    heads = GLA_HEADS
    dk = GLA_DK // heads
    dv = GLA_DV // heads
    x2 = x.reshape(b * s, d)
    w = w_in.astype(BF16)
    o0, o1, o2, o3 = GLA_DK, 2 * GLA_DK, 2 * GLA_DK + GLA_DV, 2 * GLA_DK + GLA_DV + GLA_GATE_RANK
    q, k, v, glr, r = _proj(
        x2, [w[:, :o0], w[:, o0:o1], w[:, o1:o2], _pad_cols(w[:, o2:o3], LANES), w[:, o3:]],
        [F32, F32, BF16, F32, F32], scales=(dk ** -0.5, 1.0, 1.0, 1.0, 1.0))
    wg2 = jnp.pad(w_gate2.astype(BF16), ((0, LANES - GLA_GATE_RANK), (0, 0)))
    dall, masks, nlev = _gla_tables(c)
    nck = GLA_STEP_CHUNKS
    tok = lambda width: pl.BlockSpec((1, nck * c, width), lambda i, j: (i, j, 0))
    o = pl.pallas_call(
        functools.partial(_gla_body, nlev=nlev, nck=nck),
        grid=(b, s // (nck * c)),
        in_specs=[
            tok(GLA_DK), tok(GLA_DK), tok(GLA_DV), tok(LANES), tok(GLA_DV),
            _full((LANES, GLA_DK)),
            _full((1, GLA_DK)),
            _full((1, dv)),
            _full(dall.shape),
            _full(masks.shape),
        ],
        out_specs=tok(GLA_DV),
        out_shape=jax.ShapeDtypeStruct((b, s, GLA_DV), BF16),
        scratch_shapes=[pltpu.VMEM((heads, dv, dk), F32)],
        compiler_params=_cparams("parallel", "arbitrary"),
        name="gla_core",
    )(q.reshape(b, s, -1), k.reshape(b, s, -1), v.reshape(b, s, -1), glr.reshape(b, s, -1),
      r.reshape(b, s, -1), wg2, b_gate.reshape(1, -1), norm_g.reshape(1, -1),
      jnp.asarray(dall, BF16), jnp.asarray(masks, F32))
    return _oproj_ln(o.reshape(b * s, GLA_DV), w_out, x2, ln_g, ln_b).reshape(b, s, d)


def _gelu_tanh(x):
    return 0.5 * x * (1.0 + jnp.tanh(np.sqrt(2.0 / np.pi).astype(np.float32) * (x + 0.044715 * (x * x * x))))


def _cmp_body(c_ref, plo_ref, phi_ref, w1a_ref, w1b_ref, w2_ref, o_ref):
    c = c_ref[0]
    a = _dot((c + plo_ref[...]).astype(BF16), w1a_ref[...])
    bm = _dot((c + phi_ref[...]).astype(BF16), w1b_ref[...])
    nxt = jnp.concatenate([bm[1:], jnp.zeros_like(bm[:1])], axis=0)
    row = lax.broadcasted_iota(jnp.int32, a.shape, 0)
    pre = jnp.where(row < a.shape[0] - 1, a + nxt, 0.0)
    o_ref[0] = _dot(_gelu_tanh(pre).astype(BF16), w2_ref[...]).astype(o_ref.dtype)


def _nsa_compress(kvc, cmp_pos, cmp_w1, cmp_w2):
    b, s, _ = kvc.shape
    st, g, d = NSA_CMP_STRIDE, NSA_KV_GROUPS, HEAD_DIM
    nch = s // st
    width = 2 * g * d
    c = kvc.reshape(b, nch, st * width)
    eye = jnp.eye(2 * g, dtype=F32)
    sel = jnp.repeat(jnp.eye(2, dtype=F32), g, axis=1)

    def expand(w1_half):
        wj = jnp.einsum('jlde,jc->lcde', w1_half, sel)
        return jnp.einsum('lcde,cf->lcdfe', wj, eye).reshape(st * width, width).astype(BF16)

    w1a = expand(cmp_w1[:, :st])
    w1b = expand(cmp_w1[:, st:])
    w2 = jnp.einsum('cde,cf->cdfe', jnp.einsum('jde,jc->cde', cmp_w2, sel), eye).reshape(width, width).astype(BF16)

    def pos_row(p_half):
        return jnp.einsum('jld,jc->lcd', p_half, sel).reshape(1, st * width)

    return pl.pallas_call(
        _cmp_body,
        grid=(b,),
        in_specs=[
            pl.BlockSpec((1, nch, st * width), lambda i: (i, 0, 0)),
            _full((1, st * width)),
            _full((1, st * width)),
            _full((st * width, width)),
            _full((st * width, width)),
            _full((width, width)),
        ],
        out_specs=pl.BlockSpec((1, nch, width), lambda i: (i, 0, 0)),
        out_shape=jax.ShapeDtypeStruct((b, nch, width), BF16),
        compiler_params=_cparams("parallel"),
        name="nsa_compress",
    )(c, pos_row(cmp_pos[:, :st]), pos_row(cmp_pos[:, st:]), w1a, w1b, w2)


def _nsa_overlap_t(s):
    n_cmp_pad = s // NSA_CMP_STRIDE
    n_sel = s // NSA_SEL_LEN
    blk = np.arange(n_cmp_pad) * NSA_CMP_STRIDE
    sel_start = np.arange(n_sel) * NSA_SEL_LEN
    ov = (blk[None, :] < sel_start[:, None] + NSA_SEL_LEN) & (blk[None, :] + NSA_CMP_LEN > sel_start[:, None])
    return ov.astype(np.float32)


NSA_MASK = 1.0e30


def _nsa_body(qt_ref, kc_ref, vct_ref, ks_ref, kw_ref, vt_ref, glt_ref, ovt_ref, o_ref, qt_scr, sa_scr, sb_scr,
              m_scr, acc_scr, *, tq):
    d = HEAD_DIM
    groups = NSA_KV_GROUPS
    rep = N_HEADS // groups
    n_cmp = kc_ref.shape[2]
    n_sel = ovt_ref.shape[0]
    t = pl.program_id(1)
    q0 = t * tq
    win_tiles = NSA_WINDOW // tq
    gates = jax.nn.sigmoid(glt_ref[...])

    tpos = (q0 + lax.broadcasted_iota(jnp.int32, (1, tq), 1)).astype(F32)
    brow = lax.broadcasted_iota(jnp.int32, (ATT_BIAS_ROWS, tq), 0)
    ones_rows = jnp.where(brow == 0, 1.0, 0.0).astype(BF16)
    ci = lax.broadcasted_iota(jnp.int32, (tq, tq), 0)
    ai = lax.broadcasted_iota(jnp.int32, (tq, tq), 1)
    tile8 = lambda m: jnp.concatenate([m] * rep, axis=1)
    diag_add = tile8(jnp.where(ci <= ai, 0.0, NEG_INF))
    part_add = tile8(jnp.where((ci > ai) & (t >= win_tiles), 0.0, NEG_INF))
    ncm = lax.broadcasted_iota(jnp.int32, (n_cmp, tq), 0)
    tcm = q0 + lax.broadcasted_iota(jnp.int32, (n_cmp, tq), 1)
    valid_c = ncm * NSA_CMP_STRIDE + (NSA_CMP_LEN - 1) <= tcm
    jm = lax.broadcasted_iota(jnp.int32, (n_sel, tq), 0)
    cur = (q0 + lax.broadcasted_iota(jnp.int32, (n_sel, tq), 1)) // NSA_SEL_LEN
    causal_sel = jm <= cur
    forced = (jm == 0) | (jm == cur) | (jm == cur - 1)
    pad_rows = jnp.zeros((ATT_QROWS - d - ATT_BIAS_ROWS - n_sel, tq), BF16)

    o_cmp = []
    for g in range(groups):
        heads = [(qt_ref[h * d:(h + 1) * d, :], _alibi_rows(h, tpos, brow)) for h in range(g * rep, (g + 1) * rep)]

        def build(mask_rows):
            cols = [jnp.concatenate([qh, bias, mask_rows, pad_rows], axis=0) for qh, bias in heads]
            return jnp.concatenate(cols, axis=1)

        qt_scr[2 * g] = build(jnp.zeros((n_sel, tq), BF16))

        s_c = _dot(kc_ref[0, g], qt_scr[2 * g])
        p_sum = jnp.zeros((n_cmp, tq), F32)
        ps = []
        for r in range(rep):
            sr = jnp.where(valid_c, s_c[:, r * tq:(r + 1) * tq], NEG_INF)
            m = jnp.max(sr, axis=0, keepdims=True)
            e = jnp.where(valid_c, jnp.exp2(sr - m), 0.0)
            den = jnp.sum(e, axis=0, keepdims=True)
            p = e * jnp.where(den > 0.0, 1.0 / den, 0.0)
            p_sum = p_sum + p
            ps.append(p.astype(BF16))
        o_cmp.append(_dot(vct_ref[0, g * d:(g + 1) * d, :], jnp.concatenate(ps, axis=1)))

        p_hi = p_sum.astype(BF16)
        p_lo = (p_sum - p_hi.astype(F32)).astype(BF16)
        ovt = ovt_ref[...]
        imp_t = _dot(ovt, p_hi) + _dot(ovt, p_lo)
        score = jnp.where(causal_sel, imp_t + jnp.where(forced, NSA_FORCE_BONUS, 0.0), NEG_INF)
        rank = jnp.zeros((n_sel, tq), F32)
        for mp in range(n_sel):
            row = score[mp:mp + 1, :]
            ahead = (row > score) | ((row == score) & (mp < jm))
            rank = rank + jnp.where(ahead, 1.0, 0.0)
        selected = (rank < NSA_TOPK) & causal_sel
        qt_scr[2 * g + 1] = build(jnp.where(selected, 0.0, -NSA_MASK).astype(BF16))

    def score(chains, kt, slot):
        k0 = pl.multiple_of(kt * tq, tq)
        for ci, (k_ref, g, _, qi, _) in enumerate(chains):
            slot[ci] = _dot(k_ref[0, g, pl.ds(k0, tq), :], qt_scr[qi])

    def absorb(chains, kt, add, slot):
        for ci, (_, _, v_row0, _, si) in enumerate(chains):
            s = slot[ci]
            if add is not None:
                s = s + add
            m_prev = m_scr[si]
            m_new = jnp.maximum(m_prev, jnp.max(s, axis=0, keepdims=True))
            e = jnp.exp2(s - m_new)
            alpha = jnp.exp2(m_prev - m_new)
            vals = jnp.concatenate([vt_ref[0, kt, v_row0:v_row0 + d, :], ones_rows], axis=0)
            acc_scr[si] = alpha * acc_scr[si] + _dot(vals, e.astype(BF16))
            m_scr[si] = m_new

    def sweep(chains, lo):
        n = t - lo

        def pair(j, carry):
            a = lo + 2 * j
            score(chains, a + 1, sb_scr)
            absorb(chains, a, None, sa_scr)
            score(chains, a + 2, sa_scr)
            absorb(chains, a + 1, None, sb_scr)
            return carry

        lax.fori_loop(0, n // 2, pair, 0)

        @pl.when(n % 2 == 1)
        def _():
            score(chains, t, sb_scr)
            absorb(chains, t - 1, None, sa_scr)
            absorb(chains, t, diag_add, sb_scr)

        @pl.when(n % 2 == 0)
        def _():
            absorb(chains, t, diag_add, sa_scr)

    m_scr[...] = jnp.full(m_scr.shape, NEG_INF, F32)
    acc_scr[...] = jnp.zeros(acc_scr.shape, F32)

    sel_chains = [(ks_ref, g, g * d, 2 * g + 1, g) for g in range(groups)]
    score(sel_chains, 0, sa_scr)
    sweep(sel_chains, 0)

    win_chains = [(kw_ref, g, (groups + g) * d, 2 * g, groups + g) for g in range(groups)]
    oldest = jnp.maximum(t - win_tiles, 0)
    lo = jnp.maximum(t - win_tiles + 1, 0)
    score(win_chains, oldest, sb_scr)
    score(win_chains, jnp.minimum(lo, t), sa_scr)
    absorb(win_chains, oldest, part_add, sb_scr)
    sweep(win_chains, lo)

    o_sel = [acc_scr[g, :d] / acc_scr[g, d:d + 1] for g in range(groups)]
    o_win = [acc_scr[groups + g, :d] / acc_scr[groups + g, d:d + 1] for g in range(groups)]

    for g in range(groups):
        for r in range(rep):
            h = g * rep + r
            cs = slice(r * tq, (r + 1) * tq)
            o = (gates[h:h + 1] * o_cmp[g][:, cs] + gates[N_HEADS + h:N_HEADS + h + 1] * o_sel[g][:, cs]
                 + gates[2 * N_HEADS + h:2 * N_HEADS + h + 1] * o_win[g][:, cs])
            o_ref[h * d:(h + 1) * d, :] = o.astype(o_ref.dtype)


def _nsa_mixer_ln(x, w_in, cmp_pos, cmp_w1, cmp_w2, w_out, ln_g, ln_b):
    b, s, d = x.shape
    hd = N_HEADS * HEAD_DIM
    groups = NSA_KV_GROUPS
    gd = groups * HEAD_DIM
    n = b * s
    tq = ATT_TQ
    n_sel = s // NSA_SEL_LEN
    n_cmp = s // NSA_CMP_STRIDE
    assert HEAD_DIM + ATT_BIAS_ROWS + n_sel <= ATT_QROWS and n_sel % 16 == 0 and NSA_WINDOW % tq == 0
    x2 = x.reshape(n, d)
    w = w_in.astype(BF16)
    c0 = hd + 2 * gd
    w_k2 = jnp.concatenate([w[:, c0:c0 + gd], w[:, c0 + 2 * gd:c0 + 3 * gd]], axis=1)
    w_v2 = jnp.concatenate([w[:, c0 + gd:c0 + 2 * gd], w[:, c0 + 3 * gd:c0 + 4 * gd]], axis=1)
    qt, kvc, k2, vt, glt = _proj_mixed(x2, [(w[:, :hd], "feat", BF16, HEAD_DIM ** -0.5 * LOG2E),
                                            (w[:, hd:c0], "tok", F32, 1.0),
                                            (w_k2, "tok", BF16, 1.0),
                                            (w_v2, "feat_tiles", BF16, 1.0),
                                            (_pad_cols(w[:, c0 + 4 * gd:], LANES), "feat", F32, 1.0)])

    cmp = _nsa_compress(kvc.reshape(b, s, 2 * gd), cmp_pos, cmp_w1, cmp_w2)
    tok = np.arange(s, dtype=np.float64)
    cpos = np.arange(n_cmp, dtype=np.float64) * NSA_CMP_STRIDE + (NSA_CMP_LEN - 1) / 2
    k2 = k2.reshape(b, s, 2 * gd)
    k_sel = _keys_with_tail(k2[:, :, :gd], groups, _key_tail(tok, n_sel, NSA_SEL_LEN))
    k_win = _keys_with_tail(k2[:, :, gd:], groups, _key_tail(tok))
    k_cmp = _keys_with_tail(cmp[:, :, :gd], groups, _key_tail(cpos))
    v_cmp_t = cmp[:, :, gd:].transpose(0, 2, 1)
    vt = vt.reshape(b, s // tq, 2 * gd, tq)
    ovt = _nsa_overlap_t(s)
    nt = s // tq
    o_t = pl.pallas_call(
        functools.partial(_nsa_body, tq=tq),
        grid=(b, nt),
        in_specs=[
            pl.BlockSpec((hd, tq), lambda i, j: (0, i * nt + j)),
            pl.BlockSpec((1, groups, n_cmp, ATT_QROWS), lambda i, j: (i, 0, 0, 0)),
            pl.BlockSpec((1, gd, n_cmp), lambda i, j: (i, 0, 0)),
            pl.BlockSpec((1, groups, s, ATT_QROWS), lambda i, j: (i, 0, 0, 0)),
            pl.BlockSpec((1, groups, s, ATT_QROWS), lambda i, j: (i, 0, 0, 0)),
            pl.BlockSpec((1, nt, 2 * gd, tq), lambda i, j: (i, 0, 0, 0)),
            pl.BlockSpec((LANES, tq), lambda i, j: (0, i * nt + j)),
            _full(ovt.shape),
        ],
        out_specs=pl.BlockSpec((hd, tq), lambda i, j: (0, i * nt + j)),
        out_shape=jax.ShapeDtypeStruct((hd, n), BF16),
        scratch_shapes=[pltpu.VMEM((2 * groups, ATT_QROWS, (N_HEADS // groups) * tq), BF16),
                        pltpu.VMEM((groups, tq, (N_HEADS // groups) * tq), F32),
                        pltpu.VMEM((groups, tq, (N_HEADS // groups) * tq), F32),
                        pltpu.VMEM((2 * groups, 1, (N_HEADS // groups) * tq), F32),
                        pltpu.VMEM((2 * groups, HEAD_DIM + ATT_BIAS_ROWS, (N_HEADS // groups) * tq), F32)],
        compiler_params=_cparams("parallel", "parallel"),
        name="nsa_core",
    )(qt, k_cmp, v_cmp_t, k_sel, k_win, vt, glt, jnp.asarray(ovt, BF16))
    return _oproj_ln(o_t, w_out, x2, ln_g, ln_b, feature_major=True).reshape(b, s, d)


def kernel(x, mem, ln_g, ln_b, ffn1_w_gate, ffn1_w_up, ffn1_w_down, ffn2_w_gate, ffn2_w_up, ffn2_w_down,
           xattn_w_q, xattn_w_kv, xattn_w_o, swa_w_in, swa_sinks, swa_w_out, nsa_w_in, nsa_cmp_pos,
           nsa_cmp_w1, nsa_cmp_w2, nsa_w_out, gla_w_in, gla_w_gate2, gla_b_gate, gla_norm_g, gla_w_out,
           pool_w, pool_scale):
    b, s, d = x.shape
    for i in range(DEPTH):
        kind = i % N_MIXERS
        j = i // N_MIXERS
        x = _ffn_ln(x.reshape(b * s, d), ffn1_w_gate[i], ffn1_w_up[i], ffn1_w_down[i],
                    ln_g[i, 0], ln_b[i, 0]).reshape(b, s, d)
        if kind == 0:
            x = _swa_mixer_ln(x, swa_w_in[j], swa_sinks[j], swa_w_out[j], ln_g[i, 1], ln_b[i, 1])
        elif kind == 1:
            x = _nsa_mixer_ln(x, nsa_w_in[j], nsa_cmp_pos[j], nsa_cmp_w1[j], nsa_cmp_w2[j], nsa_w_out[j],
                              ln_g[i, 1], ln_b[i, 1])
        elif kind == 2:
            x = _gla_mixer_ln(x, gla_w_in[j], gla_w_gate2[j], gla_b_gate[j], gla_norm_g[j], gla_w_out[j],
                              ln_g[i, 1], ln_b[i, 1])
        else:
            x = _pool_mixer_ln(x, pool_w[j], pool_scale[j], ln_g[i, 1], ln_b[i, 1])
        x = _xattn_ln(x, mem, xattn_w_q[i], xattn_w_kv[i], xattn_w_o[i], ln_g[i, 2], ln_b[i, 2])
        x = _ffn_ln(x.reshape(b * s, d), ffn2_w_gate[i], ffn2_w_up[i], ffn2_w_down[i],
                    ln_g[i, 3], ln_b[i, 3]).reshape(b, s, d)
    return x
```

```python
import functools

import numpy as np
import jax
import jax.numpy as jnp
from jax import lax
from jax.experimental import pallas as pl
from jax.experimental.pallas import tpu as pltpu

F32 = jnp.float32
BF16 = jnp.bfloat16

D_MODEL = 1024
DEPTH = 4
N_MIXERS = 4
HEAD_DIM = 64
N_HEADS = D_MODEL // HEAD_DIM
SWA_KV_HEADS = 4
SWA_WINDOW = 128
NSA_KV_GROUPS = 2
NSA_CMP_LEN = 32
NSA_CMP_STRIDE = 16
NSA_SEL_LEN = 64
NSA_TOPK = 8
NSA_WINDOW = 512
NSA_FORCE_BONUS = 1.0e4
GLA_HEADS = 4
GLA_DK = D_MODEL // 2
GLA_DV = D_MODEL
GLA_GATE_RANK = 16
GLA_TAU = 16.0
GLA_CHUNK = 64
POOL_WINDOWS = (2, 4, 8, 16)
POOL_GROUP = D_MODEL // 4
XATTN_HEADS = 4
DN_ALPHA = (2 * DEPTH) ** 0.25
LN_EPS = 1e-5
NEG_INF = -1e30
LOG2E = float(np.log2(np.e))

LANES = 128
V7X_VMEM_LIMIT_BYTES = 56 * 1024 * 1024

_NT = (((1,), (1,)), ((), ()))
_TN = (((0,), (0,)), ((), ()))


def _cparams(*sem, flags=None):
    return pltpu.CompilerParams(dimension_semantics=sem, vmem_limit_bytes=V7X_VMEM_LIMIT_BYTES, flags=flags)


def _dot(a, b):
    return jnp.dot(a, b, preferred_element_type=F32)


def _dot_nt(a, b):
    return lax.dot_general(a, b, _NT, preferred_element_type=F32)


def _dot_tn(a, b):
    return lax.dot_general(a, b, _TN, preferred_element_type=F32)


def _layer_norm(y, g, b):
    mu = jnp.mean(y, axis=-1, keepdims=True)
    yc = y - mu
    var = jnp.mean(yc * yc, axis=-1, keepdims=True)
    return yc * lax.rsqrt(var + LN_EPS) * g + b


def _alibi_slope(h, n):
    return float(2.0 ** (-8.0 * (h + 1) / n))


def _full(shape):
    nd = len(shape)
    return pl.BlockSpec(shape, lambda *_: (0,) * nd, pipeline_mode=pl.Buffered(1))


FFN_TM = 1024
FFN_SUB = 512
FFN_TF = 256


def _ffn_body(x_ref, wg_ref, wu_ref, wd_ref, g_ref, b_ref, o_ref, acc_ref, *, tf, sub):
    nf = wd_ref.shape[0] // tf
    for si in range(x_ref.shape[0] // sub):
        rows = slice(si * sub, (si + 1) * sub)
        x = x_ref[rows, :]
        xb = x.astype(BF16)
        for c in range(nf):
            cs = slice(c * tf, (c + 1) * tf)
            gate = _dot(xb, wg_ref[:, cs])
            up = _dot(xb, wu_ref[:, cs])
            h = (gate * jax.nn.sigmoid(gate) * up).astype(BF16)
            d = _dot(h, wd_ref[cs, :])
            if c == 0:
                acc_ref[si] = d
            else:
                acc_ref[si] += d
        y = DN_ALPHA * x + 0.5 * acc_ref[si]
        o_ref[rows, :] = _layer_norm(y, g_ref[...], b_ref[...])


def _layer_of(stacked, layer):
    _, r, c = stacked.shape
    return pl.BlockSpec((None, r, c), lambda *_: (layer, 0, 0), pipeline_mode=pl.Buffered(1))


def _ffn_ln(x2, w_gate, w_up, w_down, layer, ln_g, ln_b):
    n, d = x2.shape
    tm = min(FFN_TM, n)
    sub = min(FFN_SUB, tm)
    return pl.pallas_call(
        functools.partial(_ffn_body, tf=FFN_TF, sub=sub),
        grid=(n // tm,),
        in_specs=[
            pl.BlockSpec((tm, d), lambda i: (i, 0)),
            _layer_of(w_gate, layer),
            _layer_of(w_up, layer),
            _layer_of(w_down, layer),
            _full((1, d)),
            _full((1, d)),
        ],
        out_specs=pl.BlockSpec((tm, d), lambda i: (i, 0)),
        out_shape=jax.ShapeDtypeStruct((n, d), F32),
        scratch_shapes=[pltpu.VMEM((tm // sub, sub, d), F32)],
        compiler_params=_cparams("parallel"),
        name="ffn_ln",
    )(x2, w_gate, w_up, w_down, ln_g.reshape(1, d), ln_b.reshape(1, d))


PROJ_TM = 512


def _proj_body(x_ref, *refs, scales):
    n = len(scales)
    xb = x_ref[...].astype(BF16)
    for w_ref, o_ref, s in zip(refs[:n], refs[n:], scales):
        r = _dot(xb, w_ref[...])
        if s != 1.0:
            r = r * s
        o_ref[...] = r.astype(o_ref.dtype)


def _proj(x2, weights, dtypes, scales=None):
    n, k = x2.shape
    scales = tuple(scales) if scales is not None else (1.0,) * len(weights)
    tm = min(PROJ_TM, n)
    in_specs = [pl.BlockSpec((tm, k), lambda i: (i, 0))] + [_full(w.shape) for w in weights]
    out_specs = [pl.BlockSpec((tm, w.shape[1]), lambda i: (i, 0)) for w in weights]
    out_shape = [jax.ShapeDtypeStruct((n, w.shape[1]), dt) for w, dt in zip(weights, dtypes)]
    return pl.pallas_call(
        functools.partial(_proj_body, scales=scales),
        grid=(n // tm,),
        in_specs=in_specs,
        out_specs=out_specs,
        out_shape=out_shape,
        compiler_params=_cparams("parallel"),
        name="proj",
    )(x2, *weights)


def _pad_cols(w, m):
    return jnp.pad(w, ((0, 0), (0, m - w.shape[1])))


def _oproj_body(o_ref, w_ref, x_ref, g_ref, b_ref, out_ref, *, feature_major):
    y = _dot_tn(o_ref[...], w_ref[...]) if feature_major else _dot(o_ref[...], w_ref[...])
    out_ref[...] = _layer_norm(DN_ALPHA * x_ref[...] + y, g_ref[...], b_ref[...])


def _oproj_ln(o2, w_out, x2, ln_g, ln_b, feature_major=False):
    n, d = x2.shape
    k = w_out.shape[0]
    tm = min(PROJ_TM, n)
    return pl.pallas_call(
        functools.partial(_oproj_body, feature_major=feature_major),
        grid=(n // tm,),
        in_specs=[
            pl.BlockSpec((k, tm), lambda i: (0, i)) if feature_major else pl.BlockSpec((tm, k), lambda i: (i, 0)),
            _full((k, d)),
            pl.BlockSpec((tm, d), lambda i: (i, 0)),
            _full((1, d)),
            _full((1, d)),
        ],
        out_specs=pl.BlockSpec((tm, d), lambda i: (i, 0)),
        out_shape=jax.ShapeDtypeStruct((n, d), F32),
        compiler_params=_cparams("parallel"),
        name="oproj_ln",
    )(o2, w_out.astype(BF16), x2, ln_g.reshape(1, d), ln_b.reshape(1, d))


XATTN_TQ = 1024
XATTN_SUB = 1024


def _xattn_body(x_ref, kv_ref, wq_ref, wo_ref, g_ref, b_ref, o_ref, *, heads, sub):
    d = x_ref.shape[-1]
    dh = d // heads
    for si in range(x_ref.shape[1] // sub):
        rows = slice(si * sub, (si + 1) * sub)
        x = x_ref[0, rows, :]
        q = (_dot(x.astype(BF16), wq_ref[...]) * dh ** -0.5).astype(BF16)
        scores = [_dot_nt(q[:, h * dh:(h + 1) * dh], kv_ref[0, :, h * dh:(h + 1) * dh]) for h in range(heads)]
        outs = []
        for h in range(heads):
            vh = kv_ref[0, :, d + h * dh:d + (h + 1) * dh]
            s = scores[h]
            m = jnp.max(s, axis=-1, keepdims=True)
            e = jnp.exp(s - m)
            p = e / jnp.sum(e, axis=-1, keepdims=True)
            outs.append(_dot(p.astype(BF16), vh).astype(BF16))
        o = jnp.concatenate(outs, axis=-1)
        y = _dot(o, wo_ref[...])
        o_ref[0, rows, :] = _layer_norm(DN_ALPHA * x + y, g_ref[...], b_ref[...])


def _xattn_ln(x, mem, w_q, w_kv, w_o, layer, ln_g, ln_b):
    b, s, d = x.shape
    m = mem.shape[1]
    (kv,) = _proj(mem.reshape(b * m, d), [w_kv], [BF16])
    kv = kv.reshape(b, m, 2 * d)
    tq = min(XATTN_TQ, s)
    return pl.pallas_call(
        functools.partial(_xattn_body, heads=XATTN_HEADS, sub=min(XATTN_SUB, tq)),
        grid=(b, s // tq),
        in_specs=[
            pl.BlockSpec((1, tq, d), lambda i, j: (i, j, 0)),
            pl.BlockSpec((1, m, 2 * d), lambda i, j: (i, 0, 0)),
            _layer_of(w_q, layer),
            _layer_of(w_o, layer),
            _full((1, d)),
            _full((1, d)),
        ],
        out_specs=pl.BlockSpec((1, tq, d), lambda i, j: (i, j, 0)),
        out_shape=jax.ShapeDtypeStruct((b, s, d), F32),
        compiler_params=_cparams("parallel", "parallel"),
        name="xattn_ln",
    )(x, kv, w_q, w_o, ln_g.reshape(1, d), ln_b.reshape(1, d))


ATT_TQ = 128
ATT_QROWS = 128
ATT_BIAS_ROWS = 16


def _split3_bf16(x):
    out = []
    r = np.float32(x)
    for _ in range(3):
        p = np.float32(np.asarray(r, np.float32).astype(BF16).astype(np.float32))
        out.append(float(p))
        r = np.float32(r - p)
    return out


def _alibi_rows(h, tpos, brow):
    slope = _alibi_slope(h, N_HEADS) * LOG2E
    pieces = _split3_bf16(slope * LANES) + _split3_bf16(slope)
    v = -np.float32(slope) * tpos
    v_hi = v.astype(BF16).astype(F32)
    v_mid = (v - v_hi).astype(BF16).astype(F32)
    v_lo = v - v_hi - v_mid
    bias = jnp.zeros(brow.shape, F32)
    for k, val in enumerate(pieces + [v_hi, v_mid, v_lo]):
        bias = jnp.where(brow == k, val, bias)
    return bias.astype(BF16)


def _key_tail(pos, n_blocks=0, block_len=1):
    n = pos.shape[0]
    hi = np.floor(pos / LANES)
    lo = pos - hi * LANES
    tail = np.zeros((n, ATT_QROWS - HEAD_DIM), np.float32)
    tail[:, 0:3] = hi[:, None]
    tail[:, 3:6] = lo[:, None]
    tail[:, 6:9] = 1.0
    if n_blocks:
        blk = (pos // block_len).astype(np.int64)
        tail[np.arange(n), ATT_BIAS_ROWS + blk] = 1.0
    return tail


def _keys_with_tail(k_tok, groups, tail):
    b, rows, _ = k_tok.shape
    k = k_tok.reshape(b, rows, groups, HEAD_DIM).transpose(0, 2, 1, 3)
    t_b = jnp.broadcast_to(jnp.asarray(tail, BF16), (b, groups, rows, tail.shape[1]))
    return jnp.concatenate([k, t_b], axis=-1)


def _proj_mixed_body(x_ref, *refs, kinds, scales, tq):
    n = len(kinds)
    xb = x_ref[...].astype(BF16)
    tails = iter(refs[n:len(refs) - n])
    for w_ref, o_ref, kind, s in zip(refs[:n], refs[len(refs) - n:], kinds, scales):
        r = _dot(xb, w_ref[...]) if kind in ("tok", "keys") else _dot_nt(w_ref[...], xb)
        if s != 1.0:
            r = r * s
        r = r.astype(o_ref.dtype)
        if kind == "feat_tiles":
            for c in range(o_ref.shape[0]):
                o_ref[c] = r[:, c * tq:(c + 1) * tq]
        elif kind == "keys":
            tail = next(tails)[...]
            for g in range(o_ref.shape[1]):
                o_ref[0, g] = jnp.concatenate([r[:, g * HEAD_DIM:(g + 1) * HEAD_DIM], tail], axis=1)
        else:
            o_ref[...] = r


def _proj_mixed(x2, specs, seq, tq=ATT_TQ):
    n, k = x2.shape
    tm = min(PROJ_TM, seq)
    per_seq = seq // tm
    kinds = tuple(s[1] for s in specs)
    ws, tails, tail_specs, out_specs, out_shape = [], [], [], [], []
    for spec in specs:
        w, kind, dt = spec[:3]
        m = w.shape[1]
        if kind == "tok":
            ws.append(w)
            out_specs.append(pl.BlockSpec((tm, m), lambda i: (i, 0)))
            out_shape.append(jax.ShapeDtypeStruct((n, m), dt))
        elif kind == "feat":
            ws.append(w.T)
            out_specs.append(pl.BlockSpec((m, tm), lambda i: (0, i)))
            out_shape.append(jax.ShapeDtypeStruct((m, n), dt))
        elif kind == "feat_tiles":
            ws.append(w.T)
            out_specs.append(pl.BlockSpec((tm // tq, m, tq), lambda i: (i, 0, 0)))
            out_shape.append(jax.ShapeDtypeStruct((n // tq, m, tq), dt))
        else:
            ws.append(w)
            groups = m // HEAD_DIM
            tails.append(jnp.asarray(spec[4], dt))
            tail_specs.append(pl.BlockSpec((tm, ATT_QROWS - HEAD_DIM), lambda i: (i % per_seq, 0)))
            out_specs.append(pl.BlockSpec((1, groups, tm, ATT_QROWS), lambda i: (i // per_seq, 0, i % per_seq, 0)))
            out_shape.append(jax.ShapeDtypeStruct((n // seq, groups, seq, ATT_QROWS), dt))
    return pl.pallas_call(
        functools.partial(_proj_mixed_body, kinds=kinds, scales=tuple(s[3] for s in specs), tq=tq),
        grid=(n // tm,),
        in_specs=[pl.BlockSpec((tm, k), lambda i: (i, 0))] + [_full(w.shape) for w in ws] + tail_specs,
        out_specs=out_specs,
        out_shape=out_shape,
        compiler_params=_cparams("parallel"),
        name="proj_mixed",
    )(x2, *ws, *tails)


SWA_CHAIN_HEADS = 4
SWA_LOOKAHEAD = 2


def _swa_body(sink_ref, qt_ref, k_ref, vt_ref, o_ref, *, tq):
    d = HEAD_DIM
    groups = SWA_KV_HEADS
    rep = N_HEADS // groups
    t = pl.program_id(1)
    prev = jnp.maximum(t - 1, 0)
    tpos = (t * tq + lax.broadcasted_iota(jnp.int32, (1, tq), 1)).astype(F32)
    brow = lax.broadcasted_iota(jnp.int32, (ATT_BIAS_ROWS, tq), 0)
    ones_rows = jnp.where(brow == 0, 1.0, 0.0).astype(BF16)
    ci = lax.broadcasted_iota(jnp.int32, (tq, tq), 0)
    ai = lax.broadcasted_iota(jnp.int32, (tq, tq), 1)
    hb = SWA_CHAIN_HEADS
    tile_r = lambda m: jnp.concatenate([m] * hb, axis=1)
    diag_add = tile_r(jnp.where(ci <= ai, 0.0, NEG_INF))
    prev_add = tile_r(jnp.where((ci > ai) & (t > 0), 0.0, NEG_INF))
    pad_rows = jnp.zeros((ATT_QROWS - d - ATT_BIAS_ROWS, tq), BF16)
    k0 = pl.multiple_of(t * tq, tq)
    kp = pl.multiple_of(prev * tq, tq)

    def issue(c):
        g = (c * hb) // rep
        qt = jnp.concatenate(
            [jnp.concatenate([qt_ref[h * d:(h + 1) * d, :], _alibi_rows(h, tpos, brow), pad_rows], axis=0)
             for h in range(c * hb, (c + 1) * hb)], axis=1)
        return _dot(k_ref[0, g, pl.ds(kp, tq), :], qt), _dot(k_ref[0, g, pl.ds(k0, tq), :], qt)

    def finish(c, sc):
        g = (c * hb) // rep
        s_prev = sc[0] + prev_add
        s_diag = sc[1] + diag_add
        sink = jnp.concatenate([jnp.full((1, tq), sink_ref[h] * LOG2E, F32) for h in range(c * hb, (c + 1) * hb)], axis=1)
        m = jnp.maximum(jnp.maximum(jnp.max(s_prev, axis=0, keepdims=True),
                                    jnp.max(s_diag, axis=0, keepdims=True)), sink)
        e_prev = jnp.exp2(s_prev - m)
        e_diag = jnp.exp2(s_diag - m)
        acc = (_dot(jnp.concatenate([vt_ref[0, prev, g * d:(g + 1) * d, :], ones_rows], axis=0), e_prev.astype(BF16))
               + _dot(jnp.concatenate([vt_ref[0, t, g * d:(g + 1) * d, :], ones_rows], axis=0), e_diag.astype(BF16)))
        o = acc[:d] / (acc[d:d + 1] + jnp.exp2(sink - m))
        for r in range(hb):
            h = c * hb + r
            o_ref[h * d:(h + 1) * d, :] = o[:, r * tq:(r + 1) * tq].astype(o_ref.dtype)

    n_chains = N_HEADS // hb
    pending = [issue(c) for c in range(min(SWA_LOOKAHEAD, n_chains))]
    for c in range(n_chains):
        finish(c, pending[c])
        if c + SWA_LOOKAHEAD < n_chains:
            pending.append(issue(c + SWA_LOOKAHEAD))


def _swa_mixer_ln(x, w_in, sinks, w_out, ln_g, ln_b):
    b, s, d = x.shape
    hd = N_HEADS * HEAD_DIM
    groups = SWA_KV_HEADS
    gd = groups * HEAD_DIM
    n = b * s
    tq = ATT_TQ
    nt = s // tq
    assert SWA_WINDOW == tq
    x2 = x.reshape(n, d)
    w = w_in.astype(BF16)
    tail = _key_tail(np.arange(s, dtype=np.float64))
    qt, kk, vt = _proj_mixed(x2, [(w[:, :hd], "feat", BF16, HEAD_DIM ** -0.5 * LOG2E),
                                  (w[:, hd:hd + gd], "keys", BF16, 1.0, tail),
                                  (w[:, hd + gd:], "feat_tiles", BF16, 1.0)], seq=s)
    o_t = pl.pallas_call(
        functools.partial(_swa_body, tq=tq),
        grid=(b, nt),
        in_specs=[
            pl.BlockSpec(memory_space=pltpu.SMEM),
            pl.BlockSpec((hd, tq), lambda i, j: (0, i * nt + j)),
            pl.BlockSpec((1, groups, s, ATT_QROWS), lambda i, j: (i, 0, 0, 0)),
            pl.BlockSpec((1, nt, gd, tq), lambda i, j: (i, 0, 0, 0)),
        ],
        out_specs=pl.BlockSpec((hd, tq), lambda i, j: (0, i * nt + j)),
        out_shape=jax.ShapeDtypeStruct((hd, n), BF16),
        compiler_params=_cparams("parallel", "parallel"),
        name="swa_core",
    )(sinks.astype(F32), qt, kk, vt.reshape(b, nt, gd, tq))
    return _oproj_ln(o_t, w_out, x2, ln_g, ln_b, feature_major=True).reshape(b, s, d)


POOL_TM = 512
POOL_HALO = 16


def _pool_body(x_ref, halo_ref, w_ref, sc_ref, g_ref, b_ref, o_ref, ext_ref, *, tm):
    t = pl.program_id(1)
    x = x_ref[0]
    ext_ref[0:POOL_HALO, :] = jnp.where(t > 0, halo_ref[0], 0.0)
    ext_ref[POOL_HALO:, :] = x
    row = t * tm + lax.broadcasted_iota(jnp.int32, (tm, 1), 0)
    ys = []
    for gi, w in enumerate(POOL_WINDOWS):
        cs = slice(gi * POOL_GROUP, (gi + 1) * POOL_GROUP)
        xg = x[:, cs]
        acc = xg
        for k in range(1, w):
            acc = acc + ext_ref[POOL_HALO - k:POOL_HALO - k + tm, cs]
        cnt = jnp.minimum(row + 1, w).astype(F32)
        diff = (acc / cnt - xg).astype(BF16)
        ys.append(_dot(diff, w_ref[gi]))
    y = jnp.concatenate(ys, axis=-1) * sc_ref[...]
    o_ref[0] = _layer_norm(DN_ALPHA * x + y, g_ref[...], b_ref[...])


def _pool_mixer_ln(x, w_grp, scale, ln_g, ln_b):
    b, s, d = x.shape
    tm = min(POOL_TM, s)
    hb = tm // POOL_HALO
    ng = len(POOL_WINDOWS)
    return pl.pallas_call(
        functools.partial(_pool_body, tm=tm),
        grid=(b, s // tm),
        in_specs=[
            pl.BlockSpec((1, tm, d), lambda i, j: (i, j, 0)),
            pl.BlockSpec((1, POOL_HALO, d), lambda i, j: (i, jnp.maximum(j * hb - 1, 0), 0)),
            _full((ng, POOL_GROUP, POOL_GROUP)),
            _full((1, d)),
            _full((1, d)),
            _full((1, d)),
        ],
        out_specs=pl.BlockSpec((1, tm, d), lambda i, j: (i, j, 0)),
        out_shape=jax.ShapeDtypeStruct((b, s, d), F32),
        scratch_shapes=[pltpu.VMEM((tm + POOL_HALO, d), F32)],
        compiler_params=_cparams("parallel", "parallel"),
        name="pool_ln",
    )(x, x, w_grp.astype(BF16), scale.reshape(1, d), ln_g.reshape(1, d), ln_b.reshape(1, d))


GLA_STEP_CHUNKS = 4


def _gla_tables(c):
    levels = []
    s = c
    while s >= 1:
        levels.append(s)
        s //= 2
    rows, masks = [], []
    idx = np.arange(c)
    for s in levels:
        dq = np.zeros((c, c), np.float32)
        dk = np.zeros((c, c), np.float32)
        for i in range(c):
            blk = i // s
            if s == c or blk % 2 == 1:
                dq[i, blk * s:i + 1] = 1.0
            dk[i, i + 1:blk * s + s] = 1.0
        rows.append(dq)
        if s > 1:
            rows.append(dk)
        if s < c:
            masks.append(((idx[:, None] // (2 * s) == idx[None, :] // (2 * s))
                          & ((idx[:, None] // s) % 2 == 1) & ((idx[None, :] // s) % 2 == 0)))
    masks.append(np.eye(c, dtype=bool))
    return np.concatenate(rows, 0), np.stack(masks).astype(np.float32), len(levels)


def _gla_body(q_ref, k_ref, v_ref, glr_ref, r_ref, wg2_ref, bg_ref, ng_ref, dall_ref, mask_ref,
              o_ref, st_ref, *, nlev, nck):
    c = GLA_CHUNK
    heads = GLA_HEADS
    dk = GLA_DK // heads
    dv = GLA_DV // heads

    @pl.when(pl.program_id(1) == 0)
    def _():
        st_ref[...] = jnp.zeros_like(st_ref)

    z = _dot(glr_ref[0].astype(BF16), wg2_ref[...]) + bg_ref[...]
    log_a = (jnp.minimum(z, 0.0) - jnp.log1p(jnp.exp(-jnp.abs(z)))) * (1.0 / GLA_TAU)
    hi = log_a.astype(BF16)
    lo = (log_a - hi.astype(F32)).astype(BF16)
    parts = []
    for ck in range(nck):
        parts += [hi[ck * c:(ck + 1) * c], lo[ck * c:(ck + 1) * c]]
    e2 = _dot(dall_ref[...], jnp.concatenate(parts, axis=-1))
    q = q_ref[0]
    k = k_ref[0]
    pre = []
    for ck in range(nck):
        rows = slice(ck * c, (ck + 1) * c)
        decay = jnp.exp(e2[:, 2 * ck * GLA_DK:(2 * ck + 1) * GLA_DK]
                        + e2[:, (2 * ck + 1) * GLA_DK:(2 * ck + 2) * GLA_DK])
        for h in range(heads):
            ks = slice(h * dk, (h + 1) * dk)
            qh = q[rows, ks]
            kh = k[rows, ks]
            tbl = lambda i: decay[i * c:(i + 1) * c, ks]
            q_in = (qh * tbl(0)).astype(BF16)
            k_out = (kh * tbl(1)).astype(BF16)
            khb = kh.astype(BF16)
            att = mask_ref[nlev - 1] * _dot_nt(qh.astype(BF16), khb)
            for li in range(nlev - 1):
                ql = (qh * tbl(2 + 2 * li)).astype(BF16)
                kl = (kh * tbl(3 + 2 * li)).astype(BF16) if li < nlev - 2 else khb
                att = att + mask_ref[li] * _dot_nt(ql, kl)
            total = decay[c - 1:c, ks]
            pre.append((att.astype(BF16), q_in, k_out, total))
    states = [st_ref[h] for h in range(heads)]
    r = r_ref[0]
    gate = r * jax.nn.sigmoid(r)
    for ck in range(nck):
        rows = slice(ck * c, (ck + 1) * c)
        outs = []
        for h in range(heads):
            att, q_in, k_out, total = pre[ck * heads + h]
            vh = v_ref[0, rows, h * dv:(h + 1) * dv]
            out = _dot(att, vh) + _dot_nt(q_in, states[h].astype(BF16))
            states[h] = states[h] * total + _dot_tn(vh, k_out)
            outs.append(out * lax.rsqrt(jnp.mean(out * out, axis=-1, keepdims=True) + LN_EPS) * ng_ref[...])
        o_ref[0, rows, :] = (jnp.concatenate(outs, axis=-1) * gate[rows]).astype(o_ref.dtype)
    for h in range(heads):
        st_ref[h] = states[h]


def _gla_mixer_ln(x, w_in, w_gate2, b_gate, norm_g, w_out, ln_g, ln_b):
    b, s, d = x.shape
    c = GLA_CHUNK
    heads = GLA_HEADS
    dk = GLA_DK // heads
    dv = GLA_DV // heads
    x2 = x.reshape(b * s, d)
    w = w_in.astype(BF16)
    o0, o1, o2, o3 = GLA_DK, 2 * GLA_DK, 2 * GLA_DK + GLA_DV, 2 * GLA_DK + GLA_DV + GLA_GATE_RANK
    q, k, v, glr, r = _proj(
        x2, [w[:, :o0], w[:, o0:o1], w[:, o1:o2], _pad_cols(w[:, o2:o3], LANES), w[:, o3:]],
        [F32, F32, BF16, F32, F32], scales=(dk ** -0.5, 1.0, 1.0, 1.0, 1.0))
    wg2 = jnp.pad(w_gate2.astype(BF16), ((0, LANES - GLA_GATE_RANK), (0, 0)))
    dall, masks, nlev = _gla_tables(c)
    nck = GLA_STEP_CHUNKS
    tok = lambda width: pl.BlockSpec((1, nck * c, width), lambda i, j: (i, j, 0))
    o = pl.pallas_call(
        functools.partial(_gla_body, nlev=nlev, nck=nck),
        grid=(b, s // (nck * c)),
        in_specs=[
            tok(GLA_DK), tok(GLA_DK), tok(GLA_DV), tok(LANES), tok(GLA_DV),
            _full((LANES, GLA_DK)),
            _full((1, GLA_DK)),
            _full((1, dv)),
            _full(dall.shape),
            _full(masks.shape),
        ],
        out_specs=tok(GLA_DV),
        out_shape=jax.ShapeDtypeStruct((b, s, GLA_DV), BF16),
        scratch_shapes=[pltpu.VMEM((heads, dv, dk), F32)],
        compiler_params=_cparams("parallel", "arbitrary"),
        name="gla_core",
    )(q.reshape(b, s, -1), k.reshape(b, s, -1), v.reshape(b, s, -1), glr.reshape(b, s, -1),
      r.reshape(b, s, -1), wg2, b_gate.reshape(1, -1), norm_g.reshape(1, -1),
      jnp.asarray(dall, BF16), jnp.asarray(masks, F32))
    return _oproj_ln(o.reshape(b * s, GLA_DV), w_out, x2, ln_g, ln_b).reshape(b, s, d)


def _gelu_tanh(x):
    return 0.5 * x * (1.0 + jnp.tanh(np.sqrt(2.0 / np.pi).astype(np.float32) * (x + 0.044715 * (x * x * x))))


def _cmp_body(kc_ref, vc_ref, plo_ref, phi_ref, w1a_ref, w1b_ref, w2_ref, o_ref):
    nch, width = o_ref.shape[1], o_ref.shape[2]
    st = kc_ref.shape[1] // nch
    a = jnp.zeros((nch, width), F32)
    bm = jnp.zeros((nch, width), F32)
    for l in range(st):
        c = jnp.concatenate([kc_ref[0, pl.ds(l, nch, stride=st), :], vc_ref[0, pl.ds(l, nch, stride=st), :]], axis=1)
        ws = slice(l * width, (l + 1) * width)
        a = a + _dot((c + plo_ref[:, ws]).astype(BF16), w1a_ref[ws, :])
        bm = bm + _dot((c + phi_ref[:, ws]).astype(BF16), w1b_ref[ws, :])
    nxt = jnp.concatenate([bm[1:], jnp.zeros_like(bm[:1])], axis=0)
    row = lax.broadcasted_iota(jnp.int32, a.shape, 0)
    pre = jnp.where(row < a.shape[0] - 1, a + nxt, 0.0)
    o_ref[0] = _dot(_gelu_tanh(pre).astype(BF16), w2_ref[...]).astype(o_ref.dtype)


def _nsa_compress(kc, vc, cmp_pos, cmp_w1, cmp_w2):
    b, s, half = kc.shape
    st, g, d = NSA_CMP_STRIDE, NSA_KV_GROUPS, HEAD_DIM
    nch = s // st
    width = 2 * g * d
    eye = jnp.eye(2 * g, dtype=F32)
    sel = jnp.repeat(jnp.eye(2, dtype=F32), g, axis=1)

    def expand(w1_half):
        wj = jnp.einsum('jlde,jc->lcde', w1_half, sel)
        return jnp.einsum('lcde,cf->lcdfe', wj, eye).reshape(st * width, width).astype(BF16)

    w1a = expand(cmp_w1[:, :st])
    w1b = expand(cmp_w1[:, st:])
    w2 = jnp.einsum('cde,cf->cdfe', jnp.einsum('jde,jc->cde', cmp_w2, sel), eye).reshape(width, width).astype(BF16)

    def pos_row(p_half):
        return jnp.einsum('jld,jc->lcd', p_half, sel).reshape(1, st * width)

    return pl.pallas_call(
        _cmp_body,
        grid=(b,),
        in_specs=[
            pl.BlockSpec((1, s, half), lambda i: (i, 0, 0)),
            pl.BlockSpec((1, s, half), lambda i: (i, 0, 0)),
            _full((1, st * width)),
            _full((1, st * width)),
            _full((st * width, width)),
            _full((st * width, width)),
            _full((width, width)),
        ],
        out_specs=pl.BlockSpec((1, nch, width), lambda i: (i, 0, 0)),
        out_shape=jax.ShapeDtypeStruct((b, nch, width), BF16),
        compiler_params=_cparams("parallel"),
        name="nsa_compress",
    )(kc, vc, pos_row(cmp_pos[:, :st]), pos_row(cmp_pos[:, st:]), w1a, w1b, w2)


def _nsa_overlap_t(s):
    n_cmp_pad = s // NSA_CMP_STRIDE
    n_sel = s // NSA_SEL_LEN
    blk = np.arange(n_cmp_pad) * NSA_CMP_STRIDE
    sel_start = np.arange(n_sel) * NSA_SEL_LEN
    ov = (blk[None, :] < sel_start[:, None] + NSA_SEL_LEN) & (blk[None, :] + NSA_CMP_LEN > sel_start[:, None])
    return ov.astype(np.float32)


NSA_MASK = 1.0e30


def _nsa_body(qt_ref, kc_ref, vct_ref, ks_ref, kw_ref, vt_ref, glt_ref, ovt_ref, o_ref, qt_scr, sa_scr, sb_scr,
              m_scr, acc_scr, *, tq):
    d = HEAD_DIM
    groups = NSA_KV_GROUPS
    rep = N_HEADS // groups
    n_cmp = kc_ref.shape[2]
    n_sel = ovt_ref.shape[0]
    t = pl.program_id(1)
    q0 = t * tq
    win_tiles = NSA_WINDOW // tq
    gates = jax.nn.sigmoid(glt_ref[...])

    tpos = (q0 + lax.broadcasted_iota(jnp.int32, (1, tq), 1)).astype(F32)
    brow = lax.broadcasted_iota(jnp.int32, (ATT_BIAS_ROWS, tq), 0)
    ones_rows = jnp.where(brow == 0, 1.0, 0.0).astype(BF16)
    ci = lax.broadcasted_iota(jnp.int32, (tq, tq), 0)
    ai = lax.broadcasted_iota(jnp.int32, (tq, tq), 1)
    tile8 = lambda m: jnp.concatenate([m] * rep, axis=1)
    diag_add = tile8(jnp.where(ci <= ai, 0.0, NEG_INF))
    part_add = tile8(jnp.where((ci > ai) & (t >= win_tiles), 0.0, NEG_INF))
    ncm = lax.broadcasted_iota(jnp.int32, (n_cmp, tq), 0)
    tcm = q0 + lax.broadcasted_iota(jnp.int32, (n_cmp, tq), 1)
    valid_c = ncm * NSA_CMP_STRIDE + (NSA_CMP_LEN - 1) <= tcm
    jm = lax.broadcasted_iota(jnp.int32, (n_sel, tq), 0)
    cur = (q0 + lax.broadcasted_iota(jnp.int32, (n_sel, tq), 1)) // NSA_SEL_LEN
    causal_sel = jm <= cur
    forced = (jm == 0) | (jm == cur) | (jm == cur - 1)
    pad_rows = jnp.zeros((ATT_QROWS - d - ATT_BIAS_ROWS - n_sel, tq), BF16)

    o_cmp = []
    for g in range(groups):
        heads = [(qt_ref[h * d:(h + 1) * d, :], _alibi_rows(h, tpos, brow)) for h in range(g * rep, (g + 1) * rep)]

        def build(mask_rows):
            cols = [jnp.concatenate([qh, bias, mask_rows, pad_rows], axis=0) for qh, bias in heads]
            return jnp.concatenate(cols, axis=1)

        qt_scr[2 * g] = build(jnp.zeros((n_sel, tq), BF16))

        s_c = _dot(kc_ref[0, g], qt_scr[2 * g])
        p_sum = jnp.zeros((n_cmp, tq), F32)
        ps = []
        for r in range(rep):
            sr = jnp.where(valid_c, s_c[:, r * tq:(r + 1) * tq], NEG_INF)
            m = jnp.max(sr, axis=0, keepdims=True)
            e = jnp.where(valid_c, jnp.exp2(sr - m), 0.0)
            den = jnp.sum(e, axis=0, keepdims=True)
            p = e * jnp.where(den > 0.0, 1.0 / den, 0.0)
            p_sum = p_sum + p
            ps.append(p.astype(BF16))
        o_cmp.append(_dot(vct_ref[0, g * d:(g + 1) * d, :], jnp.concatenate(ps, axis=1)))

        p_hi = p_sum.astype(BF16)
        p_lo = (p_sum - p_hi.astype(F32)).astype(BF16)
        ovt = ovt_ref[...]
        imp_t = _dot(ovt, p_hi) + _dot(ovt, p_lo)
        score = jnp.where(causal_sel, imp_t + jnp.where(forced, NSA_FORCE_BONUS, 0.0), NEG_INF)
        rank = jnp.zeros((n_sel, tq), F32)
        for mp in range(n_sel):
            row = score[mp:mp + 1, :]
            ahead = (row > score) | ((row == score) & (mp < jm))
            rank = rank + jnp.where(ahead, 1.0, 0.0)
        selected = (rank < NSA_TOPK) & causal_sel
        qt_scr[2 * g + 1] = build(jnp.where(selected, 0.0, -NSA_MASK).astype(BF16))

    def score(chains, kt, slot):
        k0 = pl.multiple_of(kt * tq, tq)
        for ci, (k_ref, g, _, qi, _) in enumerate(chains):
            slot[ci] = _dot(k_ref[0, g, pl.ds(k0, tq), :], qt_scr[qi])

    def absorb(chains, kt, add, slot):
        for ci, (_, _, v_row0, _, si) in enumerate(chains):
            s = slot[ci]
            if add is not None:
                s = s + add
            m_prev = m_scr[si]
            m_new = jnp.maximum(m_prev, jnp.max(s, axis=0, keepdims=True))
            e = jnp.exp2(s - m_new)
            alpha = jnp.exp2(m_prev - m_new)
            vals = jnp.concatenate([vt_ref[0, kt, v_row0:v_row0 + d, :], ones_rows], axis=0)
            acc_scr[si] = alpha * acc_scr[si] + _dot(vals, e.astype(BF16))
            m_scr[si] = m_new

    def sweep(chains, lo):
        n = t - lo

        def pair(j, carry):
            a = lo + 2 * j
            score(chains, a + 1, sb_scr)
            absorb(chains, a, None, sa_scr)
            score(chains, a + 2, sa_scr)
            absorb(chains, a + 1, None, sb_scr)
            return carry

        lax.fori_loop(0, n // 2, pair, 0)

        @pl.when(n % 2 == 1)
        def _():
            score(chains, t, sb_scr)
            absorb(chains, t - 1, None, sa_scr)
            absorb(chains, t, diag_add, sb_scr)

        @pl.when(n % 2 == 0)
        def _():
            absorb(chains, t, diag_add, sa_scr)

    m_scr[...] = jnp.full(m_scr.shape, NEG_INF, F32)
    acc_scr[...] = jnp.zeros(acc_scr.shape, F32)

    sel_chains = [(ks_ref, g, g * d, 2 * g + 1, g) for g in range(groups)]
    score(sel_chains, 0, sa_scr)
    sweep(sel_chains, 0)

    win_chains = [(kw_ref, g, (groups + g) * d, 2 * g, groups + g) for g in range(groups)]
    oldest = jnp.maximum(t - win_tiles, 0)
    lo = jnp.maximum(t - win_tiles + 1, 0)
    score(win_chains, oldest, sb_scr)
    score(win_chains, jnp.minimum(lo, t), sa_scr)
    absorb(win_chains, oldest, part_add, sb_scr)
    sweep(win_chains, lo)

    o_sel = [acc_scr[g, :d] / acc_scr[g, d:d + 1] for g in range(groups)]
    o_win = [acc_scr[groups + g, :d] / acc_scr[groups + g, d:d + 1] for g in range(groups)]

    for g in range(groups):
        for r in range(rep):
            h = g * rep + r
            cs = slice(r * tq, (r + 1) * tq)
            o = (gates[h:h + 1] * o_cmp[g][:, cs] + gates[N_HEADS + h:N_HEADS + h + 1] * o_sel[g][:, cs]
                 + gates[2 * N_HEADS + h:2 * N_HEADS + h + 1] * o_win[g][:, cs])
            o_ref[h * d:(h + 1) * d, :] = o.astype(o_ref.dtype)


def _nsa_mixer_ln(x, w_in, cmp_pos, cmp_w1, cmp_w2, w_out, ln_g, ln_b):
    b, s, d = x.shape
    hd = N_HEADS * HEAD_DIM
    groups = NSA_KV_GROUPS
    gd = groups * HEAD_DIM
    n = b * s
    tq = ATT_TQ
    n_sel = s // NSA_SEL_LEN
    n_cmp = s // NSA_CMP_STRIDE
    assert HEAD_DIM + ATT_BIAS_ROWS + n_sel <= ATT_QROWS and n_sel % 16 == 0 and NSA_WINDOW % tq == 0
    x2 = x.reshape(n, d)
    w = w_in.astype(BF16)
    c0 = hd + 2 * gd
    w_v2 = jnp.concatenate([w[:, c0 + gd:c0 + 2 * gd], w[:, c0 + 3 * gd:c0 + 4 * gd]], axis=1)
    tok = np.arange(s, dtype=np.float64)
    qt, kc, vc, k_sel, k_win, vt, glt = _proj_mixed(
        x2, [(w[:, :hd], "feat", BF16, HEAD_DIM ** -0.5 * LOG2E),
             (w[:, hd:hd + gd], "tok", F32, 1.0),
             (w[:, hd + gd:c0], "tok", F32, 1.0),
             (w[:, c0:c0 + gd], "keys", BF16, 1.0, _key_tail(tok, n_sel, NSA_SEL_LEN)),
             (w[:, c0 + 2 * gd:c0 + 3 * gd], "keys", BF16, 1.0, _key_tail(tok)),
             (w_v2, "feat_tiles", BF16, 1.0),
             (_pad_cols(w[:, c0 + 4 * gd:], LANES), "feat", F32, 1.0)], seq=s)

    cmp = _nsa_compress(kc.reshape(b, s, gd), vc.reshape(b, s, gd), cmp_pos, cmp_w1, cmp_w2)
    cpos = np.arange(n_cmp, dtype=np.float64) * NSA_CMP_STRIDE + (NSA_CMP_LEN - 1) / 2
    k_cmp = _keys_with_tail(cmp[:, :, :gd], groups, _key_tail(cpos))
    v_cmp_t = cmp[:, :, gd:].transpose(0, 2, 1)
    vt = vt.reshape(b, s // tq, 2 * gd, tq)
    ovt = _nsa_overlap_t(s)
    nt = s // tq
    o_t = pl.pallas_call(
        functools.partial(_nsa_body, tq=tq),
        grid=(b, nt),
        in_specs=[
            pl.BlockSpec((hd, tq), lambda i, j: (0, i * nt + j)),
            pl.BlockSpec((1, groups, n_cmp, ATT_QROWS), lambda i, j: (i, 0, 0, 0)),
            pl.BlockSpec((1, gd, n_cmp), lambda i, j: (i, 0, 0)),
            pl.BlockSpec((1, groups, s, ATT_QROWS), lambda i, j: (i, 0, 0, 0)),
            pl.BlockSpec((1, groups, s, ATT_QROWS), lambda i, j: (i, 0, 0, 0)),
            pl.BlockSpec((1, nt, 2 * gd, tq), lambda i, j: (i, 0, 0, 0)),
            pl.BlockSpec((LANES, tq), lambda i, j: (0, i * nt + j)),
            _full(ovt.shape),
        ],
        out_specs=pl.BlockSpec((hd, tq), lambda i, j: (0, i * nt + j)),
        out_shape=jax.ShapeDtypeStruct((hd, n), BF16),
        scratch_shapes=[pltpu.VMEM((2 * groups, ATT_QROWS, (N_HEADS // groups) * tq), BF16),
                        pltpu.VMEM((groups, tq, (N_HEADS // groups) * tq), F32),
                        pltpu.VMEM((groups, tq, (N_HEADS // groups) * tq), F32),
                        pltpu.VMEM((2 * groups, 1, (N_HEADS // groups) * tq), F32),
                        pltpu.VMEM((2 * groups, HEAD_DIM + ATT_BIAS_ROWS, (N_HEADS // groups) * tq), F32)],
        compiler_params=_cparams("parallel", "parallel"),
        name="nsa_core",
    )(qt, k_cmp, v_cmp_t, k_sel, k_win, vt, glt, jnp.asarray(ovt, BF16))
    return _oproj_ln(o_t, w_out, x2, ln_g, ln_b, feature_major=True).reshape(b, s, d)


def kernel(x, mem, ln_g, ln_b, ffn1_w_gate, ffn1_w_up, ffn1_w_down, ffn2_w_gate, ffn2_w_up, ffn2_w_down,
           xattn_w_q, xattn_w_kv, xattn_w_o, swa_w_in, swa_sinks, swa_w_out, nsa_w_in, nsa_cmp_pos,
           nsa_cmp_w1, nsa_cmp_w2, nsa_w_out, gla_w_in, gla_w_gate2, gla_b_gate, gla_norm_g, gla_w_out,
           pool_w, pool_scale):
    b, s, d = x.shape
    ffn1 = [w.astype(BF16) for w in (ffn1_w_gate, ffn1_w_up, ffn1_w_down)]
    ffn2 = [w.astype(BF16) for w in (ffn2_w_gate, ffn2_w_up, ffn2_w_down)]
    xq, xkv, xo = (w.astype(BF16) for w in (xattn_w_q, xattn_w_kv, xattn_w_o))
    for i in range(DEPTH):
        kind = i % N_MIXERS
        j = i // N_MIXERS
        x = _ffn_ln(x.reshape(b * s, d), *ffn1, i, ln_g[i, 0], ln_b[i, 0]).reshape(b, s, d)
        if kind == 0:
            x = _swa_mixer_ln(x, swa_w_in[j], swa_sinks[j], swa_w_out[j], ln_g[i, 1], ln_b[i, 1])
        elif kind == 1:
            x = _nsa_mixer_ln(x, nsa_w_in[j], nsa_cmp_pos[j], nsa_cmp_w1[j], nsa_cmp_w2[j], nsa_w_out[j],
                              ln_g[i, 1], ln_b[i, 1])
        elif kind == 2:
            x = _gla_mixer_ln(x, gla_w_in[j], gla_w_gate2[j], gla_b_gate[j], gla_norm_g[j], gla_w_out[j],
                              ln_g[i, 1], ln_b[i, 1])
        else:
            x = _pool_mixer_ln(x, pool_w[j], pool_scale[j], ln_g[i, 1], ln_b[i, 1])
        x = _xattn_ln(x, mem, xq, xkv[i], xo, i, ln_g[i, 2], ln_b[i, 2])
        x = _ffn_ln(x.reshape(b * s, d), *ffn2, i, ln_g[i, 3], ln_b[i, 3]).reshape(b, s, d)
    return x
```

```python
import functools

import numpy as np
import jax
import jax.numpy as jnp
from jax import lax
from jax.experimental import pallas as pl
from jax.experimental.pallas import tpu as pltpu

F32 = jnp.float32
BF16 = jnp.bfloat16

D_MODEL = 1024
DEPTH = 4
N_MIXERS = 4
HEAD_DIM = 64
N_HEADS = D_MODEL // HEAD_DIM
SWA_KV_HEADS = 4
SWA_WINDOW = 128
NSA_KV_GROUPS = 2
NSA_CMP_LEN = 32
NSA_CMP_STRIDE = 16
NSA_SEL_LEN = 64
NSA_TOPK = 8
NSA_WINDOW = 512
NSA_FORCE_BONUS = 1.0e4
GLA_HEADS = 4
GLA_DK = D_MODEL // 2
GLA_DV = D_MODEL
GLA_GATE_RANK = 16
GLA_TAU = 16.0
GLA_CHUNK = 64
POOL_WINDOWS = (2, 4, 8, 16)
POOL_GROUP = D_MODEL // 4
XATTN_HEADS = 4
DN_ALPHA = (2 * DEPTH) ** 0.25
LN_EPS = 1e-5
NEG_INF = -1e30
LOG2E = float(np.log2(np.e))

LANES = 128
V7X_VMEM_LIMIT_BYTES = 56 * 1024 * 1024

_NT = (((1,), (1,)), ((), ()))
_TN = (((0,), (0,)), ((), ()))


def _cparams(*sem, flags=None):
    return pltpu.CompilerParams(dimension_semantics=sem, vmem_limit_bytes=V7X_VMEM_LIMIT_BYTES, flags=flags)


def _dot(a, b):
    return jnp.dot(a, b, preferred_element_type=F32)


def _dot_nt(a, b):
    return lax.dot_general(a, b, _NT, preferred_element_type=F32)


def _dot_tn(a, b):
    return lax.dot_general(a, b, _TN, preferred_element_type=F32)


def _layer_norm(y, g, b):
    mu = jnp.mean(y, axis=-1, keepdims=True)
    yc = y - mu
    var = jnp.mean(yc * yc, axis=-1, keepdims=True)
    return yc * lax.rsqrt(var + LN_EPS) * g + b


def _alibi_slope(h, n):
    return float(2.0 ** (-8.0 * (h + 1) / n))


def _full(shape):
    nd = len(shape)
    return pl.BlockSpec(shape, lambda *_: (0,) * nd, pipeline_mode=pl.Buffered(1))


FFN_TM = 1024
FFN_SUB = 512
FFN_TF = 256


def _ffn_body(x_ref, wg_ref, wu_ref, wd_ref, g_ref, b_ref, o_ref, acc_ref, *, tf, sub):
    nf = wd_ref.shape[0] // tf
    for si in range(x_ref.shape[0] // sub):
        rows = slice(si * sub, (si + 1) * sub)
        x = x_ref[rows, :]
        xb = x.astype(BF16)
        for c in range(nf):
            cs = slice(c * tf, (c + 1) * tf)
            gate = _dot(xb, wg_ref[:, cs])
            up = _dot(xb, wu_ref[:, cs])
            h = (gate * jax.nn.sigmoid(gate) * up).astype(BF16)
            d = _dot(h, wd_ref[cs, :])
            if c == 0:
                acc_ref[si] = d
            else:
                acc_ref[si] += d
        y = DN_ALPHA * x + 0.5 * acc_ref[si]
        o_ref[rows, :] = _layer_norm(y, g_ref[...], b_ref[...])


def _layer_of(stacked, layer):
    _, r, c = stacked.shape
    return pl.BlockSpec((None, r, c), lambda *_: (layer, 0, 0), pipeline_mode=pl.Buffered(1))


def _ffn_ln(x2, w_gate, w_up, w_down, layer, ln_g, ln_b):
    n, d = x2.shape
    tm = min(FFN_TM, n)
    sub = min(FFN_SUB, tm)
    return pl.pallas_call(
        functools.partial(_ffn_body, tf=FFN_TF, sub=sub),
        grid=(n // tm,),
        in_specs=[
            pl.BlockSpec((tm, d), lambda i: (i, 0)),
            _layer_of(w_gate, layer),
            _layer_of(w_up, layer),
            _layer_of(w_down, layer),
            _full((1, d)),
            _full((1, d)),
        ],
        out_specs=pl.BlockSpec((tm, d), lambda i: (i, 0)),
        out_shape=jax.ShapeDtypeStruct((n, d), F32),
        scratch_shapes=[pltpu.VMEM((tm // sub, sub, d), F32)],
        compiler_params=_cparams("parallel"),
        name="ffn_ln",
    )(x2, w_gate, w_up, w_down, ln_g.reshape(1, d), ln_b.reshape(1, d))


PROJ_TM = 512


def _proj_body(x_ref, *refs, scales):
    n = len(scales)
    xb = x_ref[...].astype(BF16)
    for w_ref, o_ref, s in zip(refs[:n], refs[n:], scales):
        r = _dot(xb, w_ref[...])
        if s != 1.0:
            r = r * s
        o_ref[...] = r.astype(o_ref.dtype)


def _proj(x2, weights, dtypes, scales=None):
    n, k = x2.shape
    scales = tuple(scales) if scales is not None else (1.0,) * len(weights)
    tm = min(PROJ_TM, n)
    in_specs = [pl.BlockSpec((tm, k), lambda i: (i, 0))] + [_full(w.shape) for w in weights]
    out_specs = [pl.BlockSpec((tm, w.shape[1]), lambda i: (i, 0)) for w in weights]
    out_shape = [jax.ShapeDtypeStruct((n, w.shape[1]), dt) for w, dt in zip(weights, dtypes)]
    return pl.pallas_call(
        functools.partial(_proj_body, scales=scales),
        grid=(n // tm,),
        in_specs=in_specs,
        out_specs=out_specs,
        out_shape=out_shape,
        compiler_params=_cparams("parallel"),
        name="proj",
    )(x2, *weights)


def _pad_cols(w, m):
    return jnp.pad(w, ((0, 0), (0, m - w.shape[1])))


def _oproj_body(o_ref, w_ref, x_ref, g_ref, b_ref, out_ref, *, feature_major):
    y = _dot_tn(o_ref[...], w_ref[...]) if feature_major else _dot(o_ref[...], w_ref[...])
    out_ref[...] = _layer_norm(DN_ALPHA * x_ref[...] + y, g_ref[...], b_ref[...])


def _oproj_ln(o2, w_out, x2, ln_g, ln_b, feature_major=False):
    n, d = x2.shape
    k = w_out.shape[0]
    tm = min(PROJ_TM, n)
    return pl.pallas_call(
        functools.partial(_oproj_body, feature_major=feature_major),
        grid=(n // tm,),
        in_specs=[
            pl.BlockSpec((k, tm), lambda i: (0, i)) if feature_major else pl.BlockSpec((tm, k), lambda i: (i, 0)),
            _full((k, d)),
            pl.BlockSpec((tm, d), lambda i: (i, 0)),
            _full((1, d)),
            _full((1, d)),
        ],
        out_specs=pl.BlockSpec((tm, d), lambda i: (i, 0)),
        out_shape=jax.ShapeDtypeStruct((n, d), F32),
        compiler_params=_cparams("parallel"),
        name="oproj_ln",
    )(o2, w_out.astype(BF16), x2, ln_g.reshape(1, d), ln_b.reshape(1, d))


XATTN_TQ = 1024
XATTN_SUB = 1024


def _xattn_body(x_ref, kv_ref, wq_ref, wo_ref, g_ref, b_ref, o_ref, *, heads, sub):
    d = x_ref.shape[-1]
    dh = d // heads
    for si in range(x_ref.shape[1] // sub):
        rows = slice(si * sub, (si + 1) * sub)
        x = x_ref[0, rows, :]
        q = (_dot(x.astype(BF16), wq_ref[...]) * dh ** -0.5).astype(BF16)
        scores = [_dot_nt(q[:, h * dh:(h + 1) * dh], kv_ref[0, :, h * dh:(h + 1) * dh]) for h in range(heads)]
        outs = []
        for h in range(heads):
            vh = kv_ref[0, :, d + h * dh:d + (h + 1) * dh]
            s = scores[h]
            m = jnp.max(s, axis=-1, keepdims=True)
            e = jnp.exp(s - m)
            p = e / jnp.sum(e, axis=-1, keepdims=True)
            outs.append(_dot(p.astype(BF16), vh).astype(BF16))
        o = jnp.concatenate(outs, axis=-1)
        y = _dot(o, wo_ref[...])
        o_ref[0, rows, :] = _layer_norm(DN_ALPHA * x + y, g_ref[...], b_ref[...])


def _xattn_ln(x, mem, w_q, w_kv, w_o, layer, ln_g, ln_b):
    b, s, d = x.shape
    m = mem.shape[1]
    (kv,) = _proj(mem.reshape(b * m, d), [w_kv], [BF16])
    kv = kv.reshape(b, m, 2 * d)
    tq = min(XATTN_TQ, s)
    return pl.pallas_call(
        functools.partial(_xattn_body, heads=XATTN_HEADS, sub=min(XATTN_SUB, tq)),
        grid=(b, s // tq),
        in_specs=[
            pl.BlockSpec((1, tq, d), lambda i, j: (i, j, 0)),
            pl.BlockSpec((1, m, 2 * d), lambda i, j: (i, 0, 0)),
            _layer_of(w_q, layer),
            _layer_of(w_o, layer),
            _full((1, d)),
            _full((1, d)),
        ],
        out_specs=pl.BlockSpec((1, tq, d), lambda i, j: (i, j, 0)),
        out_shape=jax.ShapeDtypeStruct((b, s, d), F32),
        compiler_params=_cparams("parallel", "parallel"),
        name="xattn_ln",
    )(x, kv, w_q, w_o, ln_g.reshape(1, d), ln_b.reshape(1, d))


ATT_TQ = 128
ATT_QROWS = 128
ATT_BIAS_ROWS = 16


def _split3_bf16(x):
    out = []
    r = np.float32(x)
    for _ in range(3):
        p = np.float32(np.asarray(r, np.float32).astype(BF16).astype(np.float32))
        out.append(float(p))
        r = np.float32(r - p)
    return out


def _alibi_rows(h, tpos, brow):
    slope = _alibi_slope(h, N_HEADS) * LOG2E
    pieces = _split3_bf16(slope * LANES) + _split3_bf16(slope)
    v = -np.float32(slope) * tpos
    v_hi = v.astype(BF16).astype(F32)
    v_mid = (v - v_hi).astype(BF16).astype(F32)
    v_lo = v - v_hi - v_mid
    bias = jnp.zeros(brow.shape, F32)
    for k, val in enumerate(pieces + [v_hi, v_mid, v_lo]):
        bias = jnp.where(brow == k, val, bias)
    return bias.astype(BF16)


def _key_tail(pos, n_blocks=0, block_len=1):
    n = pos.shape[0]
    hi = np.floor(pos / LANES)
    lo = pos - hi * LANES
    tail = np.zeros((n, ATT_QROWS - HEAD_DIM), np.float32)
    tail[:, 0:3] = hi[:, None]
    tail[:, 3:6] = lo[:, None]
    tail[:, 6:9] = 1.0
    if n_blocks:
        blk = (pos // block_len).astype(np.int64)
        tail[np.arange(n), ATT_BIAS_ROWS + blk] = 1.0
    return tail


def _keys_with_tail(k_tok, groups, tail):
    b, rows, _ = k_tok.shape
    k = k_tok.reshape(b, rows, groups, HEAD_DIM).transpose(0, 2, 1, 3)
    t_b = jnp.broadcast_to(jnp.asarray(tail, BF16), (b, groups, rows, tail.shape[1]))
    return jnp.concatenate([k, t_b], axis=-1)


def _proj_mixed_body(x_ref, *refs, kinds, scales, tq):
    n = len(kinds)
    xb = x_ref[...].astype(BF16)
    tails = iter(refs[n:len(refs) - n])
    for w_ref, o_ref, kind, s in zip(refs[:n], refs[len(refs) - n:], kinds, scales):
        r = _dot(xb, w_ref[...]) if kind in ("tok", "keys") else _dot_nt(w_ref[...], xb)
        if s != 1.0:
            r = r * s
        r = r.astype(o_ref.dtype)
        if kind == "feat_tiles":
            for c in range(o_ref.shape[0]):
                o_ref[c] = r[:, c * tq:(c + 1) * tq]
        elif kind == "keys":
            tail = next(tails)[...]
            for g in range(o_ref.shape[1]):
                o_ref[0, g] = jnp.concatenate([r[:, g * HEAD_DIM:(g + 1) * HEAD_DIM], tail], axis=1)
        else:
            o_ref[...] = r


def _proj_mixed(x2, specs, seq, tq=ATT_TQ):
    n, k = x2.shape
    tm = min(PROJ_TM, seq)
    per_seq = seq // tm
    kinds = tuple(s[1] for s in specs)
    ws, tails, tail_specs, out_specs, out_shape = [], [], [], [], []
    for spec in specs:
        w, kind, dt = spec[:3]
        m = w.shape[1]
        if kind == "tok":
            ws.append(w)
            out_specs.append(pl.BlockSpec((tm, m), lambda i: (i, 0)))
            out_shape.append(jax.ShapeDtypeStruct((n, m), dt))
        elif kind == "feat":
            ws.append(w.T)
            out_specs.append(pl.BlockSpec((m, tm), lambda i: (0, i)))
            out_shape.append(jax.ShapeDtypeStruct((m, n), dt))
        elif kind == "feat_tiles":
            ws.append(w.T)
            out_specs.append(pl.BlockSpec((tm // tq, m, tq), lambda i: (i, 0, 0)))
            out_shape.append(jax.ShapeDtypeStruct((n // tq, m, tq), dt))
        else:
            ws.append(w)
            groups = m // HEAD_DIM
            tails.append(jnp.asarray(spec[4], dt))
            tail_specs.append(pl.BlockSpec((tm, ATT_QROWS - HEAD_DIM), lambda i: (i % per_seq, 0)))
            out_specs.append(pl.BlockSpec((1, groups, tm, ATT_QROWS), lambda i: (i // per_seq, 0, i % per_seq, 0)))
            out_shape.append(jax.ShapeDtypeStruct((n // seq, groups, seq, ATT_QROWS), dt))
    return pl.pallas_call(
        functools.partial(_proj_mixed_body, kinds=kinds, scales=tuple(s[3] for s in specs), tq=tq),
        grid=(n // tm,),
        in_specs=[pl.BlockSpec((tm, k), lambda i: (i, 0))] + [_full(w.shape) for w in ws] + tail_specs,
        out_specs=out_specs,
        out_shape=out_shape,
        compiler_params=_cparams("parallel"),
        name="proj_mixed",
    )(x2, *ws, *tails)


SWA_CHAIN_HEADS = 4
SWA_LOOKAHEAD = 2


def _swa_body(sink_ref, qt_ref, k_ref, vt_ref, o_ref, *, tq):
    d = HEAD_DIM
    groups = SWA_KV_HEADS
    rep = N_HEADS // groups
    t = pl.program_id(1)
    prev = jnp.maximum(t - 1, 0)
    tpos = (t * tq + lax.broadcasted_iota(jnp.int32, (1, tq), 1)).astype(F32)
    brow = lax.broadcasted_iota(jnp.int32, (ATT_BIAS_ROWS, tq), 0)
    ones_rows = jnp.where(brow == 0, 1.0, 0.0).astype(BF16)
    ci = lax.broadcasted_iota(jnp.int32, (tq, tq), 0)
    ai = lax.broadcasted_iota(jnp.int32, (tq, tq), 1)
    hb = SWA_CHAIN_HEADS
    tile_r = lambda m: jnp.concatenate([m] * hb, axis=1)
    diag_add = tile_r(jnp.where(ci <= ai, 0.0, NEG_INF))
    prev_add = tile_r(jnp.where((ci > ai) & (t > 0), 0.0, NEG_INF))
    pad_rows = jnp.zeros((ATT_QROWS - d - ATT_BIAS_ROWS, tq), BF16)
    k0 = pl.multiple_of(t * tq, tq)
    kp = pl.multiple_of(prev * tq, tq)

    def issue(c):
        g = (c * hb) // rep
        qt = jnp.concatenate(
            [jnp.concatenate([qt_ref[h * d:(h + 1) * d, :], _alibi_rows(h, tpos, brow), pad_rows], axis=0)
             for h in range(c * hb, (c + 1) * hb)], axis=1)
        return _dot(k_ref[0, g, pl.ds(kp, tq), :], qt), _dot(k_ref[0, g, pl.ds(k0, tq), :], qt)

    def finish(c, sc):
        g = (c * hb) // rep
        s_prev = sc[0] + prev_add
        s_diag = sc[1] + diag_add
        sink = jnp.concatenate([jnp.full((1, tq), sink_ref[h] * LOG2E, F32) for h in range(c * hb, (c + 1) * hb)], axis=1)
        m = jnp.maximum(jnp.maximum(jnp.max(s_prev, axis=0, keepdims=True),
                                    jnp.max(s_diag, axis=0, keepdims=True)), sink)
        e_prev = jnp.exp2(s_prev - m)
        e_diag = jnp.exp2(s_diag - m)
        acc = (_dot(jnp.concatenate([vt_ref[0, prev, g * d:(g + 1) * d, :], ones_rows], axis=0), e_prev.astype(BF16))
               + _dot(jnp.concatenate([vt_ref[0, t, g * d:(g + 1) * d, :], ones_rows], axis=0), e_diag.astype(BF16)))
        o = acc[:d] / (acc[d:d + 1] + jnp.exp2(sink - m))
        for r in range(hb):
            h = c * hb + r
            o_ref[h * d:(h + 1) * d, :] = o[:, r * tq:(r + 1) * tq].astype(o_ref.dtype)

    n_chains = N_HEADS // hb
    pending = [issue(c) for c in range(min(SWA_LOOKAHEAD, n_chains))]
    for c in range(n_chains):
        finish(c, pending[c])
        if c + SWA_LOOKAHEAD < n_chains:
            pending.append(issue(c + SWA_LOOKAHEAD))


def _swa_mixer_ln(x, w_in, sinks, w_out, ln_g, ln_b):
    b, s, d = x.shape
    hd = N_HEADS * HEAD_DIM
    groups = SWA_KV_HEADS
    gd = groups * HEAD_DIM
    n = b * s
    tq = ATT_TQ
    nt = s // tq
    assert SWA_WINDOW == tq
    x2 = x.reshape(n, d)
    w = w_in.astype(BF16)
    tail = _key_tail(np.arange(s, dtype=np.float64))
    qt, kk, vt = _proj_mixed(x2, [(w[:, :hd], "feat", BF16, HEAD_DIM ** -0.5 * LOG2E),
                                  (w[:, hd:hd + gd], "keys", BF16, 1.0, tail),
                                  (w[:, hd + gd:], "feat_tiles", BF16, 1.0)], seq=s)
    o_t = pl.pallas_call(
        functools.partial(_swa_body, tq=tq),
        grid=(b, nt),
        in_specs=[
            pl.BlockSpec(memory_space=pltpu.SMEM),
            pl.BlockSpec((hd, tq), lambda i, j: (0, i * nt + j)),
            pl.BlockSpec((1, groups, s, ATT_QROWS), lambda i, j: (i, 0, 0, 0)),
            pl.BlockSpec((1, nt, gd, tq), lambda i, j: (i, 0, 0, 0)),
        ],
        out_specs=pl.BlockSpec((hd, tq), lambda i, j: (0, i * nt + j)),
        out_shape=jax.ShapeDtypeStruct((hd, n), BF16),
        compiler_params=_cparams("parallel", "parallel"),
        name="swa_core",
    )(sinks.astype(F32), qt, kk, vt.reshape(b, nt, gd, tq))
    return _oproj_ln(o_t, w_out, x2, ln_g, ln_b, feature_major=True).reshape(b, s, d)


POOL_TM = 512
POOL_HALO = 16
POOL_LEAD = 8


def _pool_body(x_ref, halo_ref, w_ref, sc_ref, g_ref, b_ref, o_ref, ext_ref, tmp_ref, *, tm):
    t = pl.program_id(1)
    x = x_ref[0]
    lead = POOL_LEAD
    top = lead + POOL_HALO
    rows = top + tm
    ext_ref[0:lead, :] = jnp.zeros((lead, x.shape[1]), F32)
    tmp_ref[0:lead, :] = jnp.zeros((lead, x.shape[1]), F32)
    ext_ref[lead:top, :] = jnp.where(t > 0, halo_ref[0], 0.0)
    ext_ref[top:, :] = x
    row = t * tm + lax.broadcasted_iota(jnp.int32, (tm, 1), 0)
    ys = []
    for gi, w in enumerate(POOL_WINDOWS):
        cs = slice(gi * POOL_GROUP, (gi + 1) * POOL_GROUP)
        xg = x[:, cs]
        src, dst = ext_ref, tmp_ref
        shift = 1
        while shift < w:
            dst[lead:rows, cs] = src[lead:rows, cs] + src[lead - shift:rows - shift, cs]
            src, dst = dst, src
            shift *= 2
        acc = src[top:rows, cs]
        cnt = jnp.minimum(row + 1, w).astype(F32)
        diff = (acc / cnt - xg).astype(BF16)
        ys.append(_dot(diff, w_ref[gi]))
    y = jnp.concatenate(ys, axis=-1) * sc_ref[...]
    o_ref[0] = _layer_norm(DN_ALPHA * x + y, g_ref[...], b_ref[...])


def _pool_mixer_ln(x, w_grp, scale, ln_g, ln_b):
    b, s, d = x.shape
    tm = min(POOL_TM, s)
    hb = tm // POOL_HALO
    ng = len(POOL_WINDOWS)
    assert all(w & (w - 1) == 0 and w // 2 <= POOL_LEAD and w <= POOL_HALO for w in POOL_WINDOWS)
    return pl.pallas_call(
        functools.partial(_pool_body, tm=tm),
        grid=(b, s // tm),
        in_specs=[
            pl.BlockSpec((1, tm, d), lambda i, j: (i, j, 0)),
            pl.BlockSpec((1, POOL_HALO, d), lambda i, j: (i, jnp.maximum(j * hb - 1, 0), 0)),
            _full((ng, POOL_GROUP, POOL_GROUP)),
            _full((1, d)),
            _full((1, d)),
            _full((1, d)),
        ],
        out_specs=pl.BlockSpec((1, tm, d), lambda i, j: (i, j, 0)),
        out_shape=jax.ShapeDtypeStruct((b, s, d), F32),
        scratch_shapes=[pltpu.VMEM((POOL_LEAD + POOL_HALO + tm, d), F32)] * 2,
        compiler_params=_cparams("parallel", "parallel"),
        name="pool_ln",
    )(x, x, w_grp.astype(BF16), scale.reshape(1, d), ln_g.reshape(1, d), ln_b.reshape(1, d))


GLA_STEP_CHUNKS = 4


def _gla_tables(c):
    levels = []
    s = c
    while s >= 1:
        levels.append(s)
        s //= 2
    rows, masks = [], []
    idx = np.arange(c)
    for s in levels:
        dq = np.zeros((c, c), np.float32)
        dk = np.zeros((c, c), np.float32)
        for i in range(c):
            blk = i // s
            if s == c or blk % 2 == 1:
                dq[i, blk * s:i + 1] = 1.0
            dk[i, i + 1:blk * s + s] = 1.0
        rows.append(dq)
        if s > 1:
            rows.append(dk)
        if s < c:
            masks.append(((idx[:, None] // (2 * s) == idx[None, :] // (2 * s))
                          & ((idx[:, None] // s) % 2 == 1) & ((idx[None, :] // s) % 2 == 0)))
    masks.append(np.eye(c, dtype=bool))
    return np.concatenate(rows, 0), np.stack(masks).astype(np.float32), len(levels)


def _gla_body(q_ref, k_ref, v_ref, glr_ref, r_ref, x_ref, wg2_ref, bg_ref, ng_ref, dall_ref, mask_ref,
              wo_ref, lg_ref, lb_ref, o_ref, st_ref, *, nlev, nck):
    c = GLA_CHUNK
    heads = GLA_HEADS
    dk = GLA_DK // heads
    dv = GLA_DV // heads

    @pl.when(pl.program_id(1) == 0)
    def _():
        st_ref[...] = jnp.zeros_like(st_ref)

    z = _dot(glr_ref[0].astype(BF16), wg2_ref[...]) + bg_ref[...]
    log_a = (jnp.minimum(z, 0.0) - jnp.log1p(jnp.exp(-jnp.abs(z)))) * (1.0 / GLA_TAU)
    hi = log_a.astype(BF16)
    lo = (log_a - hi.astype(F32)).astype(BF16)
    parts = []
    for ck in range(nck):
        parts += [hi[ck * c:(ck + 1) * c], lo[ck * c:(ck + 1) * c]]
    e2 = _dot(dall_ref[...], jnp.concatenate(parts, axis=-1))
    q = q_ref[0]
    k = k_ref[0]
    pre = []
    for ck in range(nck):
        rows = slice(ck * c, (ck + 1) * c)
        decay = jnp.exp(e2[:, 2 * ck * GLA_DK:(2 * ck + 1) * GLA_DK]
                        + e2[:, (2 * ck + 1) * GLA_DK:(2 * ck + 2) * GLA_DK])
        for h in range(heads):
            ks = slice(h * dk, (h + 1) * dk)
            qh = q[rows, ks]
            kh = k[rows, ks]
            tbl = lambda i: decay[i * c:(i + 1) * c, ks]
            q_in = (qh * tbl(0)).astype(BF16)
            k_out = (kh * tbl(1)).astype(BF16)
            khb = kh.astype(BF16)
            att = mask_ref[nlev - 1] * _dot_nt(qh.astype(BF16), khb)
            for li in range(nlev - 1):
                ql = (qh * tbl(2 + 2 * li)).astype(BF16)
                kl = (kh * tbl(3 + 2 * li)).astype(BF16) if li < nlev - 2 else khb
                att = att + mask_ref[li] * _dot_nt(ql, kl)
            total = decay[c - 1:c, ks]
            pre.append((att.astype(BF16), q_in, k_out, total))
    states = [st_ref[h] for h in range(heads)]
    r = r_ref[0]
    gate = r * jax.nn.sigmoid(r)
    gated = []
    for ck in range(nck):
        rows = slice(ck * c, (ck + 1) * c)
        outs = []
        for h in range(heads):
            att, q_in, k_out, total = pre[ck * heads + h]
            vh = v_ref[0, rows, h * dv:(h + 1) * dv]
            out = _dot(att, vh) + _dot_nt(q_in, states[h].astype(BF16))
            states[h] = states[h] * total + _dot_tn(vh, k_out)
            outs.append(out * lax.rsqrt(jnp.mean(out * out, axis=-1, keepdims=True) + LN_EPS) * ng_ref[...])
        gated.append((jnp.concatenate(outs, axis=-1) * gate[rows]).astype(BF16))
    for h in range(heads):
        st_ref[h] = states[h]
    y = _dot(jnp.concatenate(gated, axis=0), wo_ref[...])
    o_ref[0] = _layer_norm(DN_ALPHA * x_ref[0] + y, lg_ref[...], lb_ref[...])


def _gla_mixer_ln(x, w_in, w_gate2, b_gate, norm_g, w_out, ln_g, ln_b):
    b, s, d = x.shape
    c = GLA_CHUNK
    heads = GLA_HEADS
    dk = GLA_DK // heads
    dv = GLA_DV // heads
    x2 = x.reshape(b * s, d)
    w = w_in.astype(BF16)
    o0, o1, o2, o3 = GLA_DK, 2 * GLA_DK, 2 * GLA_DK + GLA_DV, 2 * GLA_DK + GLA_DV + GLA_GATE_RANK
    q, k, v, glr, r = _proj(
        x2, [w[:, :o0], w[:, o0:o1], w[:, o1:o2], _pad_cols(w[:, o2:o3], LANES), w[:, o3:]],
        [F32, F32, BF16, F32, F32], scales=(dk ** -0.5, 1.0, 1.0, 1.0, 1.0))
    wg2 = jnp.pad(w_gate2.astype(BF16), ((0, LANES - GLA_GATE_RANK), (0, 0)))
    dall, masks, nlev = _gla_tables(c)
    nck = GLA_STEP_CHUNKS
    tok = lambda width: pl.BlockSpec((1, nck * c, width), lambda i, j: (i, j, 0))
    return pl.pallas_call(
        functools.partial(_gla_body, nlev=nlev, nck=nck),
        grid=(b, s // (nck * c)),
        in_specs=[
            tok(GLA_DK), tok(GLA_DK), tok(GLA_DV), tok(LANES), tok(GLA_DV), tok(d),
            _full((LANES, GLA_DK)),
            _full((1, GLA_DK)),
            _full((1, dv)),
            _full(dall.shape),
            _full(masks.shape),
            _full((GLA_DV, d)),
            _full((1, d)),
            _full((1, d)),
        ],
        out_specs=tok(d),
        out_shape=jax.ShapeDtypeStruct((b, s, d), F32),
        scratch_shapes=[pltpu.VMEM((heads, dv, dk), F32)],
        compiler_params=_cparams("parallel", "arbitrary"),
        name="gla_core",
    )(q.reshape(b, s, -1), k.reshape(b, s, -1), v.reshape(b, s, -1), glr.reshape(b, s, -1),
      r.reshape(b, s, -1), x, wg2, b_gate.reshape(1, -1), norm_g.reshape(1, -1),
      jnp.asarray(dall, BF16), jnp.asarray(masks, F32), w_out.astype(BF16), ln_g.reshape(1, d), ln_b.reshape(1, d))


def _gelu_tanh(x):
    return 0.5 * x * (1.0 + jnp.tanh(np.sqrt(2.0 / np.pi).astype(np.float32) * (x + 0.044715 * (x * x * x))))


def _cmp_body(kc_ref, vc_ref, plo_ref, phi_ref, w1a_ref, w1b_ref, w2_ref, o_ref):
    nch, width = o_ref.shape[1], o_ref.shape[2]
    st = kc_ref.shape[1] // nch
    a = jnp.zeros((nch, width), F32)
    bm = jnp.zeros((nch, width), F32)
    for l in range(st):
        c = jnp.concatenate([kc_ref[0, pl.ds(l, nch, stride=st), :], vc_ref[0, pl.ds(l, nch, stride=st), :]], axis=1)
        ws = slice(l * width, (l + 1) * width)
        a = a + _dot((c + plo_ref[:, ws]).astype(BF16), w1a_ref[ws, :])
        bm = bm + _dot((c + phi_ref[:, ws]).astype(BF16), w1b_ref[ws, :])
    nxt = jnp.concatenate([bm[1:], jnp.zeros_like(bm[:1])], axis=0)
    row = lax.broadcasted_iota(jnp.int32, a.shape, 0)
    pre = jnp.where(row < a.shape[0] - 1, a + nxt, 0.0)
    o_ref[0] = _dot(_gelu_tanh(pre).astype(BF16), w2_ref[...]).astype(o_ref.dtype)


def _nsa_compress(kvc, cmp_pos, cmp_w1, cmp_w2):
    b, s, _ = kvc.shape
    half = LANES
    st, g, d = NSA_CMP_STRIDE, NSA_KV_GROUPS, HEAD_DIM
    nch = s // st
    width = 2 * g * d
    eye = jnp.eye(2 * g, dtype=F32)
    sel = jnp.repeat(jnp.eye(2, dtype=F32), g, axis=1)

    def expand(w1_half):
        wj = jnp.einsum('jlde,jc->lcde', w1_half, sel)
        return jnp.einsum('lcde,cf->lcdfe', wj, eye).reshape(st * width, width).astype(BF16)

    w1a = expand(cmp_w1[:, :st])
    w1b = expand(cmp_w1[:, st:])
    w2 = jnp.einsum('cde,cf->cdfe', jnp.einsum('jde,jc->cde', cmp_w2, sel), eye).reshape(width, width).astype(BF16)

    def pos_row(p_half):
        return jnp.einsum('jld,jc->lcd', p_half, sel).reshape(1, st * width)

    return pl.pallas_call(
        _cmp_body,
        grid=(b,),
        in_specs=[
            pl.BlockSpec((1, s, half), lambda i: (i, 0, 0)),
            pl.BlockSpec((1, s, half), lambda i: (i, 0, 1)),
            _full((1, st * width)),
            _full((1, st * width)),
            _full((st * width, width)),
            _full((st * width, width)),
            _full((width, width)),
        ],
        out_specs=pl.BlockSpec((1, nch, width), lambda i: (i, 0, 0)),
        out_shape=jax.ShapeDtypeStruct((b, nch, width), BF16),
        compiler_params=_cparams("parallel"),
        name="nsa_compress",
    )(kvc, kvc, pos_row(cmp_pos[:, :st]), pos_row(cmp_pos[:, st:]), w1a, w1b, w2)


def _nsa_overlap_t(s):
    n_cmp_pad = s // NSA_CMP_STRIDE
    n_sel = s // NSA_SEL_LEN
    blk = np.arange(n_cmp_pad) * NSA_CMP_STRIDE
    sel_start = np.arange(n_sel) * NSA_SEL_LEN
    ov = (blk[None, :] < sel_start[:, None] + NSA_SEL_LEN) & (blk[None, :] + NSA_CMP_LEN > sel_start[:, None])
    return ov.astype(np.float32)


NSA_MASK = 1.0e30


def _nsa_body(qt_ref, kc_ref, vct_ref, ks_ref, kw_ref, vt_ref, glt_ref, ovt_ref, o_ref, qt_scr, sa_scr, sb_scr,
              m_scr, acc_scr, *, tq):
    d = HEAD_DIM
    groups = NSA_KV_GROUPS
    rep = N_HEADS // groups
    n_cmp = kc_ref.shape[2]
    n_sel = ovt_ref.shape[0]
    t = pl.program_id(1)
    q0 = t * tq
    win_tiles = NSA_WINDOW // tq
    gates = jax.nn.sigmoid(glt_ref[...])

    tpos = (q0 + lax.broadcasted_iota(jnp.int32, (1, tq), 1)).astype(F32)
    brow = lax.broadcasted_iota(jnp.int32, (ATT_BIAS_ROWS, tq), 0)
    ones_rows = jnp.where(brow == 0, 1.0, 0.0).astype(BF16)
    ci = lax.broadcasted_iota(jnp.int32, (tq, tq), 0)
    ai = lax.broadcasted_iota(jnp.int32, (tq, tq), 1)
    tile8 = lambda m: jnp.concatenate([m] * rep, axis=1)
    diag_add = tile8(jnp.where(ci <= ai, 0.0, NEG_INF))
    part_add = tile8(jnp.where((ci > ai) & (t >= win_tiles), 0.0, NEG_INF))
    ncm = lax.broadcasted_iota(jnp.int32, (n_cmp, tq), 0)
    tcm = q0 + lax.broadcasted_iota(jnp.int32, (n_cmp, tq), 1)
    valid_c = ncm * NSA_CMP_STRIDE + (NSA_CMP_LEN - 1) <= tcm
    jm = lax.broadcasted_iota(jnp.int32, (n_sel, tq), 0)
    cur = (q0 + lax.broadcasted_iota(jnp.int32, (n_sel, tq), 1)) // NSA_SEL_LEN
    causal_sel = jm <= cur
    forced = (jm == 0) | (jm == cur) | (jm == cur - 1)
    pad_rows = jnp.zeros((ATT_QROWS - d - ATT_BIAS_ROWS - n_sel, tq), BF16)

    o_cmp = []
    for g in range(groups):
        heads = [(qt_ref[h * d:(h + 1) * d, :], _alibi_rows(h, tpos, brow)) for h in range(g * rep, (g + 1) * rep)]

        def build(mask_rows):
            cols = [jnp.concatenate([qh, bias, mask_rows, pad_rows], axis=0) for qh, bias in heads]
            return jnp.concatenate(cols, axis=1)

        qt_scr[2 * g] = build(jnp.zeros((n_sel, tq), BF16))

        s_c = _dot(kc_ref[0, g], qt_scr[2 * g])
        p_sum = jnp.zeros((n_cmp, tq), F32)
        ps = []
        for r in range(rep):
            sr = jnp.where(valid_c, s_c[:, r * tq:(r + 1) * tq], NEG_INF)
            m = jnp.max(sr, axis=0, keepdims=True)
            e = jnp.where(valid_c, jnp.exp2(sr - m), 0.0)
            den = jnp.sum(e, axis=0, keepdims=True)
            p = e * jnp.where(den > 0.0, 1.0 / den, 0.0)
            p_sum = p_sum + p
            ps.append(p.astype(BF16))
        o_cmp.append(_dot(vct_ref[0, g * d:(g + 1) * d, :], jnp.concatenate(ps, axis=1)))

        p_hi = p_sum.astype(BF16)
        p_lo = (p_sum - p_hi.astype(F32)).astype(BF16)
        ovt = ovt_ref[...]
        imp_t = _dot(ovt, p_hi) + _dot(ovt, p_lo)
        score = jnp.where(causal_sel, imp_t + jnp.where(forced, NSA_FORCE_BONUS, 0.0), NEG_INF)
        rank = jnp.zeros((n_sel, tq), F32)
        for mp in range(n_sel):
            row = score[mp:mp + 1, :]
            ahead = (row > score) | ((row == score) & (mp < jm))
            rank = rank + jnp.where(ahead, 1.0, 0.0)
        selected = (rank < NSA_TOPK) & causal_sel
        qt_scr[2 * g + 1] = build(jnp.where(selected, 0.0, -NSA_MASK).astype(BF16))

    def score(chains, kt, slot):
        k0 = pl.multiple_of(kt * tq, tq)
        for ci, (k_ref, g, _, qi, _) in enumerate(chains):
            slot[ci] = _dot(k_ref[0, g, pl.ds(k0, tq), :], qt_scr[qi])

    def absorb(chains, kt, add, slot):
        for ci, (_, _, v_row0, _, si) in enumerate(chains):
            s = slot[ci]
            if add is not None:
                s = s + add
            m_prev = m_scr[si]
            m_new = jnp.maximum(m_prev, jnp.max(s, axis=0, keepdims=True))
            e = jnp.exp2(s - m_new)
            alpha = jnp.exp2(m_prev - m_new)
            vals = jnp.concatenate([vt_ref[0, kt, v_row0:v_row0 + d, :], ones_rows], axis=0)
            acc_scr[si] = alpha * acc_scr[si] + _dot(vals, e.astype(BF16))
            m_scr[si] = m_new

    def sweep(chains, lo):
        n = t - lo

        def pair(j, carry):
            a = lo + 2 * j
            score(chains, a + 1, sb_scr)
            absorb(chains, a, None, sa_scr)
            score(chains, a + 2, sa_scr)
            absorb(chains, a + 1, None, sb_scr)
            return carry

        lax.fori_loop(0, n // 2, pair, 0)

        @pl.when(n % 2 == 1)
        def _():
            score(chains, t, sb_scr)
            absorb(chains, t - 1, None, sa_scr)
            absorb(chains, t, diag_add, sb_scr)

        @pl.when(n % 2 == 0)
        def _():
            absorb(chains, t, diag_add, sa_scr)

    m_scr[...] = jnp.full(m_scr.shape, NEG_INF, F32)
    acc_scr[...] = jnp.zeros(acc_scr.shape, F32)

    sel_chains = [(ks_ref, g, g * d, 2 * g + 1, g) for g in range(groups)]
    score(sel_chains, 0, sa_scr)
    sweep(sel_chains, 0)

    win_chains = [(kw_ref, g, (groups + g) * d, 2 * g, groups + g) for g in range(groups)]
    oldest = jnp.maximum(t - win_tiles, 0)
    lo = jnp.maximum(t - win_tiles + 1, 0)
    score(win_chains, oldest, sb_scr)
    score(win_chains, jnp.minimum(lo, t), sa_scr)
    absorb(win_chains, oldest, part_add, sb_scr)
    sweep(win_chains, lo)

    o_sel = [acc_scr[g, :d] / acc_scr[g, d:d + 1] for g in range(groups)]
    o_win = [acc_scr[groups + g, :d] / acc_scr[groups + g, d:d + 1] for g in range(groups)]

    for g in range(groups):
        for r in range(rep):
            h = g * rep + r
            cs = slice(r * tq, (r + 1) * tq)
            o = (gates[h:h + 1] * o_cmp[g][:, cs] + gates[N_HEADS + h:N_HEADS + h + 1] * o_sel[g][:, cs]
                 + gates[2 * N_HEADS + h:2 * N_HEADS + h + 1] * o_win[g][:, cs])
            o_ref[h * d:(h + 1) * d, :] = o.astype(o_ref.dtype)


def _nsa_mixer_ln(x, w_in, cmp_pos, cmp_w1, cmp_w2, w_out, ln_g, ln_b):
    b, s, d = x.shape
    hd = N_HEADS * HEAD_DIM
    groups = NSA_KV_GROUPS
    gd = groups * HEAD_DIM
    n = b * s
    tq = ATT_TQ
    n_sel = s // NSA_SEL_LEN
    n_cmp = s // NSA_CMP_STRIDE
    assert HEAD_DIM + ATT_BIAS_ROWS + n_sel <= ATT_QROWS and n_sel % 16 == 0 and NSA_WINDOW % tq == 0
    x2 = x.reshape(n, d)
    w = w_in.astype(BF16)
    c0 = hd + 2 * gd
    w_k2 = jnp.concatenate([w[:, c0:c0 + gd], w[:, c0 + 2 * gd:c0 + 3 * gd]], axis=1)
    w_v2 = jnp.concatenate([w[:, c0 + gd:c0 + 2 * gd], w[:, c0 + 3 * gd:c0 + 4 * gd]], axis=1)
    tok = np.arange(s, dtype=np.float64)
    qt, kvc, k_sw, vt, glt = _proj_mixed(
        x2, [(w[:, :hd], "feat", BF16, HEAD_DIM ** -0.5 * LOG2E),
             (w[:, hd:c0], "tok", F32, 1.0),
             (w_k2, "keys", BF16, 1.0, _key_tail(tok, n_sel, NSA_SEL_LEN)),
             (w_v2, "feat_tiles", BF16, 1.0),
             (_pad_cols(w[:, c0 + 4 * gd:], LANES), "feat", F32, 1.0)], seq=s)

    cmp = _nsa_compress(kvc.reshape(b, s, 2 * gd), cmp_pos, cmp_w1, cmp_w2)
    cpos = np.arange(n_cmp, dtype=np.float64) * NSA_CMP_STRIDE + (NSA_CMP_LEN - 1) / 2
    k_cmp = _keys_with_tail(cmp[:, :, :gd], groups, _key_tail(cpos))
    v_cmp_t = cmp[:, :, gd:].transpose(0, 2, 1)
    vt = vt.reshape(b, s // tq, 2 * gd, tq)
    ovt = _nsa_overlap_t(s)
    nt = s // tq
    o_t = pl.pallas_call(
        functools.partial(_nsa_body, tq=tq),
        grid=(b, nt),
        in_specs=[
            pl.BlockSpec((hd, tq), lambda i, j: (0, i * nt + j)),
            pl.BlockSpec((1, groups, n_cmp, ATT_QROWS), lambda i, j: (i, 0, 0, 0)),
            pl.BlockSpec((1, gd, n_cmp), lambda i, j: (i, 0, 0)),
            pl.BlockSpec((1, groups, s, ATT_QROWS), lambda i, j: (i, 0, 0, 0)),
            pl.BlockSpec((1, groups, s, ATT_QROWS), lambda i, j: (i, 1, 0, 0)),
            pl.BlockSpec((1, nt, 2 * gd, tq), lambda i, j: (i, 0, 0, 0)),
            pl.BlockSpec((LANES, tq), lambda i, j: (0, i * nt + j)),
            _full(ovt.shape),
        ],
        out_specs=pl.BlockSpec((hd, tq), lambda i, j: (0, i * nt + j)),
        out_shape=jax.ShapeDtypeStruct((hd, n), BF16),
        scratch_shapes=[pltpu.VMEM((2 * groups, ATT_QROWS, (N_HEADS // groups) * tq), BF16),
                        pltpu.VMEM((groups, tq, (N_HEADS // groups) * tq), F32),
                        pltpu.VMEM((groups, tq, (N_HEADS // groups) * tq), F32),
                        pltpu.VMEM((2 * groups, 1, (N_HEADS // groups) * tq), F32),
                        pltpu.VMEM((2 * groups, HEAD_DIM + ATT_BIAS_ROWS, (N_HEADS // groups) * tq), F32)],
        compiler_params=_cparams("parallel", "parallel"),
        name="nsa_core",
    )(qt, k_cmp, v_cmp_t, k_sw, k_sw, vt, glt, jnp.asarray(ovt, BF16))
    return _oproj_ln(o_t, w_out, x2, ln_g, ln_b, feature_major=True).reshape(b, s, d)


def kernel(x, mem, ln_g, ln_b, ffn1_w_gate, ffn1_w_up, ffn1_w_down, ffn2_w_gate, ffn2_w_up, ffn2_w_down,
           xattn_w_q, xattn_w_kv, xattn_w_o, swa_w_in, swa_sinks, swa_w_out, nsa_w_in, nsa_cmp_pos,
           nsa_cmp_w1, nsa_cmp_w2, nsa_w_out, gla_w_in, gla_w_gate2, gla_b_gate, gla_norm_g, gla_w_out,
           pool_w, pool_scale):
    b, s, d = x.shape
    ffn1 = [w.astype(BF16) for w in (ffn1_w_gate, ffn1_w_up, ffn1_w_down)]
    ffn2 = [w.astype(BF16) for w in (ffn2_w_gate, ffn2_w_up, ffn2_w_down)]
    xq, xkv, xo = (w.astype(BF16) for w in (xattn_w_q, xattn_w_kv, xattn_w_o))
    for i in range(DEPTH):
        kind = i % N_MIXERS
        j = i // N_MIXERS
        x = _ffn_ln(x.reshape(b * s, d), *ffn1, i, ln_g[i, 0], ln_b[i, 0]).reshape(b, s, d)
        if kind == 0:
            x = _swa_mixer_ln(x, swa_w_in[j], swa_sinks[j], swa_w_out[j], ln_g[i, 1], ln_b[i, 1])
        elif kind == 1:
            x = _nsa_mixer_ln(x, nsa_w_in[j], nsa_cmp_pos[j], nsa_cmp_w1[j], nsa_cmp_w2[j], nsa_w_out[j],
                              ln_g[i, 1], ln_b[i, 1])
        elif kind == 2:
            x = _gla_mixer_ln(x, gla_w_in[j], gla_w_gate2[j], gla_b_gate[j], gla_norm_g[j], gla_w_out[j],
                              ln_g[i, 1], ln_b[i, 1])
        else:
            x = _pool_mixer_ln(x, pool_w[j], pool_scale[j], ln_g[i, 1], ln_b[i, 1])
        x = _xattn_ln(x, mem, xq, xkv[i], xo, i, ln_g[i, 2], ln_b[i, 2])
        x = _ffn_ln(x.reshape(b * s, d), *ffn2, i, ln_g[i, 3], ln_b[i, 3]).reshape(b, s, d)
    return x
```

```python
import functools

import numpy as np
import jax
import jax.numpy as jnp
from jax import lax
from jax.experimental import pallas as pl
from jax.experimental.pallas import tpu as pltpu

F32 = jnp.float32
BF16 = jnp.bfloat16

D_MODEL = 1024
DEPTH = 4
N_MIXERS = 4
HEAD_DIM = 64
N_HEADS = D_MODEL // HEAD_DIM
SWA_KV_HEADS = 4
SWA_WINDOW = 128
NSA_KV_GROUPS = 2
NSA_CMP_LEN = 32
NSA_CMP_STRIDE = 16
NSA_SEL_LEN = 64
NSA_TOPK = 8
NSA_WINDOW = 512
NSA_FORCE_BONUS = 1.0e4
GLA_HEADS = 4
GLA_DK = D_MODEL // 2
GLA_DV = D_MODEL
GLA_GATE_RANK = 16
GLA_TAU = 16.0
GLA_CHUNK = 64
POOL_WINDOWS = (2, 4, 8, 16)
POOL_GROUP = D_MODEL // 4
XATTN_HEADS = 4
DN_ALPHA = (2 * DEPTH) ** 0.25
LN_EPS = 1e-5
NEG_INF = -1e30
LOG2E = float(np.log2(np.e))

LANES = 128
V7X_VMEM_LIMIT_BYTES = 56 * 1024 * 1024

_NT = (((1,), (1,)), ((), ()))
_TN = (((0,), (0,)), ((), ()))


def _cparams(*sem, flags=None):
    return pltpu.CompilerParams(dimension_semantics=sem, vmem_limit_bytes=V7X_VMEM_LIMIT_BYTES, flags=flags)


def _dot(a, b):
    return jnp.dot(a, b, preferred_element_type=F32)


def _dot_nt(a, b):
    return lax.dot_general(a, b, _NT, preferred_element_type=F32)


def _dot_tn(a, b):
    return lax.dot_general(a, b, _TN, preferred_element_type=F32)


def _layer_norm(y, g, b):
    mu = jnp.mean(y, axis=-1, keepdims=True)
    yc = y - mu
    var = jnp.mean(yc * yc, axis=-1, keepdims=True)
    return yc * lax.rsqrt(var + LN_EPS) * g + b


def _alibi_slope(h, n):
    return float(2.0 ** (-8.0 * (h + 1) / n))


def _full(shape):
    nd = len(shape)
    return pl.BlockSpec(shape, lambda *_: (0,) * nd, pipeline_mode=pl.Buffered(1))


FFN_TM = 1024
FFN_SUB = 512
FFN_TF = 256


def _ffn_body(x_ref, wg_ref, wu_ref, wd_ref, g_ref, b_ref, o_ref, acc_ref, *, tf, sub):
    nf = wd_ref.shape[0] // tf
    for si in range(x_ref.shape[0] // sub):
        rows = slice(si * sub, (si + 1) * sub)
        x = x_ref[rows, :]
        xb = x.astype(BF16)
        for c in range(nf):
            cs = slice(c * tf, (c + 1) * tf)
            gate = _dot(xb, wg_ref[:, cs])
            up = _dot(xb, wu_ref[:, cs])
            h = (gate * jax.nn.sigmoid(gate) * up).astype(BF16)
            d = _dot(h, wd_ref[cs, :])
            if c == 0:
                acc_ref[si] = d
            else:
                acc_ref[si] += d
        y = DN_ALPHA * x + 0.5 * acc_ref[si]
        o_ref[rows, :] = _layer_norm(y, g_ref[...], b_ref[...])


def _layer_of(stacked, layer):
    _, r, c = stacked.shape
    return pl.BlockSpec((None, r, c), lambda *_: (layer, 0, 0), pipeline_mode=pl.Buffered(1))


def _ffn_ln(x2, w_gate, w_up, w_down, layer, ln_g, ln_b):
    n, d = x2.shape
    tm = min(FFN_TM, n)
    sub = min(FFN_SUB, tm)
    return pl.pallas_call(
        functools.partial(_ffn_body, tf=FFN_TF, sub=sub),
        grid=(n // tm,),
        in_specs=[
            pl.BlockSpec((tm, d), lambda i: (i, 0)),
            _layer_of(w_gate, layer),
            _layer_of(w_up, layer),
            _layer_of(w_down, layer),
            _full((1, d)),
            _full((1, d)),
        ],
        out_specs=pl.BlockSpec((tm, d), lambda i: (i, 0)),
        out_shape=jax.ShapeDtypeStruct((n, d), F32),
        scratch_shapes=[pltpu.VMEM((tm // sub, sub, d), F32)],
        compiler_params=_cparams("parallel"),
        name="ffn_ln",
    )(x2, w_gate, w_up, w_down, ln_g.reshape(1, d), ln_b.reshape(1, d))


PROJ_TM = 512


def _proj_body(x_ref, *refs, scales):
    n = len(scales)
    xb = x_ref[...].astype(BF16)
    for w_ref, o_ref, s in zip(refs[:n], refs[n:], scales):
        r = _dot(xb, w_ref[...])
        if s != 1.0:
            r = r * s
        o_ref[...] = r.astype(o_ref.dtype)


def _proj(x2, weights, dtypes, scales=None):
    n, k = x2.shape
    scales = tuple(scales) if scales is not None else (1.0,) * len(weights)
    tm = min(PROJ_TM, n)
    in_specs = [pl.BlockSpec((tm, k), lambda i: (i, 0))] + [_full(w.shape) for w in weights]
    out_specs = [pl.BlockSpec((tm, w.shape[1]), lambda i: (i, 0)) for w in weights]
    out_shape = [jax.ShapeDtypeStruct((n, w.shape[1]), dt) for w, dt in zip(weights, dtypes)]
    return pl.pallas_call(
        functools.partial(_proj_body, scales=scales),
        grid=(n // tm,),
        in_specs=in_specs,
        out_specs=out_specs,
        out_shape=out_shape,
        compiler_params=_cparams("parallel"),
        name="proj",
    )(x2, *weights)


def _pad_cols(w, m):
    return jnp.pad(w, ((0, 0), (0, m - w.shape[1])))


OPROJ_TM = 1024


def _oproj_body(o_ref, w_ref, x_ref, g_ref, b_ref, out_ref, *, feature_major):
    y = _dot_tn(o_ref[...], w_ref[...]) if feature_major else _dot(o_ref[...], w_ref[...])
    out_ref[...] = _layer_norm(DN_ALPHA * x_ref[...] + y, g_ref[...], b_ref[...])


def _oproj_ln(o2, w_out, x2, ln_g, ln_b, feature_major=False):
    n, d = x2.shape
    k = w_out.shape[0]
    tm = min(OPROJ_TM, n)
    return pl.pallas_call(
        functools.partial(_oproj_body, feature_major=feature_major),
        grid=(n // tm,),
        in_specs=[
            pl.BlockSpec((k, tm), lambda i: (0, i)) if feature_major else pl.BlockSpec((tm, k), lambda i: (i, 0)),
            _full((k, d)),
            pl.BlockSpec((tm, d), lambda i: (i, 0)),
            _full((1, d)),
            _full((1, d)),
        ],
        out_specs=pl.BlockSpec((tm, d), lambda i: (i, 0)),
        out_shape=jax.ShapeDtypeStruct((n, d), F32),
        compiler_params=_cparams("parallel"),
        name="oproj_ln",
    )(o2, w_out.astype(BF16), x2, ln_g.reshape(1, d), ln_b.reshape(1, d))


XATTN_TQ = 1024
XATTN_SUB = 512


def _xattn_body(x_ref, kv_ref, wq_ref, wo_ref, g_ref, b_ref, o_ref, *, heads, sub):
    d = x_ref.shape[-1]
    dh = d // heads
    chains = [slice(si * sub, (si + 1) * sub) for si in range(x_ref.shape[1] // sub)]
    xs = [x_ref[0, rows, :] for rows in chains]
    qs = [(_dot(x.astype(BF16), wq_ref[...]) * dh ** -0.5).astype(BF16) for x in xs]
    scores = [[_dot_nt(q[:, h * dh:(h + 1) * dh], kv_ref[0, :, h * dh:(h + 1) * dh]) for h in range(heads)]
              for q in qs]
    outs = [[] for _ in chains]
    for h in range(heads):
        vh = kv_ref[0, :, d + h * dh:d + (h + 1) * dh]
        for ci in range(len(chains)):
            s = scores[ci][h]
            m = jnp.max(s, axis=-1, keepdims=True)
            e = jnp.exp(s - m)
            p = e / jnp.sum(e, axis=-1, keepdims=True)
            outs[ci].append(_dot(p.astype(BF16), vh).astype(BF16))
    ys = [_dot(jnp.concatenate(o, axis=-1), wo_ref[...]) for o in outs]
    for rows, x, y in zip(chains, xs, ys):
        o_ref[0, rows, :] = _layer_norm(DN_ALPHA * x + y, g_ref[...], b_ref[...])


def _xattn_ln(x, mem, w_q, w_kv, w_o, layer, ln_g, ln_b):
    b, s, d = x.shape
    m = mem.shape[1]
    (kv,) = _proj(mem.reshape(b * m, d), [w_kv], [BF16])
    kv = kv.reshape(b, m, 2 * d)
    tq = min(XATTN_TQ, s)
    return pl.pallas_call(
        functools.partial(_xattn_body, heads=XATTN_HEADS, sub=min(XATTN_SUB, tq)),
        grid=(b, s // tq),
        in_specs=[
            pl.BlockSpec((1, tq, d), lambda i, j: (i, j, 0)),
            pl.BlockSpec((1, m, 2 * d), lambda i, j: (i, 0, 0)),
            _layer_of(w_q, layer),
            _layer_of(w_o, layer),
            _full((1, d)),
            _full((1, d)),
        ],
        out_specs=pl.BlockSpec((1, tq, d), lambda i, j: (i, j, 0)),
        out_shape=jax.ShapeDtypeStruct((b, s, d), F32),
        compiler_params=_cparams("parallel", "parallel"),
        name="xattn_ln",
    )(x, kv, w_q, w_o, ln_g.reshape(1, d), ln_b.reshape(1, d))


ATT_TQ = 128
ATT_QROWS = 128
ATT_BIAS_ROWS = 16


def _split3_bf16(x):
    out = []
    r = np.float32(x)
    for _ in range(3):
        p = np.float32(np.asarray(r, np.float32).astype(BF16).astype(np.float32))
        out.append(float(p))
        r = np.float32(r - p)
    return out


def _alibi_rows(h, tpos, brow):
    slope = _alibi_slope(h, N_HEADS) * LOG2E
    pieces = _split3_bf16(slope * LANES) + _split3_bf16(slope)
    v = -np.float32(slope) * tpos
    v_hi = v.astype(BF16).astype(F32)
    v_mid = (v - v_hi).astype(BF16).astype(F32)
    v_lo = v - v_hi - v_mid
    bias = jnp.zeros(brow.shape, F32)
    for k, val in enumerate(pieces + [v_hi, v_mid, v_lo]):
        bias = jnp.where(brow == k, val, bias)
    return bias.astype(BF16)


def _key_tail(pos, n_blocks=0, block_len=1):
    n = pos.shape[0]
    hi = np.floor(pos / LANES)
    lo = pos - hi * LANES
    tail = np.zeros((n, ATT_QROWS - HEAD_DIM), np.float32)
    tail[:, 0:3] = hi[:, None]
    tail[:, 3:6] = lo[:, None]
    tail[:, 6:9] = 1.0
    if n_blocks:
        blk = (pos // block_len).astype(np.int64)
        tail[np.arange(n), ATT_BIAS_ROWS + blk] = 1.0
    return tail


def _keys_with_tail(k_tok, groups, tail):
    b, rows, _ = k_tok.shape
    k = k_tok.reshape(b, rows, groups, HEAD_DIM).transpose(0, 2, 1, 3)
    t_b = jnp.broadcast_to(jnp.asarray(tail, BF16), (b, groups, rows, tail.shape[1]))
    return jnp.concatenate([k, t_b], axis=-1)


def _proj_mixed_body(x_ref, *refs, kinds, scales, tq):
    n = len(kinds)
    xb = x_ref[...].astype(BF16)
    tails = iter(refs[n:len(refs) - n])
    for w_ref, o_ref, kind, s in zip(refs[:n], refs[len(refs) - n:], kinds, scales):
        r = _dot(xb, w_ref[...]) if kind in ("tok", "keys") else _dot_nt(w_ref[...], xb)
        if s != 1.0:
            r = r * s
        r = r.astype(o_ref.dtype)
        if kind == "feat_tiles":
            for c in range(o_ref.shape[0]):
                o_ref[c] = r[:, c * tq:(c + 1) * tq]
        elif kind == "keys":
            tail = next(tails)[...]
            for g in range(o_ref.shape[1]):
                o_ref[0, g] = jnp.concatenate([r[:, g * HEAD_DIM:(g + 1) * HEAD_DIM], tail], axis=1)
        else:
            o_ref[...] = r


def _proj_mixed(x2, specs, seq, tq=ATT_TQ):
    n, k = x2.shape
    tm = min(PROJ_TM, seq)
    per_seq = seq // tm
    kinds = tuple(s[1] for s in specs)
    ws, tails, tail_specs, out_specs, out_shape = [], [], [], [], []
    for spec in specs:
        w, kind, dt = spec[:3]
        m = w.shape[1]
        if kind == "tok":
            ws.append(w)
            out_specs.append(pl.BlockSpec((tm, m), lambda i: (i, 0)))
            out_shape.append(jax.ShapeDtypeStruct((n, m), dt))
        elif kind == "feat":
            ws.append(w.T)
            out_specs.append(pl.BlockSpec((m, tm), lambda i: (0, i)))
            out_shape.append(jax.ShapeDtypeStruct((m, n), dt))
        elif kind == "feat_tiles":
            ws.append(w.T)
            out_specs.append(pl.BlockSpec((tm // tq, m, tq), lambda i: (i, 0, 0)))
            out_shape.append(jax.ShapeDtypeStruct((n // tq, m, tq), dt))
        else:
            ws.append(w)
            groups = m // HEAD_DIM
            tails.append(jnp.asarray(spec[4], dt))
            tail_specs.append(pl.BlockSpec((tm, ATT_QROWS - HEAD_DIM), lambda i: (i % per_seq, 0)))
            out_specs.append(pl.BlockSpec((1, groups, tm, ATT_QROWS), lambda i: (i // per_seq, 0, i % per_seq, 0)))
            out_shape.append(jax.ShapeDtypeStruct((n // seq, groups, seq, ATT_QROWS), dt))
    return pl.pallas_call(
        functools.partial(_proj_mixed_body, kinds=kinds, scales=tuple(s[3] for s in specs), tq=tq),
        grid=(n // tm,),
        in_specs=[pl.BlockSpec((tm, k), lambda i: (i, 0))] + [_full(w.shape) for w in ws] + tail_specs,
        out_specs=out_specs,
        out_shape=out_shape,
        compiler_params=_cparams("parallel"),
        name="proj_mixed",
    )(x2, *ws, *tails)


SWA_CHAIN_HEADS = 4
SWA_LOOKAHEAD = 2


def _swa_body(sink_ref, qt_ref, k_ref, vt_ref, o_ref, *, tq):
    d = HEAD_DIM
    groups = SWA_KV_HEADS
    rep = N_HEADS // groups
    t = pl.program_id(1)
    first = t == 0
    base = jnp.maximum(t - 1, 0)
    kb = pl.multiple_of(base * tq, tq)
    tpos = (t * tq + lax.broadcasted_iota(jnp.int32, (1, tq), 1)).astype(F32)
    brow = lax.broadcasted_iota(jnp.int32, (ATT_BIAS_ROWS, tq), 0)
    ones_rows = jnp.where(lax.broadcasted_iota(jnp.int32, (ATT_BIAS_ROWS, 2 * tq), 0) == 0, 1.0, 0.0).astype(BF16)
    ci = lax.broadcasted_iota(jnp.int32, (tq, tq), 0)
    ai = lax.broadcasted_iota(jnp.int32, (tq, tq), 1)
    hb = SWA_CHAIN_HEADS
    older = jnp.where(ci > ai, 0.0, NEG_INF)
    own = jnp.where(ci <= ai, 0.0, NEG_INF)
    add = jnp.concatenate([jnp.where(first, own, older), jnp.where(first, NEG_INF, own)], axis=0)
    add = jnp.concatenate([add] * hb, axis=1)
    pad_rows = jnp.zeros((ATT_QROWS - d - ATT_BIAS_ROWS, tq), BF16)

    def issue(c):
        g = (c * hb) // rep
        qt = jnp.concatenate(
            [jnp.concatenate([qt_ref[h * d:(h + 1) * d, :], _alibi_rows(h, tpos, brow), pad_rows], axis=0)
             for h in range(c * hb, (c + 1) * hb)], axis=1)
        return _dot(k_ref[0, g, pl.ds(kb, 2 * tq), :], qt)

    def finish(c, s):
        g = (c * hb) // rep
        sink = jnp.concatenate([jnp.full((1, tq), sink_ref[h] * LOG2E, F32) for h in range(c * hb, (c + 1) * hb)], axis=1)
        m = jnp.maximum(jnp.max(s, axis=0, keepdims=True), sink)
        e = jnp.exp2(s - m)
        vals = jnp.concatenate([vt_ref[0, base, g * d:(g + 1) * d, :], vt_ref[0, base + 1, g * d:(g + 1) * d, :]], axis=1)
        acc = _dot(jnp.concatenate([vals, ones_rows], axis=0), e.astype(BF16))
        o = acc[:d] / (acc[d:d + 1] + jnp.exp2(sink - m))
        for r in range(hb):
            h = c * hb + r
            o_ref[h * d:(h + 1) * d, :] = o[:, r * tq:(r + 1) * tq].astype(o_ref.dtype)

    n_chains = N_HEADS // hb
    pending = [issue(c) for c in range(min(SWA_LOOKAHEAD, n_chains))]
    s_cur = pending[0] + add
    for c in range(n_chains):
        s_next = pending[c + 1] + add if c + 1 < n_chains else None
        finish(c, s_cur)
        if c + SWA_LOOKAHEAD < n_chains:
            pending.append(issue(c + SWA_LOOKAHEAD))
        s_cur = s_next


def _swa_mixer_ln(x, w_in, sinks, w_out, ln_g, ln_b):
    b, s, d = x.shape
    hd = N_HEADS * HEAD_DIM
    groups = SWA_KV_HEADS
    gd = groups * HEAD_DIM
    n = b * s
    tq = ATT_TQ
    nt = s // tq
    assert SWA_WINDOW == tq and nt >= 2
    x2 = x.reshape(n, d)
    w = w_in.astype(BF16)
    tail = _key_tail(np.arange(s, dtype=np.float64))
    qt, kk, vt = _proj_mixed(x2, [(w[:, :hd], "feat", BF16, HEAD_DIM ** -0.5 * LOG2E),
                                  (w[:, hd:hd + gd], "keys", BF16, 1.0, tail),
                                  (w[:, hd + gd:], "feat_tiles", BF16, 1.0)], seq=s)
    o_t = pl.pallas_call(
        functools.partial(_swa_body, tq=tq),
        grid=(b, nt),
        in_specs=[
            pl.BlockSpec(memory_space=pltpu.SMEM),
            pl.BlockSpec((hd, tq), lambda i, j: (0, i * nt + j)),
            pl.BlockSpec((1, groups, s, ATT_QROWS), lambda i, j: (i, 0, 0, 0)),
            pl.BlockSpec((1, nt, gd, tq), lambda i, j: (i, 0, 0, 0)),
        ],
        out_specs=pl.BlockSpec((hd, tq), lambda i, j: (0, i * nt + j)),
        out_shape=jax.ShapeDtypeStruct((hd, n), BF16),
        compiler_params=_cparams("parallel", "parallel"),
        name="swa_core",
    )(sinks.astype(F32), qt, kk, vt.reshape(b, nt, gd, tq))
    return _oproj_ln(o_t, w_out, x2, ln_g, ln_b, feature_major=True).reshape(b, s, d)


POOL_TM = 512
POOL_HALO = 16
POOL_LEAD = 8


def _pool_body(x_ref, halo_ref, w_ref, sc_ref, g_ref, b_ref, o_ref, ext_ref, tmp_ref, *, tm):
    t = pl.program_id(1)
    x = x_ref[0]
    lead = POOL_LEAD
    top = lead + POOL_HALO
    rows = top + tm
    ext_ref[0:lead, :] = jnp.zeros((lead, x.shape[1]), F32)
    tmp_ref[0:lead, :] = jnp.zeros((lead, x.shape[1]), F32)
    ext_ref[lead:top, :] = jnp.where(t > 0, halo_ref[0], 0.0)
    ext_ref[top:, :] = x
    row = t * tm + lax.broadcasted_iota(jnp.int32, (tm, 1), 0)
    ys = []
    for gi, w in enumerate(POOL_WINDOWS):
        cs = slice(gi * POOL_GROUP, (gi + 1) * POOL_GROUP)
        xg = x[:, cs]
        src, dst = ext_ref, tmp_ref
        shift = 1
        while shift < w:
            dst[lead:rows, cs] = src[lead:rows, cs] + src[lead - shift:rows - shift, cs]
            src, dst = dst, src
            shift *= 2
        acc = src[top:rows, cs]
        cnt = jnp.minimum(row + 1, w).astype(F32)
        diff = (acc / cnt - xg).astype(BF16)
        ys.append(_dot(diff, w_ref[gi]))
    y = jnp.concatenate(ys, axis=-1) * sc_ref[...]
    o_ref[0] = _layer_norm(DN_ALPHA * x + y, g_ref[...], b_ref[...])


def _pool_mixer_ln(x, w_grp, scale, ln_g, ln_b):
    b, s, d = x.shape
    tm = min(POOL_TM, s)
    hb = tm // POOL_HALO
    ng = len(POOL_WINDOWS)
    assert all(w & (w - 1) == 0 and w // 2 <= POOL_LEAD and w <= POOL_HALO for w in POOL_WINDOWS)
    return pl.pallas_call(
        functools.partial(_pool_body, tm=tm),
        grid=(b, s // tm),
        in_specs=[
            pl.BlockSpec((1, tm, d), lambda i, j: (i, j, 0)),
            pl.BlockSpec((1, POOL_HALO, d), lambda i, j: (i, jnp.maximum(j * hb - 1, 0), 0)),
            _full((ng, POOL_GROUP, POOL_GROUP)),
            _full((1, d)),
            _full((1, d)),
            _full((1, d)),
        ],
        out_specs=pl.BlockSpec((1, tm, d), lambda i, j: (i, j, 0)),
        out_shape=jax.ShapeDtypeStruct((b, s, d), F32),
        scratch_shapes=[pltpu.VMEM((POOL_LEAD + POOL_HALO + tm, d), F32)] * 2,
        compiler_params=_cparams("parallel", "parallel"),
        name="pool_ln",
    )(x, x, w_grp.astype(BF16), scale.reshape(1, d), ln_g.reshape(1, d), ln_b.reshape(1, d))


GLA_STEP_CHUNKS = 4


def _gla_tables(c):
    levels = []
    s = c
    while s >= 1:
        levels.append(s)
        s //= 2
    rows, masks = [], []
    idx = np.arange(c)
    for s in levels:
        dq = np.zeros((c, c), np.float32)
        dk = np.zeros((c, c), np.float32)
        for i in range(c):
            blk = i // s
            if s == c or blk % 2 == 1:
                dq[i, blk * s:i + 1] = 1.0
            dk[i, i + 1:blk * s + s] = 1.0
        rows.append(dq)
        if s > 1:
            rows.append(dk)
        if s < c:
            masks.append(((idx[:, None] // (2 * s) == idx[None, :] // (2 * s))
                          & ((idx[:, None] // s) % 2 == 1) & ((idx[None, :] // s) % 2 == 0)))
    masks.append(np.eye(c, dtype=bool))
    return np.concatenate(rows, 0), np.stack(masks).astype(np.float32), len(levels)


def _gla_body(q_ref, k_ref, v_ref, glr_ref, r_ref, x_ref, wg2_ref, bg_ref, ng_ref, dall_ref, mask_ref,
              wo_ref, lg_ref, lb_ref, o_ref, st_ref, *, nlev, nck):
    c = GLA_CHUNK
    heads = GLA_HEADS
    dk = GLA_DK // heads
    dv = GLA_DV // heads

    @pl.when(pl.program_id(1) == 0)
    def _():
        st_ref[...] = jnp.zeros_like(st_ref)

    z = _dot(glr_ref[0].astype(BF16), wg2_ref[...]) + bg_ref[...]
    log_a = (jnp.minimum(z, 0.0) - jnp.log1p(jnp.exp(-jnp.abs(z)))) * (1.0 / GLA_TAU)
    hi = log_a.astype(BF16)
    lo = (log_a - hi.astype(F32)).astype(BF16)
    parts = []
    for ck in range(nck):
        parts += [hi[ck * c:(ck + 1) * c], lo[ck * c:(ck + 1) * c]]
    e2 = _dot(dall_ref[...], jnp.concatenate(parts, axis=-1))
    q = q_ref[0]
    k = k_ref[0]
    pre = []
    for ck in range(nck):
        rows = slice(ck * c, (ck + 1) * c)
        decay = jnp.exp(e2[:, 2 * ck * GLA_DK:(2 * ck + 1) * GLA_DK]
                        + e2[:, (2 * ck + 1) * GLA_DK:(2 * ck + 2) * GLA_DK])
        for h in range(heads):
            ks = slice(h * dk, (h + 1) * dk)
            qh = q[rows, ks]
            kh = k[rows, ks]
            tbl = lambda i: decay[i * c:(i + 1) * c, ks]
            q_in = (qh * tbl(0)).astype(BF16)
            k_out = (kh * tbl(1)).astype(BF16)
            khb = kh.astype(BF16)
            att = mask_ref[nlev - 1] * _dot_nt(qh.astype(BF16), khb)
            for li in range(nlev - 1):
                ql = (qh * tbl(2 + 2 * li)).astype(BF16)
                kl = (kh * tbl(3 + 2 * li)).astype(BF16) if li < nlev - 2 else khb
                att = att + mask_ref[li] * _dot_nt(ql, kl)
            total = decay[c - 1:c, ks]
            pre.append((att.astype(BF16), q_in, k_out, total))
    states = [st_ref[h] for h in range(heads)]
    r = r_ref[0]
    gate = r * jax.nn.sigmoid(r)
    gated = []
    for ck in range(nck):
        rows = slice(ck * c, (ck + 1) * c)
        outs = []
        for h in range(heads):
            att, q_in, k_out, total = pre[ck * heads + h]
            vh = v_ref[0, rows, h * dv:(h + 1) * dv]
            out = _dot(att, vh) + _dot_nt(q_in, states[h].astype(BF16))
            states[h] = states[h] * total + _dot_tn(vh, k_out)
            outs.append(out * lax.rsqrt(jnp.mean(out * out, axis=-1, keepdims=True) + LN_EPS) * ng_ref[...])
        gated.append((jnp.concatenate(outs, axis=-1) * gate[rows]).astype(BF16))
    for h in range(heads):
        st_ref[h] = states[h]
    y = _dot(jnp.concatenate(gated, axis=0), wo_ref[...])
    o_ref[0] = _layer_norm(DN_ALPHA * x_ref[0] + y, lg_ref[...], lb_ref[...])


def _gla_mixer_ln(x, w_in, w_gate2, b_gate, norm_g, w_out, ln_g, ln_b):
    b, s, d = x.shape
    c = GLA_CHUNK
    heads = GLA_HEADS
    dk = GLA_DK // heads
    dv = GLA_DV // heads
    x2 = x.reshape(b * s, d)
    w = w_in.astype(BF16)
    o0, o1, o2, o3 = GLA_DK, 2 * GLA_DK, 2 * GLA_DK + GLA_DV, 2 * GLA_DK + GLA_DV + GLA_GATE_RANK
    q, k, v, glr, r = _proj(
        x2, [w[:, :o0], w[:, o0:o1], w[:, o1:o2], _pad_cols(w[:, o2:o3], LANES), w[:, o3:]],
        [F32, F32, BF16, F32, F32], scales=(dk ** -0.5, 1.0, 1.0, 1.0, 1.0))
    wg2 = jnp.pad(w_gate2.astype(BF16), ((0, LANES - GLA_GATE_RANK), (0, 0)))
    dall, masks, nlev = _gla_tables(c)
    nck = GLA_STEP_CHUNKS
    tok = lambda width: pl.BlockSpec((1, nck * c, width), lambda i, j: (i, j, 0))
    return pl.pallas_call(
        functools.partial(_gla_body, nlev=nlev, nck=nck),
        grid=(b, s // (nck * c)),
        in_specs=[
            tok(GLA_DK), tok(GLA_DK), tok(GLA_DV), tok(LANES), tok(GLA_DV), tok(d),
            _full((LANES, GLA_DK)),
            _full((1, GLA_DK)),
            _full((1, dv)),
            _full(dall.shape),
            _full(masks.shape),
            _full((GLA_DV, d)),
            _full((1, d)),
            _full((1, d)),
        ],
        out_specs=tok(d),
        out_shape=jax.ShapeDtypeStruct((b, s, d), F32),
        scratch_shapes=[pltpu.VMEM((heads, dv, dk), F32)],
        compiler_params=_cparams("parallel", "arbitrary"),
        name="gla_core",
    )(q.reshape(b, s, -1), k.reshape(b, s, -1), v.reshape(b, s, -1), glr.reshape(b, s, -1),
      r.reshape(b, s, -1), x, wg2, b_gate.reshape(1, -1), norm_g.reshape(1, -1),
      jnp.asarray(dall, BF16), jnp.asarray(masks, F32), w_out.astype(BF16), ln_g.reshape(1, d), ln_b.reshape(1, d))


def _gelu_tanh(x):
    return 0.5 * x * (1.0 + jnp.tanh(np.sqrt(2.0 / np.pi).astype(np.float32) * (x + 0.044715 * (x * x * x))))


def _cmp_body(kc_ref, vc_ref, plo_ref, phi_ref, w1a_ref, w1b_ref, w2_ref, o_ref):
    nch, width = o_ref.shape[1], o_ref.shape[2]
    st = kc_ref.shape[1] // nch
    a = jnp.zeros((nch, width), F32)
    bm = jnp.zeros((nch, width), F32)
    for l in range(st):
        c = jnp.concatenate([kc_ref[0, pl.ds(l, nch, stride=st), :], vc_ref[0, pl.ds(l, nch, stride=st), :]], axis=1)
        ws = slice(l * width, (l + 1) * width)
        a = a + _dot((c + plo_ref[:, ws]).astype(BF16), w1a_ref[ws, :])
        bm = bm + _dot((c + phi_ref[:, ws]).astype(BF16), w1b_ref[ws, :])
    nxt = jnp.concatenate([bm[1:], jnp.zeros_like(bm[:1])], axis=0)
    row = lax.broadcasted_iota(jnp.int32, a.shape, 0)
    pre = jnp.where(row < a.shape[0] - 1, a + nxt, 0.0)
    o_ref[0] = _dot(_gelu_tanh(pre).astype(BF16), w2_ref[...]).astype(o_ref.dtype)


def _nsa_compress(kvc, cmp_pos, cmp_w1, cmp_w2):
    b, s, _ = kvc.shape
    half = LANES
    st, g, d = NSA_CMP_STRIDE, NSA_KV_GROUPS, HEAD_DIM
    nch = s // st
    width = 2 * g * d
    eye = jnp.eye(2 * g, dtype=F32)
    sel = jnp.repeat(jnp.eye(2, dtype=F32), g, axis=1)

    def expand(w1_half):
        wj = jnp.einsum('jlde,jc->lcde', w1_half, sel)
        return jnp.einsum('lcde,cf->lcdfe', wj, eye).reshape(st * width, width).astype(BF16)

    w1a = expand(cmp_w1[:, :st])
    w1b = expand(cmp_w1[:, st:])
    w2 = jnp.einsum('cde,cf->cdfe', jnp.einsum('jde,jc->cde', cmp_w2, sel), eye).reshape(width, width).astype(BF16)

    def pos_row(p_half):
        return jnp.einsum('jld,jc->lcd', p_half, sel).reshape(1, st * width)

    return pl.pallas_call(
        _cmp_body,
        grid=(b,),
        in_specs=[
            pl.BlockSpec((1, s, half), lambda i: (i, 0, 0)),
            pl.BlockSpec((1, s, half), lambda i: (i, 0, 1)),
            _full((1, st * width)),
            _full((1, st * width)),
            _full((st * width, width)),
            _full((st * width, width)),
            _full((width, width)),
        ],
        out_specs=pl.BlockSpec((1, nch, width), lambda i: (i, 0, 0)),
        out_shape=jax.ShapeDtypeStruct((b, nch, width), BF16),
        compiler_params=_cparams("parallel"),
        name="nsa_compress",
    )(kvc, kvc, pos_row(cmp_pos[:, :st]), pos_row(cmp_pos[:, st:]), w1a, w1b, w2)


def _nsa_overlap_t(s):
    n_cmp_pad = s // NSA_CMP_STRIDE
    n_sel = s // NSA_SEL_LEN
    blk = np.arange(n_cmp_pad) * NSA_CMP_STRIDE
    sel_start = np.arange(n_sel) * NSA_SEL_LEN
    ov = (blk[None, :] < sel_start[:, None] + NSA_SEL_LEN) & (blk[None, :] + NSA_CMP_LEN > sel_start[:, None])
    return ov.astype(np.float32)


NSA_MASK = 1.0e30


def _nsa_body(qt_ref, kc_ref, vct_ref, ks_ref, kw_ref, vt_ref, glt_ref, ovt_ref, o_ref, qt_scr, sa_scr, sb_scr,
              m_scr, acc_scr, *, tq):
    d = HEAD_DIM
    groups = NSA_KV_GROUPS
    rep = N_HEADS // groups
    n_cmp = kc_ref.shape[2]
    n_sel = ovt_ref.shape[0]
    t = pl.program_id(1)
    q0 = t * tq
    win_tiles = NSA_WINDOW // tq
    gates = jax.nn.sigmoid(glt_ref[...])

    tpos = (q0 + lax.broadcasted_iota(jnp.int32, (1, tq), 1)).astype(F32)
    brow = lax.broadcasted_iota(jnp.int32, (ATT_BIAS_ROWS, tq), 0)
    ones_rows = jnp.where(brow == 0, 1.0, 0.0).astype(BF16)
    ci = lax.broadcasted_iota(jnp.int32, (tq, tq), 0)
    ai = lax.broadcasted_iota(jnp.int32, (tq, tq), 1)
    tile8 = lambda m: jnp.concatenate([m] * rep, axis=1)
    diag_add = tile8(jnp.where(ci <= ai, 0.0, NEG_INF))
    part_add = tile8(jnp.where((ci > ai) & (t >= win_tiles), 0.0, NEG_INF))
    ncm = lax.broadcasted_iota(jnp.int32, (n_cmp, tq), 0)
    tcm = q0 + lax.broadcasted_iota(jnp.int32, (n_cmp, tq), 1)
    valid_c = ncm * NSA_CMP_STRIDE + (NSA_CMP_LEN - 1) <= tcm
    jm = lax.broadcasted_iota(jnp.int32, (n_sel, tq), 0)
    cur = (q0 + lax.broadcasted_iota(jnp.int32, (n_sel, tq), 1)) // NSA_SEL_LEN
    causal_sel = jm <= cur
    forced = (jm == 0) | (jm == cur) | (jm == cur - 1)
    pad_rows = jnp.zeros((ATT_QROWS - d - ATT_BIAS_ROWS - n_sel, tq), BF16)

    o_cmp = []
    for g in range(groups):
        heads = [(qt_ref[h * d:(h + 1) * d, :], _alibi_rows(h, tpos, brow)) for h in range(g * rep, (g + 1) * rep)]

        def build(mask_rows):
            cols = [jnp.concatenate([qh, bias, mask_rows, pad_rows], axis=0) for qh, bias in heads]
            return jnp.concatenate(cols, axis=1)

        qt_scr[2 * g] = build(jnp.zeros((n_sel, tq), BF16))

        s_c = _dot(kc_ref[0, g], qt_scr[2 * g])
        p_sum = jnp.zeros((n_cmp, tq), F32)
        ps = []
        for r in range(rep):
            sr = jnp.where(valid_c, s_c[:, r * tq:(r + 1) * tq], NEG_INF)
            m = jnp.max(sr, axis=0, keepdims=True)
            e = jnp.where(valid_c, jnp.exp2(sr - m), 0.0)
            den = jnp.sum(e, axis=0, keepdims=True)
            p = e * jnp.where(den > 0.0, 1.0 / den, 0.0)
            p_sum = p_sum + p
            ps.append(p.astype(BF16))
        o_cmp.append(_dot(vct_ref[0, g * d:(g + 1) * d, :], jnp.concatenate(ps, axis=1)))

        p_hi = p_sum.astype(BF16)
        p_lo = (p_sum - p_hi.astype(F32)).astype(BF16)
        ovt = ovt_ref[...]
        imp_t = _dot(ovt, p_hi) + _dot(ovt, p_lo)
        score = jnp.where(causal_sel, imp_t + jnp.where(forced, NSA_FORCE_BONUS, 0.0), NEG_INF)
        rank = jnp.zeros((n_sel, tq), F32)
        for mp in range(n_sel):
            row = score[mp:mp + 1, :]
            ahead = (row > score) | ((row == score) & (mp < jm))
            rank = rank + jnp.where(ahead, 1.0, 0.0)
        selected = (rank < NSA_TOPK) & causal_sel
        qt_scr[2 * g + 1] = build(jnp.where(selected, 0.0, -NSA_MASK).astype(BF16))

    def score(chains, kt, slot):
        k0 = pl.multiple_of(kt * tq, tq)
        for ci, (k_ref, g, _, qi, _) in enumerate(chains):
            slot[ci] = _dot(k_ref[0, g, pl.ds(k0, tq), :], qt_scr[qi])

    def absorb(chains, kt, add, slot):
        for ci, (_, _, v_row0, _, si) in enumerate(chains):
            s = slot[ci]
            if add is not None:
                s = s + add
            m_prev = m_scr[si]
            m_new = jnp.maximum(m_prev, jnp.max(s, axis=0, keepdims=True))
            e = jnp.exp2(s - m_new)
            alpha = jnp.exp2(m_prev - m_new)
            vals = jnp.concatenate([vt_ref[0, kt, v_row0:v_row0 + d, :], ones_rows], axis=0)
            acc_scr[si] = alpha * acc_scr[si] + _dot(vals, e.astype(BF16))
            m_scr[si] = m_new

    def sweep(chains, lo):
        n = t - lo

        def pair(j, carry):
            a = lo + 2 * j
            score(chains, a + 1, sb_scr)
            absorb(chains, a, None, sa_scr)
            score(chains, a + 2, sa_scr)
            absorb(chains, a + 1, None, sb_scr)
            return carry

        lax.fori_loop(0, n // 2, pair, 0)

        @pl.when(n % 2 == 1)
        def _():
            score(chains, t, sb_scr)
            absorb(chains, t - 1, None, sa_scr)
            absorb(chains, t, diag_add, sb_scr)

        @pl.when(n % 2 == 0)
        def _():
            absorb(chains, t, diag_add, sa_scr)

    m_scr[...] = jnp.full(m_scr.shape, NEG_INF, F32)
    acc_scr[...] = jnp.zeros(acc_scr.shape, F32)

    sel_chains = [(ks_ref, g, g * d, 2 * g + 1, g) for g in range(groups)]
    score(sel_chains, 0, sa_scr)
    sweep(sel_chains, 0)

    win_chains = [(kw_ref, g, (groups + g) * d, 2 * g, groups + g) for g in range(groups)]
    oldest = jnp.maximum(t - win_tiles, 0)
    lo = jnp.maximum(t - win_tiles + 1, 0)
    score(win_chains, oldest, sb_scr)
    score(win_chains, jnp.minimum(lo, t), sa_scr)
    absorb(win_chains, oldest, part_add, sb_scr)
    sweep(win_chains, lo)

    o_sel = [acc_scr[g, :d] / acc_scr[g, d:d + 1] for g in range(groups)]
    o_win = [acc_scr[groups + g, :d] / acc_scr[groups + g, d:d + 1] for g in range(groups)]

    for g in range(groups):
        for r in range(rep):
            h = g * rep + r
            cs = slice(r * tq, (r + 1) * tq)
            o = (gates[h:h + 1] * o_cmp[g][:, cs] + gates[N_HEADS + h:N_HEADS + h + 1] * o_sel[g][:, cs]
                 + gates[2 * N_HEADS + h:2 * N_HEADS + h + 1] * o_win[g][:, cs])
            o_ref[h * d:(h + 1) * d, :] = o.astype(o_ref.dtype)


def _nsa_mixer_ln(x, w_in, cmp_pos, cmp_w1, cmp_w2, w_out, ln_g, ln_b):
    b, s, d = x.shape
    hd = N_HEADS * HEAD_DIM
    groups = NSA_KV_GROUPS
    gd = groups * HEAD_DIM
    n = b * s
    tq = ATT_TQ
    n_sel = s // NSA_SEL_LEN
    n_cmp = s // NSA_CMP_STRIDE
    assert HEAD_DIM + ATT_BIAS_ROWS + n_sel <= ATT_QROWS and n_sel % 16 == 0 and NSA_WINDOW % tq == 0
    x2 = x.reshape(n, d)
    w = w_in.astype(BF16)
    c0 = hd + 2 * gd
    w_k2 = jnp.concatenate([w[:, c0:c0 + gd], w[:, c0 + 2 * gd:c0 + 3 * gd]], axis=1)
    w_v2 = jnp.concatenate([w[:, c0 + gd:c0 + 2 * gd], w[:, c0 + 3 * gd:c0 + 4 * gd]], axis=1)
    tok = np.arange(s, dtype=np.float64)
    qt, kvc, k_sw, vt, glt = _proj_mixed(
        x2, [(w[:, :hd], "feat", BF16, HEAD_DIM ** -0.5 * LOG2E),
             (w[:, hd:c0], "tok", F32, 1.0),
             (w_k2, "keys", BF16, 1.0, _key_tail(tok, n_sel, NSA_SEL_LEN)),
             (w_v2, "feat_tiles", BF16, 1.0),
             (_pad_cols(w[:, c0 + 4 * gd:], LANES), "feat", F32, 1.0)], seq=s)

    cmp = _nsa_compress(kvc.reshape(b, s, 2 * gd), cmp_pos, cmp_w1, cmp_w2)
    cpos = np.arange(n_cmp, dtype=np.float64) * NSA_CMP_STRIDE + (NSA_CMP_LEN - 1) / 2
    k_cmp = _keys_with_tail(cmp[:, :, :gd], groups, _key_tail(cpos))
    v_cmp_t = cmp[:, :, gd:].transpose(0, 2, 1)
    vt = vt.reshape(b, s // tq, 2 * gd, tq)
    ovt = _nsa_overlap_t(s)
    nt = s // tq
    o_t = pl.pallas_call(
        functools.partial(_nsa_body, tq=tq),
        grid=(b, nt),
        in_specs=[
            pl.BlockSpec((hd, tq), lambda i, j: (0, i * nt + j)),
            pl.BlockSpec((1, groups, n_cmp, ATT_QROWS), lambda i, j: (i, 0, 0, 0)),
            pl.BlockSpec((1, gd, n_cmp), lambda i, j: (i, 0, 0)),
            pl.BlockSpec((1, groups, s, ATT_QROWS), lambda i, j: (i, 0, 0, 0)),
            pl.BlockSpec((1, groups, s, ATT_QROWS), lambda i, j: (i, 1, 0, 0)),
            pl.BlockSpec((1, nt, 2 * gd, tq), lambda i, j: (i, 0, 0, 0)),
            pl.BlockSpec((LANES, tq), lambda i, j: (0, i * nt + j)),
            _full(ovt.shape),
        ],
        out_specs=pl.BlockSpec((hd, tq), lambda i, j: (0, i * nt + j)),
        out_shape=jax.ShapeDtypeStruct((hd, n), BF16),
        scratch_shapes=[pltpu.VMEM((2 * groups, ATT_QROWS, (N_HEADS // groups) * tq), BF16),
                        pltpu.VMEM((groups, tq, (N_HEADS // groups) * tq), F32),
                        pltpu.VMEM((groups, tq, (N_HEADS // groups) * tq), F32),
                        pltpu.VMEM((2 * groups, 1, (N_HEADS // groups) * tq), F32),
                        pltpu.VMEM((2 * groups, HEAD_DIM + ATT_BIAS_ROWS, (N_HEADS // groups) * tq), F32)],
        compiler_params=_cparams("parallel", "parallel"),
        name="nsa_core",
    )(qt, k_cmp, v_cmp_t, k_sw, k_sw, vt, glt, jnp.asarray(ovt, BF16))
    return _oproj_ln(o_t, w_out, x2, ln_g, ln_b, feature_major=True).reshape(b, s, d)


def kernel(x, mem, ln_g, ln_b, ffn1_w_gate, ffn1_w_up, ffn1_w_down, ffn2_w_gate, ffn2_w_up, ffn2_w_down,
           xattn_w_q, xattn_w_kv, xattn_w_o, swa_w_in, swa_sinks, swa_w_out, nsa_w_in, nsa_cmp_pos,
           nsa_cmp_w1, nsa_cmp_w2, nsa_w_out, gla_w_in, gla_w_gate2, gla_b_gate, gla_norm_g, gla_w_out,
           pool_w, pool_scale):
    b, s, d = x.shape
    ffn1 = [w.astype(BF16) for w in (ffn1_w_gate, ffn1_w_up, ffn1_w_down)]
    ffn2 = [w.astype(BF16) for w in (ffn2_w_gate, ffn2_w_up, ffn2_w_down)]
    xq, xkv, xo = (w.astype(BF16) for w in (xattn_w_q, xattn_w_kv, xattn_w_o))
    for i in range(DEPTH):
        kind = i % N_MIXERS
        j = i // N_MIXERS
        x = _ffn_ln(x.reshape(b * s, d), *ffn1, i, ln_g[i, 0], ln_b[i, 0]).reshape(b, s, d)
        if kind == 0:
            x = _swa_mixer_ln(x, swa_w_in[j], swa_sinks[j], swa_w_out[j], ln_g[i, 1], ln_b[i, 1])
        elif kind == 1:
            x = _nsa_mixer_ln(x, nsa_w_in[j], nsa_cmp_pos[j], nsa_cmp_w1[j], nsa_cmp_w2[j], nsa_w_out[j],
                              ln_g[i, 1], ln_b[i, 1])
        elif kind == 2:
            x = _gla_mixer_ln(x, gla_w_in[j], gla_w_gate2[j], gla_b_gate[j], gla_norm_g[j], gla_w_out[j],
                              ln_g[i, 1], ln_b[i, 1])
        else:
            x = _pool_mixer_ln(x, pool_w[j], pool_scale[j], ln_g[i, 1], ln_b[i, 1])
        x = _xattn_ln(x, mem, xq, xkv[i], xo, i, ln_g[i, 2], ln_b[i, 2])
        x = _ffn_ln(x.reshape(b * s, d), *ffn2, i, ln_g[i, 3], ln_b[i, 3]).reshape(b, s, d)
    return x
```

```python
import functools

import numpy as np
import jax
import jax.numpy as jnp
from jax import lax
from jax.experimental import pallas as pl
from jax.experimental.pallas import tpu as pltpu

F32 = jnp.float32
BF16 = jnp.bfloat16

D_MODEL = 1024
DEPTH = 4
N_MIXERS = 4
HEAD_DIM = 64
N_HEADS = D_MODEL // HEAD_DIM
SWA_KV_HEADS = 4
SWA_WINDOW = 128
NSA_KV_GROUPS = 2
NSA_CMP_LEN = 32
NSA_CMP_STRIDE = 16
NSA_SEL_LEN = 64
NSA_TOPK = 8
NSA_WINDOW = 512
NSA_FORCE_BONUS = 1.0e4
GLA_HEADS = 4
GLA_DK = D_MODEL // 2
GLA_DV = D_MODEL
GLA_GATE_RANK = 16
GLA_TAU = 16.0
GLA_CHUNK = 64
POOL_WINDOWS = (2, 4, 8, 16)
POOL_GROUP = D_MODEL // 4
XATTN_HEADS = 4
DN_ALPHA = (2 * DEPTH) ** 0.25
LN_EPS = 1e-5
NEG_INF = -1e30
LOG2E = float(np.log2(np.e))

LANES = 128
V7X_VMEM_LIMIT_BYTES = 56 * 1024 * 1024

_NT = (((1,), (1,)), ((), ()))
_TN = (((0,), (0,)), ((), ()))


def _cparams(*sem, flags=None):
    return pltpu.CompilerParams(dimension_semantics=sem, vmem_limit_bytes=V7X_VMEM_LIMIT_BYTES, flags=flags)


def _dot(a, b):
    return jnp.dot(a, b, preferred_element_type=F32)


def _dot_nt(a, b):
    return lax.dot_general(a, b, _NT, preferred_element_type=F32)


def _dot_tn(a, b):
    return lax.dot_general(a, b, _TN, preferred_element_type=F32)


def _layer_norm(y, g, b):
    mu = jnp.mean(y, axis=-1, keepdims=True)
    yc = y - mu
    var = jnp.mean(yc * yc, axis=-1, keepdims=True)
    return yc * lax.rsqrt(var + LN_EPS) * g + b


def _alibi_slope(h, n):
    return float(2.0 ** (-8.0 * (h + 1) / n))


def _full(shape):
    nd = len(shape)
    return pl.BlockSpec(shape, lambda *_: (0,) * nd, pipeline_mode=pl.Buffered(1))


FFN_TM = 1024
FFN_SUB = 512
FFN_TF = 256


def _ffn_body(x_ref, wg_ref, wu_ref, wd_ref, g_ref, b_ref, o_ref, acc_ref, *, tf, sub):
    nf = wd_ref.shape[0] // tf
    for si in range(x_ref.shape[0] // sub):
        rows = slice(si * sub, (si + 1) * sub)
        x = x_ref[rows, :]
        xb = x.astype(BF16)
        for c in range(nf):
            cs = slice(c * tf, (c + 1) * tf)
            gate = _dot(xb, wg_ref[:, cs])
            up = _dot(xb, wu_ref[:, cs])
            h = (gate * jax.nn.sigmoid(gate) * up).astype(BF16)
            d = _dot(h, wd_ref[cs, :])
            if c == 0:
                acc_ref[si] = d
            else:
                acc_ref[si] += d
        y = DN_ALPHA * x + 0.5 * acc_ref[si]
        o_ref[rows, :] = _layer_norm(y, g_ref[...], b_ref[...])


def _layer_of(stacked, layer):
    _, r, c = stacked.shape
    return pl.BlockSpec((None, r, c), lambda *_: (layer, 0, 0), pipeline_mode=pl.Buffered(1))


def _ffn_ln(x2, w_gate, w_up, w_down, layer, ln_g, ln_b):
    n, d = x2.shape
    tm = min(FFN_TM, n)
    sub = min(FFN_SUB, tm)
    return pl.pallas_call(
        functools.partial(_ffn_body, tf=FFN_TF, sub=sub),
        grid=(n // tm,),
        in_specs=[
            pl.BlockSpec((tm, d), lambda i: (i, 0)),
            _layer_of(w_gate, layer),
            _layer_of(w_up, layer),
            _layer_of(w_down, layer),
            _full((1, d)),
            _full((1, d)),
        ],
        out_specs=pl.BlockSpec((tm, d), lambda i: (i, 0)),
        out_shape=jax.ShapeDtypeStruct((n, d), F32),
        scratch_shapes=[pltpu.VMEM((tm // sub, sub, d), F32)],
        compiler_params=_cparams("parallel"),
        name="ffn_ln",
    )(x2, w_gate, w_up, w_down, ln_g.reshape(1, d), ln_b.reshape(1, d))


PROJ_TM = 1024


def _proj_body(x_ref, *refs, scales):
    n = len(scales)
    xb = x_ref[...].astype(BF16)
    for w_ref, o_ref, s in zip(refs[:n], refs[n:], scales):
        r = _dot(xb, w_ref[...])
        if s != 1.0:
            r = r * s
        o_ref[...] = r.astype(o_ref.dtype)


def _proj(x2, weights, dtypes, scales=None):
    n, k = x2.shape
    scales = tuple(scales) if scales is not None else (1.0,) * len(weights)
    tm = min(PROJ_TM, n)
    in_specs = [pl.BlockSpec((tm, k), lambda i: (i, 0))] + [_full(w.shape) for w in weights]
    out_specs = [pl.BlockSpec((tm, w.shape[1]), lambda i: (i, 0)) for w in weights]
    out_shape = [jax.ShapeDtypeStruct((n, w.shape[1]), dt) for w, dt in zip(weights, dtypes)]
    return pl.pallas_call(
        functools.partial(_proj_body, scales=scales),
        grid=(n // tm,),
        in_specs=in_specs,
        out_specs=out_specs,
        out_shape=out_shape,
        compiler_params=_cparams("parallel"),
        name="proj",
    )(x2, *weights)


def _pad_cols(w, m):
    return jnp.pad(w, ((0, 0), (0, m - w.shape[1])))


OPROJ_TM = 1024


def _oproj_body(o_ref, w_ref, x_ref, g_ref, b_ref, out_ref, *, feature_major):
    y = _dot_tn(o_ref[...], w_ref[...]) if feature_major else _dot(o_ref[...], w_ref[...])
    out_ref[...] = _layer_norm(DN_ALPHA * x_ref[...] + y, g_ref[...], b_ref[...])


def _oproj_ln(o2, w_out, x2, ln_g, ln_b, feature_major=False):
    n, d = x2.shape
    k = w_out.shape[0]
    tm = min(OPROJ_TM, n)
    return pl.pallas_call(
        functools.partial(_oproj_body, feature_major=feature_major),
        grid=(n // tm,),
        in_specs=[
            pl.BlockSpec((k, tm), lambda i: (0, i)) if feature_major else pl.BlockSpec((tm, k), lambda i: (i, 0)),
            _full((k, d)),
            pl.BlockSpec((tm, d), lambda i: (i, 0)),
            _full((1, d)),
            _full((1, d)),
        ],
        out_specs=pl.BlockSpec((tm, d), lambda i: (i, 0)),
        out_shape=jax.ShapeDtypeStruct((n, d), F32),
        compiler_params=_cparams("parallel"),
        name="oproj_ln",
    )(o2, w_out.astype(BF16), x2, ln_g.reshape(1, d), ln_b.reshape(1, d))


XATTN_TQ = 1024
XATTN_SUB = 512


def _xattn_body(x_ref, kv_ref, wq_ref, wo_ref, g_ref, b_ref, o_ref, *, heads, sub):
    d = x_ref.shape[-1]
    dh = d // heads
    chains = [slice(si * sub, (si + 1) * sub) for si in range(x_ref.shape[1] // sub)]
    xs = [x_ref[0, rows, :] for rows in chains]
    qs = [(_dot(x.astype(BF16), wq_ref[...]) * dh ** -0.5).astype(BF16) for x in xs]
    scores = [[_dot_nt(q[:, h * dh:(h + 1) * dh], kv_ref[0, :, h * dh:(h + 1) * dh]) for h in range(heads)]
              for q in qs]
    outs = [[] for _ in chains]
    for h in range(heads):
        vh = kv_ref[0, :, d + h * dh:d + (h + 1) * dh]
        for ci in range(len(chains)):
            s = scores[ci][h]
            m = jnp.max(s, axis=-1, keepdims=True)
            e = jnp.exp(s - m)
            p = e / jnp.sum(e, axis=-1, keepdims=True)
            outs[ci].append(_dot(p.astype(BF16), vh).astype(BF16))
    ys = [_dot(jnp.concatenate(o, axis=-1), wo_ref[...]) for o in outs]
    for rows, x, y in zip(chains, xs, ys):
        o_ref[0, rows, :] = _layer_norm(DN_ALPHA * x + y, g_ref[...], b_ref[...])


def _xattn_ln(x, mem, w_q, w_kv, w_o, layer, ln_g, ln_b):
    b, s, d = x.shape
    m = mem.shape[1]
    (kv,) = _proj(mem.reshape(b * m, d), [w_kv], [BF16])
    kv = kv.reshape(b, m, 2 * d)
    tq = min(XATTN_TQ, s)
    return pl.pallas_call(
        functools.partial(_xattn_body, heads=XATTN_HEADS, sub=min(XATTN_SUB, tq)),
        grid=(b, s // tq),
        in_specs=[
            pl.BlockSpec((1, tq, d), lambda i, j: (i, j, 0)),
            pl.BlockSpec((1, m, 2 * d), lambda i, j: (i, 0, 0)),
            _layer_of(w_q, layer),
            _layer_of(w_o, layer),
            _full((1, d)),
            _full((1, d)),
        ],
        out_specs=pl.BlockSpec((1, tq, d), lambda i, j: (i, j, 0)),
        out_shape=jax.ShapeDtypeStruct((b, s, d), F32),
        compiler_params=_cparams("parallel", "parallel"),
        name="xattn_ln",
    )(x, kv, w_q, w_o, ln_g.reshape(1, d), ln_b.reshape(1, d))


ATT_TQ = 128
ATT_QROWS = 128
ATT_BIAS_ROWS = 16


def _split3_bf16(x):
    out = []
    r = np.float32(x)
    for _ in range(3):
        p = np.float32(np.asarray(r, np.float32).astype(BF16).astype(np.float32))
        out.append(float(p))
        r = np.float32(r - p)
    return out


def _alibi_rows(h, tpos, brow):
    slope = _alibi_slope(h, N_HEADS) * LOG2E
    pieces = _split3_bf16(slope * LANES) + _split3_bf16(slope)
    v = -np.float32(slope) * tpos
    v_hi = v.astype(BF16).astype(F32)
    v_mid = (v - v_hi).astype(BF16).astype(F32)
    v_lo = v - v_hi - v_mid
    bias = jnp.zeros(brow.shape, F32)
    for k, val in enumerate(pieces + [v_hi, v_mid, v_lo]):
        bias = jnp.where(brow == k, val, bias)
    return bias.astype(BF16)


def _key_tail(pos, n_blocks=0, block_len=1):
    n = pos.shape[0]
    hi = np.floor(pos / LANES)
    lo = pos - hi * LANES
    tail = np.zeros((n, ATT_QROWS - HEAD_DIM), np.float32)
    tail[:, 0:3] = hi[:, None]
    tail[:, 3:6] = lo[:, None]
    tail[:, 6:9] = 1.0
    if n_blocks:
        blk = (pos // block_len).astype(np.int64)
        tail[np.arange(n), ATT_BIAS_ROWS + blk] = 1.0
    return tail


def _keys_with_tail(k_tok, groups, tail):
    b, rows, _ = k_tok.shape
    k = k_tok.reshape(b, rows, groups, HEAD_DIM).transpose(0, 2, 1, 3)
    t_b = jnp.broadcast_to(jnp.asarray(tail, BF16), (b, groups, rows, tail.shape[1]))
    return jnp.concatenate([k, t_b], axis=-1)


def _proj_mixed_body(x_ref, *refs, kinds, scales, tq):
    n = len(kinds)
    xb = x_ref[...].astype(BF16)
    tails = iter(refs[n:len(refs) - n])
    for w_ref, o_ref, kind, s in zip(refs[:n], refs[len(refs) - n:], kinds, scales):
        r = _dot(xb, w_ref[...]) if kind in ("tok", "keys") else _dot_nt(w_ref[...], xb)
        if s != 1.0:
            r = r * s
        r = r.astype(o_ref.dtype)
        if kind == "feat_tiles":
            for c in range(o_ref.shape[0]):
                o_ref[c] = r[:, c * tq:(c + 1) * tq]
        elif kind == "keys":
            tail = next(tails)[...]
            for g in range(o_ref.shape[1]):
                o_ref[0, g] = jnp.concatenate([r[:, g * HEAD_DIM:(g + 1) * HEAD_DIM], tail], axis=1)
        else:
            o_ref[...] = r


def _proj_mixed(x2, specs, seq, tq=ATT_TQ):
    n, k = x2.shape
    tm = min(PROJ_TM, seq)
    per_seq = seq // tm
    kinds = tuple(s[1] for s in specs)
    ws, tails, tail_specs, out_specs, out_shape = [], [], [], [], []
    for spec in specs:
        w, kind, dt = spec[:3]
        m = w.shape[1]
        if kind == "tok":
            ws.append(w)
            out_specs.append(pl.BlockSpec((tm, m), lambda i: (i, 0)))
            out_shape.append(jax.ShapeDtypeStruct((n, m), dt))
        elif kind == "feat":
            ws.append(w.T)
            out_specs.append(pl.BlockSpec((m, tm), lambda i: (0, i)))
            out_shape.append(jax.ShapeDtypeStruct((m, n), dt))
        elif kind == "feat_tiles":
            ws.append(w.T)
            out_specs.append(pl.BlockSpec((tm // tq, m, tq), lambda i: (i, 0, 0)))
            out_shape.append(jax.ShapeDtypeStruct((n // tq, m, tq), dt))
        else:
            ws.append(w)
            groups = m // HEAD_DIM
            tails.append(jnp.asarray(spec[4], dt))
            tail_specs.append(pl.BlockSpec((tm, ATT_QROWS - HEAD_DIM), lambda i: (i % per_seq, 0)))
            out_specs.append(pl.BlockSpec((1, groups, tm, ATT_QROWS), lambda i: (i // per_seq, 0, i % per_seq, 0)))
            out_shape.append(jax.ShapeDtypeStruct((n // seq, groups, seq, ATT_QROWS), dt))
    return pl.pallas_call(
        functools.partial(_proj_mixed_body, kinds=kinds, scales=tuple(s[3] for s in specs), tq=tq),
        grid=(n // tm,),
        in_specs=[pl.BlockSpec((tm, k), lambda i: (i, 0))] + [_full(w.shape) for w in ws] + tail_specs,
        out_specs=out_specs,
        out_shape=out_shape,
        compiler_params=_cparams("parallel"),
        name="proj_mixed",
    )(x2, *ws, *tails)


SWA_CHAIN_HEADS = 4
SWA_LOOKAHEAD = 2


def _swa_body(sink_ref, qt_ref, k_ref, vt_ref, o_ref, *, tq):
    d = HEAD_DIM
    groups = SWA_KV_HEADS
    rep = N_HEADS // groups
    t = pl.program_id(1)
    first = t == 0
    base = jnp.maximum(t - 1, 0)
    kb = pl.multiple_of(base * tq, tq)
    tpos = (t * tq + lax.broadcasted_iota(jnp.int32, (1, tq), 1)).astype(F32)
    brow = lax.broadcasted_iota(jnp.int32, (ATT_BIAS_ROWS, tq), 0)
    ones_rows = jnp.where(lax.broadcasted_iota(jnp.int32, (ATT_BIAS_ROWS, 2 * tq), 0) == 0, 1.0, 0.0).astype(BF16)
    ci = lax.broadcasted_iota(jnp.int32, (tq, tq), 0)
    ai = lax.broadcasted_iota(jnp.int32, (tq, tq), 1)
    hb = SWA_CHAIN_HEADS
    older = jnp.where(ci > ai, 0.0, NEG_INF)
    own = jnp.where(ci <= ai, 0.0, NEG_INF)
    add = jnp.concatenate([jnp.where(first, own, older), jnp.where(first, NEG_INF, own)], axis=0)
    add = jnp.concatenate([add] * hb, axis=1)
    pad_rows = jnp.zeros((ATT_QROWS - d - ATT_BIAS_ROWS, tq), BF16)

    def issue(c):
        g = (c * hb) // rep
        qt = jnp.concatenate(
            [jnp.concatenate([qt_ref[h * d:(h + 1) * d, :], _alibi_rows(h, tpos, brow), pad_rows], axis=0)
             for h in range(c * hb, (c + 1) * hb)], axis=1)
        return _dot(k_ref[0, g, pl.ds(kb, 2 * tq), :], qt)

    def finish(c, s):
        g = (c * hb) // rep
        sink = jnp.concatenate([jnp.full((1, tq), sink_ref[h] * LOG2E, F32) for h in range(c * hb, (c + 1) * hb)], axis=1)
        m = jnp.maximum(jnp.max(s, axis=0, keepdims=True), sink)
        e = jnp.exp2(s - m)
        vals = jnp.concatenate([vt_ref[0, base, g * d:(g + 1) * d, :], vt_ref[0, base + 1, g * d:(g + 1) * d, :]], axis=1)
        acc = _dot(jnp.concatenate([vals, ones_rows], axis=0), e.astype(BF16))
        o = acc[:d] / (acc[d:d + 1] + jnp.exp2(sink - m))
        for r in range(hb):
            h = c * hb + r
            o_ref[h * d:(h + 1) * d, :] = o[:, r * tq:(r + 1) * tq].astype(o_ref.dtype)

    n_chains = N_HEADS // hb
    pending = [issue(c) for c in range(min(SWA_LOOKAHEAD, n_chains))]
    s_cur = pending[0] + add
    for c in range(n_chains):
        s_next = pending[c + 1] + add if c + 1 < n_chains else None
        finish(c, s_cur)
        if c + SWA_LOOKAHEAD < n_chains:
            pending.append(issue(c + SWA_LOOKAHEAD))
        s_cur = s_next


def _swa_mixer_ln(x, w_in, sinks, w_out, ln_g, ln_b):
    b, s, d = x.shape
    hd = N_HEADS * HEAD_DIM
    groups = SWA_KV_HEADS
    gd = groups * HEAD_DIM
    n = b * s
    tq = ATT_TQ
    nt = s // tq
    assert SWA_WINDOW == tq and nt >= 2
    x2 = x.reshape(n, d)
    w = w_in.astype(BF16)
    tail = _key_tail(np.arange(s, dtype=np.float64))
    qt, kk, vt = _proj_mixed(x2, [(w[:, :hd], "feat", BF16, HEAD_DIM ** -0.5 * LOG2E),
                                  (w[:, hd:hd + gd], "keys", BF16, 1.0, tail),
                                  (w[:, hd + gd:], "feat_tiles", BF16, 1.0)], seq=s)
    o_t = pl.pallas_call(
        functools.partial(_swa_body, tq=tq),
        grid=(b, nt),
        in_specs=[
            pl.BlockSpec(memory_space=pltpu.SMEM),
            pl.BlockSpec((hd, tq), lambda i, j: (0, i * nt + j)),
            pl.BlockSpec((1, groups, s, ATT_QROWS), lambda i, j: (i, 0, 0, 0)),
            pl.BlockSpec((1, nt, gd, tq), lambda i, j: (i, 0, 0, 0)),
        ],
        out_specs=pl.BlockSpec((hd, tq), lambda i, j: (0, i * nt + j)),
        out_shape=jax.ShapeDtypeStruct((hd, n), BF16),
        compiler_params=_cparams("parallel", "parallel"),
        name="swa_core",
    )(sinks.astype(F32), qt, kk, vt.reshape(b, nt, gd, tq))
    return _oproj_ln(o_t, w_out, x2, ln_g, ln_b, feature_major=True).reshape(b, s, d)


POOL_TM = 1024
POOL_HALO = 16
POOL_LEAD = 8


def _pool_body(x_ref, halo_ref, w_ref, sc_ref, g_ref, b_ref, o_ref, ext_ref, tmp_ref, *, tm):
    t = pl.program_id(1)
    x = x_ref[0]
    lead = POOL_LEAD
    top = lead + POOL_HALO
    rows = top + tm
    ext_ref[0:lead, :] = jnp.zeros((lead, x.shape[1]), F32)
    tmp_ref[0:lead, :] = jnp.zeros((lead, x.shape[1]), F32)
    ext_ref[lead:top, :] = jnp.where(t > 0, halo_ref[0], 0.0)
    ext_ref[top:, :] = x
    row = t * tm + lax.broadcasted_iota(jnp.int32, (tm, 1), 0)
    ys = []
    for gi, w in enumerate(POOL_WINDOWS):
        cs = slice(gi * POOL_GROUP, (gi + 1) * POOL_GROUP)
        xg = x[:, cs]
        src, dst = ext_ref, tmp_ref
        shift = 1
        while shift < w:
            dst[lead:rows, cs] = src[lead:rows, cs] + src[lead - shift:rows - shift, cs]
            src, dst = dst, src
            shift *= 2
        acc = src[top:rows, cs]
        cnt = jnp.minimum(row + 1, w).astype(F32)
        diff = (acc / cnt - xg).astype(BF16)
        ys.append(_dot(diff, w_ref[gi]))
    y = jnp.concatenate(ys, axis=-1) * sc_ref[...]
    o_ref[0] = _layer_norm(DN_ALPHA * x + y, g_ref[...], b_ref[...])


def _pool_mixer_ln(x, w_grp, scale, ln_g, ln_b):
    b, s, d = x.shape
    tm = min(POOL_TM, s)
    hb = tm // POOL_HALO
    ng = len(POOL_WINDOWS)
    assert all(w & (w - 1) == 0 and w // 2 <= POOL_LEAD and w <= POOL_HALO for w in POOL_WINDOWS)
    return pl.pallas_call(
        functools.partial(_pool_body, tm=tm),
        grid=(b, s // tm),
        in_specs=[
            pl.BlockSpec((1, tm, d), lambda i, j: (i, j, 0)),
            pl.BlockSpec((1, POOL_HALO, d), lambda i, j: (i, jnp.maximum(j * hb - 1, 0), 0)),
            _full((ng, POOL_GROUP, POOL_GROUP)),
            _full((1, d)),
            _full((1, d)),
            _full((1, d)),
        ],
        out_specs=pl.BlockSpec((1, tm, d), lambda i, j: (i, j, 0)),
        out_shape=jax.ShapeDtypeStruct((b, s, d), F32),
        scratch_shapes=[pltpu.VMEM((POOL_LEAD + POOL_HALO + tm, d), F32)] * 2,
        compiler_params=_cparams("parallel", "parallel"),
        name="pool_ln",
    )(x, x, w_grp.astype(BF16), scale.reshape(1, d), ln_g.reshape(1, d), ln_b.reshape(1, d))


GLA_STEP_CHUNKS = 4


def _gla_tables(c):
    levels = []
    s = c
    while s >= 1:
        levels.append(s)
        s //= 2
    rows, masks = [], []
    idx = np.arange(c)
    for s in levels:
        dq = np.zeros((c, c), np.float32)
        dk = np.zeros((c, c), np.float32)
        for i in range(c):
            blk = i // s
            if s == c or blk % 2 == 1:
                dq[i, blk * s:i + 1] = 1.0
            dk[i, i + 1:blk * s + s] = 1.0
        rows.append(dq)
        if s > 1:
            rows.append(dk)
        if s < c:
            masks.append(((idx[:, None] // (2 * s) == idx[None, :] // (2 * s))
                          & ((idx[:, None] // s) % 2 == 1) & ((idx[None, :] // s) % 2 == 0)))
    masks.append(np.eye(c, dtype=bool))
    return np.concatenate(rows, 0), np.stack(masks).astype(np.float32), len(levels)


def _gla_body(q_ref, k_ref, v_ref, glr_ref, r_ref, x_ref, wg2_ref, bg_ref, ng_ref, dall_ref, mask_ref,
              wo_ref, lg_ref, lb_ref, o_ref, st_ref, *, nlev, nck):
    c = GLA_CHUNK
    heads = GLA_HEADS
    dk = GLA_DK // heads
    dv = GLA_DV // heads

    @pl.when(pl.program_id(1) == 0)
    def _():
        st_ref[...] = jnp.zeros_like(st_ref)

    z = _dot(glr_ref[0].astype(BF16), wg2_ref[...]) + bg_ref[...]
    log_a = (jnp.minimum(z, 0.0) - jnp.log1p(jnp.exp(-jnp.abs(z)))) * (1.0 / GLA_TAU)
    hi = log_a.astype(BF16)
    lo = (log_a - hi.astype(F32)).astype(BF16)
    parts = []
    for ck in range(nck):
        parts += [hi[ck * c:(ck + 1) * c], lo[ck * c:(ck + 1) * c]]
    e2 = _dot(dall_ref[...], jnp.concatenate(parts, axis=-1))
    q = q_ref[0]
    k = k_ref[0]
    pre = []
    for ck in range(nck):
        rows = slice(ck * c, (ck + 1) * c)
        decay = jnp.exp(e2[:, 2 * ck * GLA_DK:(2 * ck + 1) * GLA_DK]
                        + e2[:, (2 * ck + 1) * GLA_DK:(2 * ck + 2) * GLA_DK])
        for h in range(heads):
            ks = slice(h * dk, (h + 1) * dk)
            qh = q[rows, ks]
            kh = k[rows, ks]
            tbl = lambda i: decay[i * c:(i + 1) * c, ks]
            q_in = (qh * tbl(0)).astype(BF16)
            k_out = (kh * tbl(1)).astype(BF16)
            khb = kh.astype(BF16)
            att = mask_ref[nlev - 1] * _dot_nt(qh.astype(BF16), khb)
            for li in range(nlev - 1):
                ql = (qh * tbl(2 + 2 * li)).astype(BF16)
                kl = (kh * tbl(3 + 2 * li)).astype(BF16) if li < nlev - 2 else khb
                att = att + mask_ref[li] * _dot_nt(ql, kl)
            total = decay[c - 1:c, ks]
            pre.append((att.astype(BF16), q_in, k_out, total))
    states = [st_ref[h] for h in range(heads)]
    r = r_ref[0]
    gate = r * jax.nn.sigmoid(r)
    gated = []
    for ck in range(nck):
        rows = slice(ck * c, (ck + 1) * c)
        outs = []
        for h in range(heads):
            att, q_in, k_out, total = pre[ck * heads + h]
            vh = v_ref[0, rows, h * dv:(h + 1) * dv]
            out = _dot(att, vh) + _dot_nt(q_in, states[h].astype(BF16))
            states[h] = states[h] * total + _dot_tn(vh, k_out)
            outs.append(out * lax.rsqrt(jnp.mean(out * out, axis=-1, keepdims=True) + LN_EPS) * ng_ref[...])
        gated.append((jnp.concatenate(outs, axis=-1) * gate[rows]).astype(BF16))
    for h in range(heads):
        st_ref[h] = states[h]
    y = _dot(jnp.concatenate(gated, axis=0), wo_ref[...])
    o_ref[0] = _layer_norm(DN_ALPHA * x_ref[0] + y, lg_ref[...], lb_ref[...])


def _gla_mixer_ln(x, w_in, w_gate2, b_gate, norm_g, w_out, ln_g, ln_b):
    b, s, d = x.shape
    c = GLA_CHUNK
    heads = GLA_HEADS
    dk = GLA_DK // heads
    dv = GLA_DV // heads
    x2 = x.reshape(b * s, d)
    w = w_in.astype(BF16)
    o0, o1, o2, o3 = GLA_DK, 2 * GLA_DK, 2 * GLA_DK + GLA_DV, 2 * GLA_DK + GLA_DV + GLA_GATE_RANK
    q, k, v, glr, r = _proj(
        x2, [w[:, :o0], w[:, o0:o1], w[:, o1:o2], _pad_cols(w[:, o2:o3], LANES), w[:, o3:]],
        [F32, F32, BF16, F32, F32], scales=(dk ** -0.5, 1.0, 1.0, 1.0, 1.0))
    wg2 = jnp.pad(w_gate2.astype(BF16), ((0, LANES - GLA_GATE_RANK), (0, 0)))
    dall, masks, nlev = _gla_tables(c)
    nck = GLA_STEP_CHUNKS
    tok = lambda width: pl.BlockSpec((1, nck * c, width), lambda i, j: (i, j, 0))
    return pl.pallas_call(
        functools.partial(_gla_body, nlev=nlev, nck=nck),
        grid=(b, s // (nck * c)),
        in_specs=[
            tok(GLA_DK), tok(GLA_DK), tok(GLA_DV), tok(LANES), tok(GLA_DV), tok(d),
            _full((LANES, GLA_DK)),
            _full((1, GLA_DK)),
            _full((1, dv)),
            _full(dall.shape),
            _full(masks.shape),
            _full((GLA_DV, d)),
            _full((1, d)),
            _full((1, d)),
        ],
        out_specs=tok(d),
        out_shape=jax.ShapeDtypeStruct((b, s, d), F32),
        scratch_shapes=[pltpu.VMEM((heads, dv, dk), F32)],
        compiler_params=_cparams("parallel", "arbitrary"),
        name="gla_core",
    )(q.reshape(b, s, -1), k.reshape(b, s, -1), v.reshape(b, s, -1), glr.reshape(b, s, -1),
      r.reshape(b, s, -1), x, wg2, b_gate.reshape(1, -1), norm_g.reshape(1, -1),
      jnp.asarray(dall, BF16), jnp.asarray(masks, F32), w_out.astype(BF16), ln_g.reshape(1, d), ln_b.reshape(1, d))


def _gelu_tanh(x):
    return 0.5 * x * (1.0 + jnp.tanh(np.sqrt(2.0 / np.pi).astype(np.float32) * (x + 0.044715 * (x * x * x))))


def _cmp_body(kc_ref, vc_ref, plo_ref, phi_ref, w1a_ref, w1b_ref, w2_ref, o_ref):
    nch, width = o_ref.shape[1], o_ref.shape[2]
    st = kc_ref.shape[1] // nch
    a = jnp.zeros((nch, width), F32)
    bm = jnp.zeros((nch, width), F32)
    for l in range(st):
        c = jnp.concatenate([kc_ref[0, pl.ds(l, nch, stride=st), :], vc_ref[0, pl.ds(l, nch, stride=st), :]], axis=1)
        ws = slice(l * width, (l + 1) * width)
        a = a + _dot((c + plo_ref[:, ws]).astype(BF16), w1a_ref[ws, :])
        bm = bm + _dot((c + phi_ref[:, ws]).astype(BF16), w1b_ref[ws, :])
    nxt = jnp.concatenate([bm[1:], jnp.zeros_like(bm[:1])], axis=0)
    row = lax.broadcasted_iota(jnp.int32, a.shape, 0)
    pre = jnp.where(row < a.shape[0] - 1, a + nxt, 0.0)
    o_ref[0] = _dot(_gelu_tanh(pre).astype(BF16), w2_ref[...]).astype(o_ref.dtype)


def _nsa_compress(kvc, cmp_pos, cmp_w1, cmp_w2):
    b, s, _ = kvc.shape
    half = LANES
    st, g, d = NSA_CMP_STRIDE, NSA_KV_GROUPS, HEAD_DIM
    nch = s // st
    width = 2 * g * d
    eye = jnp.eye(2 * g, dtype=F32)
    sel = jnp.repeat(jnp.eye(2, dtype=F32), g, axis=1)

    def expand(w1_half):
        wj = jnp.einsum('jlde,jc->lcde', w1_half, sel)
        return jnp.einsum('lcde,cf->lcdfe', wj, eye).reshape(st * width, width).astype(BF16)

    w1a = expand(cmp_w1[:, :st])
    w1b = expand(cmp_w1[:, st:])
    w2 = jnp.einsum('cde,cf->cdfe', jnp.einsum('jde,jc->cde', cmp_w2, sel), eye).reshape(width, width).astype(BF16)

    def pos_row(p_half):
        return jnp.einsum('jld,jc->lcd', p_half, sel).reshape(1, st * width)

    return pl.pallas_call(
        _cmp_body,
        grid=(b,),
        in_specs=[
            pl.BlockSpec((1, s, half), lambda i: (i, 0, 0)),
            pl.BlockSpec((1, s, half), lambda i: (i, 0, 1)),
            _full((1, st * width)),
            _full((1, st * width)),
            _full((st * width, width)),
            _full((st * width, width)),
            _full((width, width)),
        ],
        out_specs=pl.BlockSpec((1, nch, width), lambda i: (i, 0, 0)),
        out_shape=jax.ShapeDtypeStruct((b, nch, width), BF16),
        compiler_params=_cparams("parallel"),
        name="nsa_compress",
    )(kvc, kvc, pos_row(cmp_pos[:, :st]), pos_row(cmp_pos[:, st:]), w1a, w1b, w2)


def _nsa_overlap_t(s):
    n_cmp_pad = s // NSA_CMP_STRIDE
    n_sel = s // NSA_SEL_LEN
    blk = np.arange(n_cmp_pad) * NSA_CMP_STRIDE
    sel_start = np.arange(n_sel) * NSA_SEL_LEN
    ov = (blk[None, :] < sel_start[:, None] + NSA_SEL_LEN) & (blk[None, :] + NSA_CMP_LEN > sel_start[:, None])
    return ov.astype(np.float32)


NSA_MASK = 1.0e30


def _nsa_body(qt_ref, kc_ref, vct_ref, ks_ref, kw_ref, vt_ref, glt_ref, ovt_ref, o_ref, qt_scr, sa_scr, sb_scr,
              m_scr, acc_scr, *, tq):
    d = HEAD_DIM
    groups = NSA_KV_GROUPS
    rep = N_HEADS // groups
    n_cmp = kc_ref.shape[2]
    n_sel = ovt_ref.shape[0]
    t = pl.program_id(1)
    q0 = t * tq
    win_tiles = NSA_WINDOW // tq
    gates = jax.nn.sigmoid(glt_ref[...])

    tpos = (q0 + lax.broadcasted_iota(jnp.int32, (1, tq), 1)).astype(F32)
    brow = lax.broadcasted_iota(jnp.int32, (ATT_BIAS_ROWS, tq), 0)
    ones_rows = jnp.where(brow == 0, 1.0, 0.0).astype(BF16)
    ci = lax.broadcasted_iota(jnp.int32, (tq, tq), 0)
    ai = lax.broadcasted_iota(jnp.int32, (tq, tq), 1)
    tile8 = lambda m: jnp.concatenate([m] * rep, axis=1)
    diag_add = tile8(jnp.where(ci <= ai, 0.0, NEG_INF))
    part_add = tile8(jnp.where((ci > ai) & (t >= win_tiles), 0.0, NEG_INF))
    ncm = lax.broadcasted_iota(jnp.int32, (n_cmp, tq), 0)
    tcm = q0 + lax.broadcasted_iota(jnp.int32, (n_cmp, tq), 1)
    valid_c = ncm * NSA_CMP_STRIDE + (NSA_CMP_LEN - 1) <= tcm
    jm = lax.broadcasted_iota(jnp.int32, (n_sel, tq), 0)
    cur = (q0 + lax.broadcasted_iota(jnp.int32, (n_sel, tq), 1)) // NSA_SEL_LEN
    causal_sel = jm <= cur
    forced = (jm == 0) | (jm == cur) | (jm == cur - 1)
    pad_rows = jnp.zeros((ATT_QROWS - d - ATT_BIAS_ROWS - n_sel, tq), BF16)

    o_cmp = []
    for g in range(groups):
        heads = [(qt_ref[h * d:(h + 1) * d, :], _alibi_rows(h, tpos, brow)) for h in range(g * rep, (g + 1) * rep)]

        def build(mask_rows):
            cols = [jnp.concatenate([qh, bias, mask_rows, pad_rows], axis=0) for qh, bias in heads]
            return jnp.concatenate(cols, axis=1)

        qt_scr[2 * g] = build(jnp.zeros((n_sel, tq), BF16))

        s_c = _dot(kc_ref[0, g], qt_scr[2 * g])
        p_sum = jnp.zeros((n_cmp, tq), F32)
        ps = []
        for r in range(rep):
            sr = jnp.where(valid_c, s_c[:, r * tq:(r + 1) * tq], NEG_INF)
            m = jnp.max(sr, axis=0, keepdims=True)
            e = jnp.where(valid_c, jnp.exp2(sr - m), 0.0)
            den = jnp.sum(e, axis=0, keepdims=True)
            p = e * jnp.where(den > 0.0, 1.0 / den, 0.0)
            p_sum = p_sum + p
            ps.append(p.astype(BF16))
        o_cmp.append(_dot(vct_ref[0, g * d:(g + 1) * d, :], jnp.concatenate(ps, axis=1)))

        p_hi = p_sum.astype(BF16)
        p_lo = (p_sum - p_hi.astype(F32)).astype(BF16)
        ovt = ovt_ref[...]
        imp_t = _dot(ovt, p_hi) + _dot(ovt, p_lo)
        score = jnp.where(causal_sel, imp_t + jnp.where(forced, NSA_FORCE_BONUS, 0.0), NEG_INF)
        rank = jnp.zeros((n_sel, tq), F32)
        for mp in range(n_sel):
            row = score[mp:mp + 1, :]
            ahead = (row > score) | ((row == score) & (mp < jm))
            rank = rank + jnp.where(ahead, 1.0, 0.0)
        selected = (rank < NSA_TOPK) & causal_sel
        qt_scr[2 * g + 1] = build(jnp.where(selected, 0.0, -NSA_MASK).astype(BF16))

    def score(chains, kt, slot):
        k0 = pl.multiple_of(kt * tq, tq)
        for ci, (k_ref, g, _, qi, _) in enumerate(chains):
            slot[ci] = _dot(k_ref[0, g, pl.ds(k0, tq), :], qt_scr[qi])

    def absorb(chains, kt, add, slot):
        for ci, (_, _, v_row0, _, si) in enumerate(chains):
            s = slot[ci]
            if add is not None:
                s = s + add
            m_prev = m_scr[si]
            m_new = jnp.maximum(m_prev, jnp.max(s, axis=0, keepdims=True))
            e = jnp.exp2(s - m_new)
            alpha = jnp.exp2(m_prev - m_new)
            vals = jnp.concatenate([vt_ref[0, kt, v_row0:v_row0 + d, :], ones_rows], axis=0)
            acc_scr[si] = alpha * acc_scr[si] + _dot(vals, e.astype(BF16))
            m_scr[si] = m_new

    def sweep(chains, lo):
        n = t - lo

        def pair(j, carry):
            a = lo + 2 * j
            score(chains, a + 1, sb_scr)
            absorb(chains, a, None, sa_scr)
            score(chains, a + 2, sa_scr)
            absorb(chains, a + 1, None, sb_scr)
            return carry

        lax.fori_loop(0, n // 2, pair, 0)

        @pl.when(n % 2 == 1)
        def _():
            score(chains, t, sb_scr)
            absorb(chains, t - 1, None, sa_scr)
            absorb(chains, t, diag_add, sb_scr)

        @pl.when(n % 2 == 0)
        def _():
            absorb(chains, t, diag_add, sa_scr)

    m_scr[...] = jnp.full(m_scr.shape, NEG_INF, F32)
    acc_scr[...] = jnp.zeros(acc_scr.shape, F32)

    sel_chains = [(ks_ref, g, g * d, 2 * g + 1, g) for g in range(groups)]
    score(sel_chains, 0, sa_scr)
    sweep(sel_chains, 0)

    win_chains = [(kw_ref, g, (groups + g) * d, 2 * g, groups + g) for g in range(groups)]
    oldest = jnp.maximum(t - win_tiles, 0)
    lo = jnp.maximum(t - win_tiles + 1, 0)
    score(win_chains, oldest, sb_scr)
    score(win_chains, jnp.minimum(lo, t), sa_scr)
    absorb(win_chains, oldest, part_add, sb_scr)
    sweep(win_chains, lo)

    o_sel = [acc_scr[g, :d] / acc_scr[g, d:d + 1] for g in range(groups)]
    o_win = [acc_scr[groups + g, :d] / acc_scr[groups + g, d:d + 1] for g in range(groups)]

    for g in range(groups):
        for r in range(rep):
            h = g * rep + r
            cs = slice(r * tq, (r + 1) * tq)
            o = (gates[h:h + 1] * o_cmp[g][:, cs] + gates[N_HEADS + h:N_HEADS + h + 1] * o_sel[g][:, cs]
                 + gates[2 * N_HEADS + h:2 * N_HEADS + h + 1] * o_win[g][:, cs])
            o_ref[h * d:(h + 1) * d, :] = o.astype(o_ref.dtype)


def _nsa_mixer_ln(x, w_in, cmp_pos, cmp_w1, cmp_w2, w_out, ln_g, ln_b):
    b, s, d = x.shape
    hd = N_HEADS * HEAD_DIM
    groups = NSA_KV_GROUPS
    gd = groups * HEAD_DIM
    n = b * s
    tq = ATT_TQ
    n_sel = s // NSA_SEL_LEN
    n_cmp = s // NSA_CMP_STRIDE
    assert HEAD_DIM + ATT_BIAS_ROWS + n_sel <= ATT_QROWS and n_sel % 16 == 0 and NSA_WINDOW % tq == 0
    x2 = x.reshape(n, d)
    w = w_in.astype(BF16)
    c0 = hd + 2 * gd
    w_k2 = jnp.concatenate([w[:, c0:c0 + gd], w[:, c0 + 2 * gd:c0 + 3 * gd]], axis=1)
    w_v2 = jnp.concatenate([w[:, c0 + gd:c0 + 2 * gd], w[:, c0 + 3 * gd:c0 + 4 * gd]], axis=1)
    tok = np.arange(s, dtype=np.float64)
    qt, kvc, k_sw, vt, glt = _proj_mixed(
        x2, [(w[:, :hd], "feat", BF16, HEAD_DIM ** -0.5 * LOG2E),
             (w[:, hd:c0], "tok", F32, 1.0),
             (w_k2, "keys", BF16, 1.0, _key_tail(tok, n_sel, NSA_SEL_LEN)),
             (w_v2, "feat_tiles", BF16, 1.0),
             (_pad_cols(w[:, c0 + 4 * gd:], LANES), "feat", F32, 1.0)], seq=s)

    cmp = _nsa_compress(kvc.reshape(b, s, 2 * gd), cmp_pos, cmp_w1, cmp_w2)
    cpos = np.arange(n_cmp, dtype=np.float64) * NSA_CMP_STRIDE + (NSA_CMP_LEN - 1) / 2
    k_cmp = _keys_with_tail(cmp[:, :, :gd], groups, _key_tail(cpos))
    v_cmp_t = cmp[:, :, gd:].transpose(0, 2, 1)
    vt = vt.reshape(b, s // tq, 2 * gd, tq)
    ovt = _nsa_overlap_t(s)
    nt = s // tq
    o_t = pl.pallas_call(
        functools.partial(_nsa_body, tq=tq),
        grid=(b, nt),
        in_specs=[
            pl.BlockSpec((hd, tq), lambda i, j: (0, i * nt + j)),
            pl.BlockSpec((1, groups, n_cmp, ATT_QROWS), lambda i, j: (i, 0, 0, 0)),
            pl.BlockSpec((1, gd, n_cmp), lambda i, j: (i, 0, 0)),
            pl.BlockSpec((1, groups, s, ATT_QROWS), lambda i, j: (i, 0, 0, 0)),
            pl.BlockSpec((1, groups, s, ATT_QROWS), lambda i, j: (i, 1, 0, 0)),
            pl.BlockSpec((1, nt, 2 * gd, tq), lambda i, j: (i, 0, 0, 0)),
            pl.BlockSpec((LANES, tq), lambda i, j: (0, i * nt + j)),
            _full(ovt.shape),
        ],
        out_specs=pl.BlockSpec((hd, tq), lambda i, j: (0, i * nt + j)),
        out_shape=jax.ShapeDtypeStruct((hd, n), BF16),
        scratch_shapes=[pltpu.VMEM((2 * groups, ATT_QROWS, (N_HEADS // groups) * tq), BF16),
                        pltpu.VMEM((groups, tq, (N_HEADS // groups) * tq), F32),
                        pltpu.VMEM((groups, tq, (N_HEADS // groups) * tq), F32),
                        pltpu.VMEM((2 * groups, 1, (N_HEADS // groups) * tq), F32),
                        pltpu.VMEM((2 * groups, HEAD_DIM + ATT_BIAS_ROWS, (N_HEADS // groups) * tq), F32)],
        compiler_params=_cparams("parallel", "parallel"),
        name="nsa_core",
    )(qt, k_cmp, v_cmp_t, k_sw, k_sw, vt, glt, jnp.asarray(ovt, BF16))
    return _oproj_ln(o_t, w_out, x2, ln_g, ln_b, feature_major=True).reshape(b, s, d)


def kernel(x, mem, ln_g, ln_b, ffn1_w_gate, ffn1_w_up, ffn1_w_down, ffn2_w_gate, ffn2_w_up, ffn2_w_down,
           xattn_w_q, xattn_w_kv, xattn_w_o, swa_w_in, swa_sinks, swa_w_out, nsa_w_in, nsa_cmp_pos,
           nsa_cmp_w1, nsa_cmp_w2, nsa_w_out, gla_w_in, gla_w_gate2, gla_b_gate, gla_norm_g, gla_w_out,
           pool_w, pool_scale):
    b, s, d = x.shape
    ffn1 = [w.astype(BF16) for w in (ffn1_w_gate, ffn1_w_up, ffn1_w_down)]
    ffn2 = [w.astype(BF16) for w in (ffn2_w_gate, ffn2_w_up, ffn2_w_down)]
    xq, xkv, xo = (w.astype(BF16) for w in (xattn_w_q, xattn_w_kv, xattn_w_o))
    for i in range(DEPTH):
        kind = i % N_MIXERS
        j = i // N_MIXERS
        x = _ffn_ln(x.reshape(b * s, d), *ffn1, i, ln_g[i, 0], ln_b[i, 0]).reshape(b, s, d)
        if kind == 0:
            x = _swa_mixer_ln(x, swa_w_in[j], swa_sinks[j], swa_w_out[j], ln_g[i, 1], ln_b[i, 1])
        elif kind == 1:
            x = _nsa_mixer_ln(x, nsa_w_in[j], nsa_cmp_pos[j], nsa_cmp_w1[j], nsa_cmp_w2[j], nsa_w_out[j],
                              ln_g[i, 1], ln_b[i, 1])
        elif kind == 2:
            x = _gla_mixer_ln(x, gla_w_in[j], gla_w_gate2[j], gla_b_gate[j], gla_norm_g[j], gla_w_out[j],
                              ln_g[i, 1], ln_b[i, 1])
        else:
            x = _pool_mixer_ln(x, pool_w[j], pool_scale[j], ln_g[i, 1], ln_b[i, 1])
        x = _xattn_ln(x, mem, xq, xkv[i], xo, i, ln_g[i, 2], ln_b[i, 2])
        x = _ffn_ln(x.reshape(b * s, d), *ffn2, i, ln_g[i, 3], ln_b[i, 3]).reshape(b, s, d)
    return x
```

```python
import functools

import numpy as np
import jax
import jax.numpy as jnp
from jax import lax
from jax.experimental import pallas as pl
from jax.experimental.pallas import tpu as pltpu

F32 = jnp.float32
BF16 = jnp.bfloat16

D_MODEL = 1024
DEPTH = 4
N_MIXERS = 4
HEAD_DIM = 64
N_HEADS = D_MODEL // HEAD_DIM
SWA_KV_HEADS = 4
SWA_WINDOW = 128
NSA_KV_GROUPS = 2
NSA_CMP_LEN = 32
NSA_CMP_STRIDE = 16
NSA_SEL_LEN = 64
NSA_TOPK = 8
NSA_WINDOW = 512
NSA_FORCE_BONUS = 1.0e4
GLA_HEADS = 4
GLA_DK = D_MODEL // 2
GLA_DV = D_MODEL
GLA_GATE_RANK = 16
GLA_TAU = 16.0
GLA_CHUNK = 64
POOL_WINDOWS = (2, 4, 8, 16)
POOL_GROUP = D_MODEL // 4
XATTN_HEADS = 4
DN_ALPHA = (2 * DEPTH) ** 0.25
LN_EPS = 1e-5
NEG_INF = -1e30
LOG2E = float(np.log2(np.e))

LANES = 128
V7X_VMEM_LIMIT_BYTES = 56 * 1024 * 1024

_NT = (((1,), (1,)), ((), ()))
_TN = (((0,), (0,)), ((), ()))


def _cparams(*sem, flags=None):
    return pltpu.CompilerParams(dimension_semantics=sem, vmem_limit_bytes=V7X_VMEM_LIMIT_BYTES, flags=flags)


def _dot(a, b):
    return jnp.dot(a, b, preferred_element_type=F32)


def _dot_nt(a, b):
    return lax.dot_general(a, b, _NT, preferred_element_type=F32)


def _dot_tn(a, b):
    return lax.dot_general(a, b, _TN, preferred_element_type=F32)


def _layer_norm(y, g, b):
    mu = jnp.mean(y, axis=-1, keepdims=True)
    yc = y - mu
    var = jnp.mean(yc * yc, axis=-1, keepdims=True)
    return yc * lax.rsqrt(var + LN_EPS) * g + b


def _alibi_slope(h, n):
    return float(2.0 ** (-8.0 * (h + 1) / n))


def _full(shape):
    nd = len(shape)
    return pl.BlockSpec(shape, lambda *_: (0,) * nd, pipeline_mode=pl.Buffered(1))


FFN_TM = 1024
FFN_SUB = 512
FFN_TF = 256


def _ffn_body(x_ref, wg_ref, wu_ref, wd_ref, g_ref, b_ref, o_ref, acc_ref, *, tf, sub):
    nf = wd_ref.shape[0] // tf
    for si in range(x_ref.shape[0] // sub):
        rows = slice(si * sub, (si + 1) * sub)
        x = x_ref[rows, :]
        xb = x.astype(BF16)
        for c in range(nf):
            cs = slice(c * tf, (c + 1) * tf)
            gate = _dot(xb, wg_ref[:, cs])
            up = _dot(xb, wu_ref[:, cs])
            h = (gate * jax.nn.sigmoid(gate) * up).astype(BF16)
            d = _dot(h, wd_ref[cs, :])
            if c == 0:
                acc_ref[si] = d
            else:
                acc_ref[si] += d
        y = DN_ALPHA * x + 0.5 * acc_ref[si]
        o_ref[rows, :] = _layer_norm(y, g_ref[...], b_ref[...])


def _layer_of(stacked, layer):
    _, r, c = stacked.shape
    return pl.BlockSpec((None, r, c), lambda *_: (layer, 0, 0), pipeline_mode=pl.Buffered(1))


def _ffn_ln(x2, w_gate, w_up, w_down, layer, ln_g, ln_b):
    n, d = x2.shape
    tm = min(FFN_TM, n)
    sub = min(FFN_SUB, tm)
    return pl.pallas_call(
        functools.partial(_ffn_body, tf=FFN_TF, sub=sub),
        grid=(n // tm,),
        in_specs=[
            pl.BlockSpec((tm, d), lambda i: (i, 0)),
            _layer_of(w_gate, layer),
            _layer_of(w_up, layer),
            _layer_of(w_down, layer),
            _full((1, d)),
            _full((1, d)),
        ],
        out_specs=pl.BlockSpec((tm, d), lambda i: (i, 0)),
        out_shape=jax.ShapeDtypeStruct((n, d), F32),
        scratch_shapes=[pltpu.VMEM((tm // sub, sub, d), F32)],
        compiler_params=_cparams("parallel"),
        name="ffn_ln",
    )(x2, w_gate, w_up, w_down, ln_g.reshape(1, d), ln_b.reshape(1, d))


PROJ_TM = 1024


def _proj_body(x_ref, *refs, scales):
    n = len(scales)
    xb = x_ref[...].astype(BF16)
    for w_ref, o_ref, s in zip(refs[:n], refs[n:], scales):
        r = _dot(xb, w_ref[...])
        if s != 1.0:
            r = r * s
        o_ref[...] = r.astype(o_ref.dtype)


def _proj(x2, weights, dtypes, scales=None):
    n, k = x2.shape
    scales = tuple(scales) if scales is not None else (1.0,) * len(weights)
    tm = min(PROJ_TM, n)
    in_specs = [pl.BlockSpec((tm, k), lambda i: (i, 0))] + [_full(w.shape) for w in weights]
    out_specs = [pl.BlockSpec((tm, w.shape[1]), lambda i: (i, 0)) for w in weights]
    out_shape = [jax.ShapeDtypeStruct((n, w.shape[1]), dt) for w, dt in zip(weights, dtypes)]
    return pl.pallas_call(
        functools.partial(_proj_body, scales=scales),
        grid=(n // tm,),
        in_specs=in_specs,
        out_specs=out_specs,
        out_shape=out_shape,
        compiler_params=_cparams("parallel"),
        name="proj",
    )(x2, *weights)


def _pad_cols(w, m):
    return jnp.pad(w, ((0, 0), (0, m - w.shape[1])))


OPROJ_TM = 1024


def _oproj_body(o_ref, w_ref, x_ref, g_ref, b_ref, out_ref, *, feature_major):
    y = _dot_tn(o_ref[...], w_ref[...]) if feature_major else _dot(o_ref[...], w_ref[...])
    out_ref[...] = _layer_norm(DN_ALPHA * x_ref[...] + y, g_ref[...], b_ref[...])


def _oproj_ln(o2, w_out, x2, ln_g, ln_b, feature_major=False):
    n, d = x2.shape
    k = w_out.shape[0]
    tm = min(OPROJ_TM, n)
    return pl.pallas_call(
        functools.partial(_oproj_body, feature_major=feature_major),
        grid=(n // tm,),
        in_specs=[
            pl.BlockSpec((k, tm), lambda i: (0, i)) if feature_major else pl.BlockSpec((tm, k), lambda i: (i, 0)),
            _full((k, d)),
            pl.BlockSpec((tm, d), lambda i: (i, 0)),
            _full((1, d)),
            _full((1, d)),
        ],
        out_specs=pl.BlockSpec((tm, d), lambda i: (i, 0)),
        out_shape=jax.ShapeDtypeStruct((n, d), F32),
        compiler_params=_cparams("parallel"),
        name="oproj_ln",
    )(o2, w_out.astype(BF16), x2, ln_g.reshape(1, d), ln_b.reshape(1, d))


XATTN_TQ = 1024
XATTN_SUB = 512


def _xattn_body(x_ref, kv_ref, wq_ref, wo_ref, g_ref, b_ref, o_ref, *, heads, sub):
    d = x_ref.shape[-1]
    dh = d // heads
    chains = [slice(si * sub, (si + 1) * sub) for si in range(x_ref.shape[1] // sub)]
    xs = [x_ref[0, rows, :] for rows in chains]
    qs = [(_dot(x.astype(BF16), wq_ref[...]) * dh ** -0.5).astype(BF16) for x in xs]
    scores = [[_dot_nt(q[:, h * dh:(h + 1) * dh], kv_ref[0, :, h * dh:(h + 1) * dh]) for h in range(heads)]
              for q in qs]
    outs = [[] for _ in chains]
    for h in range(heads):
        vh = kv_ref[0, :, d + h * dh:d + (h + 1) * dh]
        for ci in range(len(chains)):
            s = scores[ci][h]
            m = jnp.max(s, axis=-1, keepdims=True)
            e = jnp.exp(s - m)
            p = e / jnp.sum(e, axis=-1, keepdims=True)
            outs[ci].append(_dot(p.astype(BF16), vh).astype(BF16))
    ys = [_dot(jnp.concatenate(o, axis=-1), wo_ref[...]) for o in outs]
    for rows, x, y in zip(chains, xs, ys):
        o_ref[0, rows, :] = _layer_norm(DN_ALPHA * x + y, g_ref[...], b_ref[...])


def _xattn_ln(x, mem, w_q, w_kv, w_o, layer, ln_g, ln_b):
    b, s, d = x.shape
    m = mem.shape[1]
    (kv,) = _proj(mem.reshape(b * m, d), [w_kv], [BF16])
    kv = kv.reshape(b, m, 2 * d)
    tq = min(XATTN_TQ, s)
    return pl.pallas_call(
        functools.partial(_xattn_body, heads=XATTN_HEADS, sub=min(XATTN_SUB, tq)),
        grid=(b, s // tq),
        in_specs=[
            pl.BlockSpec((1, tq, d), lambda i, j: (i, j, 0)),
            pl.BlockSpec((1, m, 2 * d), lambda i, j: (i, 0, 0)),
            _layer_of(w_q, layer),
            _layer_of(w_o, layer),
            _full((1, d)),
            _full((1, d)),
        ],
        out_specs=pl.BlockSpec((1, tq, d), lambda i, j: (i, j, 0)),
        out_shape=jax.ShapeDtypeStruct((b, s, d), F32),
        compiler_params=_cparams("parallel", "parallel"),
        name="xattn_ln",
    )(x, kv, w_q, w_o, ln_g.reshape(1, d), ln_b.reshape(1, d))


ATT_TQ = 128
ATT_QROWS = 128
ATT_BIAS_ROWS = 16


def _split3_bf16(x):
    out = []
    r = np.float32(x)
    for _ in range(3):
        p = np.float32(np.asarray(r, np.float32).astype(BF16).astype(np.float32))
        out.append(float(p))
        r = np.float32(r - p)
    return out


def _alibi_rows(h, tpos, brow):
    slope = _alibi_slope(h, N_HEADS) * LOG2E
    pieces = _split3_bf16(slope * LANES) + _split3_bf16(slope)
    v = -np.float32(slope) * tpos
    v_hi = v.astype(BF16).astype(F32)
    v_mid = (v - v_hi).astype(BF16).astype(F32)
    v_lo = v - v_hi - v_mid
    bias = jnp.zeros(brow.shape, F32)
    for k, val in enumerate(pieces + [v_hi, v_mid, v_lo]):
        bias = jnp.where(brow == k, val, bias)
    return bias.astype(BF16)


def _key_tail(pos, n_blocks=0, block_len=1):
    n = pos.shape[0]
    hi = np.floor(pos / LANES)
    lo = pos - hi * LANES
    tail = np.zeros((n, ATT_QROWS - HEAD_DIM), np.float32)
    tail[:, 0:3] = hi[:, None]
    tail[:, 3:6] = lo[:, None]
    tail[:, 6:9] = 1.0
    if n_blocks:
        blk = (pos // block_len).astype(np.int64)
        tail[np.arange(n), ATT_BIAS_ROWS + blk] = 1.0
    return tail


def _keys_with_tail(k_tok, groups, tail):
    b, rows, _ = k_tok.shape
    k = k_tok.reshape(b, rows, groups, HEAD_DIM).transpose(0, 2, 1, 3)
    t_b = jnp.broadcast_to(jnp.asarray(tail, BF16), (b, groups, rows, tail.shape[1]))
    return jnp.concatenate([k, t_b], axis=-1)


def _proj_mixed_body(x_ref, *refs, kinds, scales, tq):
    n = len(kinds)
    xb = x_ref[...].astype(BF16)
    tails = iter(refs[n:len(refs) - n])
    for w_ref, o_ref, kind, s in zip(refs[:n], refs[len(refs) - n:], kinds, scales):
        r = _dot(xb, w_ref[...]) if kind in ("tok", "keys") else _dot_nt(w_ref[...], xb)
        if s != 1.0:
            r = r * s
        r = r.astype(o_ref.dtype)
        if kind == "feat_tiles":
            for c in range(o_ref.shape[0]):
                o_ref[c] = r[:, c * tq:(c + 1) * tq]
        elif kind == "keys":
            tail = next(tails)[...]
            for g in range(o_ref.shape[1]):
                o_ref[0, g] = jnp.concatenate([r[:, g * HEAD_DIM:(g + 1) * HEAD_DIM], tail], axis=1)
        else:
            o_ref[...] = r


def _proj_mixed(x2, specs, seq, tq=ATT_TQ):
    n, k = x2.shape
    tm = min(PROJ_TM, seq)
    per_seq = seq // tm
    kinds = tuple(s[1] for s in specs)
    ws, tails, tail_specs, out_specs, out_shape = [], [], [], [], []
    for spec in specs:
        w, kind, dt = spec[:3]
        m = w.shape[1]
        if kind == "tok":
            ws.append(w)
            out_specs.append(pl.BlockSpec((tm, m), lambda i: (i, 0)))
            out_shape.append(jax.ShapeDtypeStruct((n, m), dt))
        elif kind == "feat":
            ws.append(w.T)
            out_specs.append(pl.BlockSpec((m, tm), lambda i: (0, i)))
            out_shape.append(jax.ShapeDtypeStruct((m, n), dt))
        elif kind == "feat_tiles":
            ws.append(w.T)
            out_specs.append(pl.BlockSpec((tm // tq, m, tq), lambda i: (i, 0, 0)))
            out_shape.append(jax.ShapeDtypeStruct((n // tq, m, tq), dt))
        else:
            ws.append(w)
            groups = m // HEAD_DIM
            tails.append(jnp.asarray(spec[4], dt))
            tail_specs.append(pl.BlockSpec((tm, ATT_QROWS - HEAD_DIM), lambda i: (i % per_seq, 0)))
            out_specs.append(pl.BlockSpec((1, groups, tm, ATT_QROWS), lambda i: (i // per_seq, 0, i % per_seq, 0)))
            out_shape.append(jax.ShapeDtypeStruct((n // seq, groups, seq, ATT_QROWS), dt))
    return pl.pallas_call(
        functools.partial(_proj_mixed_body, kinds=kinds, scales=tuple(s[3] for s in specs), tq=tq),
        grid=(n // tm,),
        in_specs=[pl.BlockSpec((tm, k), lambda i: (i, 0))] + [_full(w.shape) for w in ws] + tail_specs,
        out_specs=out_specs,
        out_shape=out_shape,
        compiler_params=_cparams("parallel"),
        name="proj_mixed",
    )(x2, *ws, *tails)


SWA_CHAIN_HEADS = 4
SWA_LOOKAHEAD = 2


def _swa_body(sink_ref, qt_ref, k_ref, vt_ref, o_ref, *, tq):
    d = HEAD_DIM
    groups = SWA_KV_HEADS
    rep = N_HEADS // groups
    t = pl.program_id(1)
    first = t == 0
    base = jnp.maximum(t - 1, 0)
    kb = pl.multiple_of(base * tq, tq)
    tpos = (t * tq + lax.broadcasted_iota(jnp.int32, (1, tq), 1)).astype(F32)
    brow = lax.broadcasted_iota(jnp.int32, (ATT_BIAS_ROWS, tq), 0)
    ones_rows = jnp.where(lax.broadcasted_iota(jnp.int32, (ATT_BIAS_ROWS, 2 * tq), 0) == 0, 1.0, 0.0).astype(BF16)
    ci = lax.broadcasted_iota(jnp.int32, (tq, tq), 0)
    ai = lax.broadcasted_iota(jnp.int32, (tq, tq), 1)
    hb = SWA_CHAIN_HEADS
    older = jnp.where(ci > ai, 0.0, NEG_INF)
    own = jnp.where(ci <= ai, 0.0, NEG_INF)
    add = jnp.concatenate([jnp.where(first, own, older), jnp.where(first, NEG_INF, own)], axis=0)
    add = jnp.concatenate([add] * hb, axis=1)
    pad_rows = jnp.zeros((ATT_QROWS - d - ATT_BIAS_ROWS, tq), BF16)

    def issue(c):
        g = (c * hb) // rep
        qt = jnp.concatenate(
            [jnp.concatenate([qt_ref[h * d:(h + 1) * d, :], _alibi_rows(h, tpos, brow), pad_rows], axis=0)
             for h in range(c * hb, (c + 1) * hb)], axis=1)
        return _dot(k_ref[0, g, pl.ds(kb, 2 * tq), :], qt)

    def finish(c, s):
        g = (c * hb) // rep
        sink = jnp.concatenate([jnp.full((1, tq), sink_ref[h] * LOG2E, F32) for h in range(c * hb, (c + 1) * hb)], axis=1)
        m = jnp.maximum(jnp.max(s, axis=0, keepdims=True), sink)
        e = jnp.exp2(s - m)
        vals = jnp.concatenate([vt_ref[0, base, g * d:(g + 1) * d, :], vt_ref[0, base + 1, g * d:(g + 1) * d, :]], axis=1)
        acc = _dot(jnp.concatenate([vals, ones_rows], axis=0), e.astype(BF16))
        o = acc[:d] / (acc[d:d + 1] + jnp.exp2(sink - m))
        for r in range(hb):
            h = c * hb + r
            o_ref[h * d:(h + 1) * d, :] = o[:, r * tq:(r + 1) * tq].astype(o_ref.dtype)

    n_chains = N_HEADS // hb
    pending = [issue(c) for c in range(min(SWA_LOOKAHEAD, n_chains))]
    s_cur = pending[0] + add
    for c in range(n_chains):
        s_next = pending[c + 1] + add if c + 1 < n_chains else None
        finish(c, s_cur)
        if c + SWA_LOOKAHEAD < n_chains:
            pending.append(issue(c + SWA_LOOKAHEAD))
        s_cur = s_next


def _swa_mixer_ln(x, w_in, sinks, w_out, ln_g, ln_b):
    b, s, d = x.shape
    hd = N_HEADS * HEAD_DIM
    groups = SWA_KV_HEADS
    gd = groups * HEAD_DIM
    n = b * s
    tq = ATT_TQ
    nt = s // tq
    assert SWA_WINDOW == tq and nt >= 2
    x2 = x.reshape(n, d)
    w = w_in.astype(BF16)
    tail = _key_tail(np.arange(s, dtype=np.float64))
    qt, kk, vt = _proj_mixed(x2, [(w[:, :hd], "feat", BF16, HEAD_DIM ** -0.5 * LOG2E),
                                  (w[:, hd:hd + gd], "keys", BF16, 1.0, tail),
                                  (w[:, hd + gd:], "feat_tiles", BF16, 1.0)], seq=s)
    o_t = pl.pallas_call(
        functools.partial(_swa_body, tq=tq),
        grid=(b, nt),
        in_specs=[
            pl.BlockSpec(memory_space=pltpu.SMEM),
            pl.BlockSpec((hd, tq), lambda i, j: (0, i * nt + j)),
            pl.BlockSpec((1, groups, s, ATT_QROWS), lambda i, j: (i, 0, 0, 0)),
            pl.BlockSpec((1, nt, gd, tq), lambda i, j: (i, 0, 0, 0)),
        ],
        out_specs=pl.BlockSpec((hd, tq), lambda i, j: (0, i * nt + j)),
        out_shape=jax.ShapeDtypeStruct((hd, n), BF16),
        compiler_params=_cparams("parallel", "parallel"),
        name="swa_core",
    )(sinks.astype(F32), qt, kk, vt.reshape(b, nt, gd, tq))
    return _oproj_ln(o_t, w_out, x2, ln_g, ln_b, feature_major=True).reshape(b, s, d)


POOL_TM = 1024
POOL_HALO = 16
POOL_LEAD = 8


def _pool_body(x_ref, halo_ref, w_ref, sc_ref, g_ref, b_ref, o_ref, ext_ref, tmp_ref, *, tm):
    t = pl.program_id(1)
    x = x_ref[0]
    lead = POOL_LEAD
    top = lead + POOL_HALO
    rows = top + tm
    ext_ref[0:lead, :] = jnp.zeros((lead, x.shape[1]), F32)
    tmp_ref[0:lead, :] = jnp.zeros((lead, x.shape[1]), F32)
    ext_ref[lead:top, :] = jnp.where(t > 0, halo_ref[0], 0.0)
    ext_ref[top:, :] = x
    row = t * tm + lax.broadcasted_iota(jnp.int32, (tm, 1), 0)
    ys = []
    for gi, w in enumerate(POOL_WINDOWS):
        cs = slice(gi * POOL_GROUP, (gi + 1) * POOL_GROUP)
        xg = x[:, cs]
        src, dst = ext_ref, tmp_ref
        shift = 1
        while shift < w:
            dst[lead:rows, cs] = src[lead:rows, cs] + src[lead - shift:rows - shift, cs]
            src, dst = dst, src
            shift *= 2
        acc = src[top:rows, cs]
        cnt = jnp.minimum(row + 1, w).astype(F32)
        diff = (acc / cnt - xg).astype(BF16)
        ys.append(_dot(diff, w_ref[gi]))
    y = jnp.concatenate(ys, axis=-1) * sc_ref[...]
    o_ref[0] = _layer_norm(DN_ALPHA * x + y, g_ref[...], b_ref[...])


def _pool_mixer_ln(x, w_grp, scale, ln_g, ln_b):
    b, s, d = x.shape
    tm = min(POOL_TM, s)
    hb = tm // POOL_HALO
    ng = len(POOL_WINDOWS)
    assert all(w & (w - 1) == 0 and w // 2 <= POOL_LEAD and w <= POOL_HALO for w in POOL_WINDOWS)
    return pl.pallas_call(
        functools.partial(_pool_body, tm=tm),
        grid=(b, s // tm),
        in_specs=[
            pl.BlockSpec((1, tm, d), lambda i, j: (i, j, 0)),
            pl.BlockSpec((1, POOL_HALO, d), lambda i, j: (i, jnp.maximum(j * hb - 1, 0), 0)),
            _full((ng, POOL_GROUP, POOL_GROUP)),
            _full((1, d)),
            _full((1, d)),
            _full((1, d)),
        ],
        out_specs=pl.BlockSpec((1, tm, d), lambda i, j: (i, j, 0)),
        out_shape=jax.ShapeDtypeStruct((b, s, d), F32),
        scratch_shapes=[pltpu.VMEM((POOL_LEAD + POOL_HALO + tm, d), F32)] * 2,
        compiler_params=_cparams("parallel", "parallel"),
        name="pool_ln",
    )(x, x, w_grp.astype(BF16), scale.reshape(1, d), ln_g.reshape(1, d), ln_b.reshape(1, d))


GLA_STEP_CHUNKS = 8


def _gla_tables(c):
    levels = []
    s = c
    while s >= 1:
        levels.append(s)
        s //= 2
    rows, masks = [], []
    idx = np.arange(c)
    for s in levels:
        dq = np.zeros((c, c), np.float32)
        dk = np.zeros((c, c), np.float32)
        for i in range(c):
            blk = i // s
            if s == c or blk % 2 == 1:
                dq[i, blk * s:i + 1] = 1.0
            dk[i, i + 1:blk * s + s] = 1.0
        rows.append(dq)
        if s > 1:
            rows.append(dk)
        if s < c:
            masks.append(((idx[:, None] // (2 * s) == idx[None, :] // (2 * s))
                          & ((idx[:, None] // s) % 2 == 1) & ((idx[None, :] // s) % 2 == 0)))
    masks.append(np.eye(c, dtype=bool))
    return np.concatenate(rows, 0), np.stack(masks).astype(np.float32), len(levels)


def _gla_body(q_ref, k_ref, v_ref, glr_ref, r_ref, x_ref, wg2_ref, bg_ref, ng_ref, dall_ref, mask_ref,
              wo_ref, lg_ref, lb_ref, o_ref, st_ref, *, nlev, nck):
    c = GLA_CHUNK
    heads = GLA_HEADS
    dk = GLA_DK // heads
    dv = GLA_DV // heads

    @pl.when(pl.program_id(1) == 0)
    def _():
        st_ref[...] = jnp.zeros_like(st_ref)

    z = _dot(glr_ref[0].astype(BF16), wg2_ref[...]) + bg_ref[...]
    log_a = (jnp.minimum(z, 0.0) - jnp.log1p(jnp.exp(-jnp.abs(z)))) * (1.0 / GLA_TAU)
    hi = log_a.astype(BF16)
    lo = (log_a - hi.astype(F32)).astype(BF16)
    parts = []
    for ck in range(nck):
        parts += [hi[ck * c:(ck + 1) * c], lo[ck * c:(ck + 1) * c]]
    e2 = _dot(dall_ref[...], jnp.concatenate(parts, axis=-1))
    q = q_ref[0]
    k = k_ref[0]
    pre = []
    for ck in range(nck):
        rows = slice(ck * c, (ck + 1) * c)
        decay = jnp.exp(e2[:, 2 * ck * GLA_DK:(2 * ck + 1) * GLA_DK]
                        + e2[:, (2 * ck + 1) * GLA_DK:(2 * ck + 2) * GLA_DK])
        for h in range(heads):
            ks = slice(h * dk, (h + 1) * dk)
            qh = q[rows, ks]
            kh = k[rows, ks]
            tbl = lambda i: decay[i * c:(i + 1) * c, ks]
            q_in = (qh * tbl(0)).astype(BF16)
            k_out = (kh * tbl(1)).astype(BF16)
            khb = kh.astype(BF16)
            att = mask_ref[nlev - 1] * _dot_nt(qh.astype(BF16), khb)
            for li in range(nlev - 1):
                ql = (qh * tbl(2 + 2 * li)).astype(BF16)
                kl = (kh * tbl(3 + 2 * li)).astype(BF16) if li < nlev - 2 else khb
                att = att + mask_ref[li] * _dot_nt(ql, kl)
            total = decay[c - 1:c, ks]
            pre.append((att.astype(BF16), q_in, k_out, total))
    states = [st_ref[h] for h in range(heads)]
    r = r_ref[0]
    gate = r * jax.nn.sigmoid(r)
    gated = []
    for ck in range(nck):
        rows = slice(ck * c, (ck + 1) * c)
        outs = []
        for h in range(heads):
            att, q_in, k_out, total = pre[ck * heads + h]
            vh = v_ref[0, rows, h * dv:(h + 1) * dv]
            out = _dot(att, vh) + _dot_nt(q_in, states[h].astype(BF16))
            states[h] = states[h] * total + _dot_tn(vh, k_out)
            outs.append(out * lax.rsqrt(jnp.mean(out * out, axis=-1, keepdims=True) + LN_EPS) * ng_ref[...])
        gated.append((jnp.concatenate(outs, axis=-1) * gate[rows]).astype(BF16))
    for h in range(heads):
        st_ref[h] = states[h]
    y = _dot(jnp.concatenate(gated, axis=0), wo_ref[...])
    o_ref[0] = _layer_norm(DN_ALPHA * x_ref[0] + y, lg_ref[...], lb_ref[...])


def _gla_mixer_ln(x, w_in, w_gate2, b_gate, norm_g, w_out, ln_g, ln_b):
    b, s, d = x.shape
    c = GLA_CHUNK
    heads = GLA_HEADS
    dk = GLA_DK // heads
    dv = GLA_DV // heads
    x2 = x.reshape(b * s, d)
    w = w_in.astype(BF16)
    o0, o1, o2, o3 = GLA_DK, 2 * GLA_DK, 2 * GLA_DK + GLA_DV, 2 * GLA_DK + GLA_DV + GLA_GATE_RANK
    q, k, v, glr, r = _proj(
        x2, [w[:, :o0], w[:, o0:o1], w[:, o1:o2], _pad_cols(w[:, o2:o3], LANES), w[:, o3:]],
        [F32, F32, BF16, F32, F32], scales=(dk ** -0.5, 1.0, 1.0, 1.0, 1.0))
    wg2 = jnp.pad(w_gate2.astype(BF16), ((0, LANES - GLA_GATE_RANK), (0, 0)))
    dall, masks, nlev = _gla_tables(c)
    nck = GLA_STEP_CHUNKS
    tok = lambda width: pl.BlockSpec((1, nck * c, width), lambda i, j: (i, j, 0))
    return pl.pallas_call(
        functools.partial(_gla_body, nlev=nlev, nck=nck),
        grid=(b, s // (nck * c)),
        in_specs=[
            tok(GLA_DK), tok(GLA_DK), tok(GLA_DV), tok(LANES), tok(GLA_DV), tok(d),
            _full((LANES, GLA_DK)),
            _full((1, GLA_DK)),
            _full((1, dv)),
            _full(dall.shape),
            _full(masks.shape),
            _full((GLA_DV, d)),
            _full((1, d)),
            _full((1, d)),
        ],
        out_specs=tok(d),
        out_shape=jax.ShapeDtypeStruct((b, s, d), F32),
        scratch_shapes=[pltpu.VMEM((heads, dv, dk), F32)],
        compiler_params=_cparams("parallel", "arbitrary"),
        name="gla_core",
    )(q.reshape(b, s, -1), k.reshape(b, s, -1), v.reshape(b, s, -1), glr.reshape(b, s, -1),
      r.reshape(b, s, -1), x, wg2, b_gate.reshape(1, -1), norm_g.reshape(1, -1),
      jnp.asarray(dall, BF16), jnp.asarray(masks, F32), w_out.astype(BF16), ln_g.reshape(1, d), ln_b.reshape(1, d))


def _gelu_tanh(x):
    return 0.5 * x * (1.0 + jnp.tanh(np.sqrt(2.0 / np.pi).astype(np.float32) * (x + 0.044715 * (x * x * x))))


def _cmp_body(kc_ref, vc_ref, plo_ref, phi_ref, w1a_ref, w1b_ref, w2_ref, o_ref):
    nch, width = o_ref.shape[1], o_ref.shape[2]
    st = kc_ref.shape[1] // nch
    a = jnp.zeros((nch, width), F32)
    bm = jnp.zeros((nch, width), F32)
    for l in range(st):
        c = jnp.concatenate([kc_ref[0, pl.ds(l, nch, stride=st), :], vc_ref[0, pl.ds(l, nch, stride=st), :]], axis=1)
        ws = slice(l * width, (l + 1) * width)
        a = a + _dot((c + plo_ref[:, ws]).astype(BF16), w1a_ref[ws, :])
        bm = bm + _dot((c + phi_ref[:, ws]).astype(BF16), w1b_ref[ws, :])
    nxt = jnp.concatenate([bm[1:], jnp.zeros_like(bm[:1])], axis=0)
    row = lax.broadcasted_iota(jnp.int32, a.shape, 0)
    pre = jnp.where(row < a.shape[0] - 1, a + nxt, 0.0)
    o_ref[0] = _dot(_gelu_tanh(pre).astype(BF16), w2_ref[...]).astype(o_ref.dtype)


def _nsa_compress(kvc, cmp_pos, cmp_w1, cmp_w2):
    b, s, _ = kvc.shape
    half = LANES
    st, g, d = NSA_CMP_STRIDE, NSA_KV_GROUPS, HEAD_DIM
    nch = s // st
    width = 2 * g * d
    eye = jnp.eye(2 * g, dtype=F32)
    sel = jnp.repeat(jnp.eye(2, dtype=F32), g, axis=1)

    def expand(w1_half):
        wj = jnp.einsum('jlde,jc->lcde', w1_half, sel)
        return jnp.einsum('lcde,cf->lcdfe', wj, eye).reshape(st * width, width).astype(BF16)

    w1a = expand(cmp_w1[:, :st])
    w1b = expand(cmp_w1[:, st:])
    w2 = jnp.einsum('cde,cf->cdfe', jnp.einsum('jde,jc->cde', cmp_w2, sel), eye).reshape(width, width).astype(BF16)

    def pos_row(p_half):
        return jnp.einsum('jld,jc->lcd', p_half, sel).reshape(1, st * width)

    return pl.pallas_call(
        _cmp_body,
        grid=(b,),
        in_specs=[
            pl.BlockSpec((1, s, half), lambda i: (i, 0, 0)),
            pl.BlockSpec((1, s, half), lambda i: (i, 0, 1)),
            _full((1, st * width)),
            _full((1, st * width)),
            _full((st * width, width)),
            _full((st * width, width)),
            _full((width, width)),
        ],
        out_specs=pl.BlockSpec((1, nch, width), lambda i: (i, 0, 0)),
        out_shape=jax.ShapeDtypeStruct((b, nch, width), BF16),
        compiler_params=_cparams("parallel"),
        name="nsa_compress",
    )(kvc, kvc, pos_row(cmp_pos[:, :st]), pos_row(cmp_pos[:, st:]), w1a, w1b, w2)


def _nsa_overlap_t(s):
    n_cmp_pad = s // NSA_CMP_STRIDE
    n_sel = s // NSA_SEL_LEN
    blk = np.arange(n_cmp_pad) * NSA_CMP_STRIDE
    sel_start = np.arange(n_sel) * NSA_SEL_LEN
    ov = (blk[None, :] < sel_start[:, None] + NSA_SEL_LEN) & (blk[None, :] + NSA_CMP_LEN > sel_start[:, None])
    return ov.astype(np.float32)


NSA_MASK = 1.0e30


def _nsa_body(qt_ref, kc_ref, vct_ref, ks_ref, kw_ref, vt_ref, glt_ref, ovt_ref, o_ref, qt_scr, sa_scr, sb_scr,
              wa_scr, wb_scr, m_scr, acc_scr, *, tq):
    d = HEAD_DIM
    groups = NSA_KV_GROUPS
    rep = N_HEADS // groups
    n_cmp = kc_ref.shape[2]
    n_sel = ovt_ref.shape[0]
    t = pl.program_id(1)
    q0 = t * tq
    win_tiles = NSA_WINDOW // tq
    gates = jax.nn.sigmoid(glt_ref[...])

    tpos = (q0 + lax.broadcasted_iota(jnp.int32, (1, tq), 1)).astype(F32)
    brow = lax.broadcasted_iota(jnp.int32, (ATT_BIAS_ROWS, tq), 0)
    ones_rows = jnp.where(brow == 0, 1.0, 0.0).astype(BF16)
    ci = lax.broadcasted_iota(jnp.int32, (tq, tq), 0)
    ai = lax.broadcasted_iota(jnp.int32, (tq, tq), 1)
    tile8 = lambda m: jnp.concatenate([m] * rep, axis=1)
    diag_add = tile8(jnp.where(ci <= ai, 0.0, NEG_INF))
    part_add = tile8(jnp.where((ci > ai) & (t >= win_tiles), 0.0, NEG_INF))
    ncm = lax.broadcasted_iota(jnp.int32, (n_cmp, tq), 0)
    tcm = q0 + lax.broadcasted_iota(jnp.int32, (n_cmp, tq), 1)
    valid_c = ncm * NSA_CMP_STRIDE + (NSA_CMP_LEN - 1) <= tcm
    jm = lax.broadcasted_iota(jnp.int32, (n_sel, tq), 0)
    cur = (q0 + lax.broadcasted_iota(jnp.int32, (n_sel, tq), 1)) // NSA_SEL_LEN
    causal_sel = jm <= cur
    forced = (jm == 0) | (jm == cur) | (jm == cur - 1)
    pad_rows = jnp.zeros((ATT_QROWS - d - ATT_BIAS_ROWS - n_sel, tq), BF16)

    o_cmp = []
    for g in range(groups):
        heads = [(qt_ref[h * d:(h + 1) * d, :], _alibi_rows(h, tpos, brow)) for h in range(g * rep, (g + 1) * rep)]

        def build(mask_rows):
            cols = [jnp.concatenate([qh, bias, mask_rows, pad_rows], axis=0) for qh, bias in heads]
            return jnp.concatenate(cols, axis=1)

        qt_scr[2 * g] = build(jnp.zeros((n_sel, tq), BF16))

        s_c = _dot(kc_ref[0, g], qt_scr[2 * g])
        p_sum = jnp.zeros((n_cmp, tq), F32)
        ps = []
        for r in range(rep):
            sr = jnp.where(valid_c, s_c[:, r * tq:(r + 1) * tq], NEG_INF)
            m = jnp.max(sr, axis=0, keepdims=True)
            e = jnp.where(valid_c, jnp.exp2(sr - m), 0.0)
            den = jnp.sum(e, axis=0, keepdims=True)
            p = e * jnp.where(den > 0.0, 1.0 / den, 0.0)
            p_sum = p_sum + p
            ps.append(p.astype(BF16))
        o_cmp.append(_dot(vct_ref[0, g * d:(g + 1) * d, :], jnp.concatenate(ps, axis=1)))

        p_hi = p_sum.astype(BF16)
        p_lo = (p_sum - p_hi.astype(F32)).astype(BF16)
        ovt = ovt_ref[...]
        imp_t = _dot(ovt, p_hi) + _dot(ovt, p_lo)
        score = jnp.where(causal_sel, imp_t + jnp.where(forced, NSA_FORCE_BONUS, 0.0), NEG_INF)
        rank = jnp.zeros((n_sel, tq), F32)
        for mp in range(n_sel):
            row = score[mp:mp + 1, :]
            ahead = (row > score) | ((row == score) & (mp < jm))
            rank = rank + jnp.where(ahead, 1.0, 0.0)
        selected = (rank < NSA_TOPK) & causal_sel
        qt_scr[2 * g + 1] = build(jnp.where(selected, 0.0, -NSA_MASK).astype(BF16))

    def score(chains, kt, slot):
        k0 = pl.multiple_of(kt * tq, tq)
        for ci, (k_ref, g, _, qi, _) in enumerate(chains):
            slot[ci] = _dot(k_ref[0, g, pl.ds(k0, tq), :], qt_scr[qi])

    def absorb(chains, kt, add, slot):
        for ci, (_, _, v_row0, _, si) in enumerate(chains):
            s = slot[ci]
            if add is not None:
                s = s + add
            m_prev = m_scr[si]
            m_new = jnp.maximum(m_prev, jnp.max(s, axis=0, keepdims=True))
            e = jnp.exp2(s - m_new)
            alpha = jnp.exp2(m_prev - m_new)
            vals = jnp.concatenate([vt_ref[0, kt, v_row0:v_row0 + d, :], ones_rows], axis=0)
            acc_scr[si] = alpha * acc_scr[si] + _dot(vals, e.astype(BF16))
            m_scr[si] = m_new

    def sweep(chains, lo, sa, sb, after_loop=None):
        n = t - lo

        def pair(j, carry):
            a = lo + 2 * j
            score(chains, a + 1, sb)
            absorb(chains, a, None, sa)
            score(chains, a + 2, sa)
            absorb(chains, a + 1, None, sb)
            return carry

        lax.fori_loop(0, n // 2, pair, 0)
        if after_loop is not None:
            after_loop()

        @pl.when(n % 2 == 1)
        def _():
            score(chains, t, sb)
            absorb(chains, t - 1, None, sa)
            absorb(chains, t, diag_add, sb)

        @pl.when(n % 2 == 0)
        def _():
            absorb(chains, t, diag_add, sa)

    m_scr[...] = jnp.full(m_scr.shape, NEG_INF, F32)
    acc_scr[...] = jnp.zeros(acc_scr.shape, F32)

    sel_chains = [(ks_ref, g, g * d, 2 * g + 1, g) for g in range(groups)]
    win_chains = [(kw_ref, g, (groups + g) * d, 2 * g, groups + g) for g in range(groups)]
    oldest = jnp.maximum(t - win_tiles, 0)
    lo = jnp.maximum(t - win_tiles + 1, 0)

    def issue_window_scores():
        score(win_chains, oldest, wb_scr)
        score(win_chains, jnp.minimum(lo, t), wa_scr)

    score(sel_chains, 0, sa_scr)
    sweep(sel_chains, 0, sa_scr, sb_scr, after_loop=issue_window_scores)

    absorb(win_chains, oldest, part_add, wb_scr)
    sweep(win_chains, lo, wa_scr, wb_scr)

    o_sel = [acc_scr[g, :d] / acc_scr[g, d:d + 1] for g in range(groups)]
    o_win = [acc_scr[groups + g, :d] / acc_scr[groups + g, d:d + 1] for g in range(groups)]

    for g in range(groups):
        for r in range(rep):
            h = g * rep + r
            cs = slice(r * tq, (r + 1) * tq)
            o = (gates[h:h + 1] * o_cmp[g][:, cs] + gates[N_HEADS + h:N_HEADS + h + 1] * o_sel[g][:, cs]
                 + gates[2 * N_HEADS + h:2 * N_HEADS + h + 1] * o_win[g][:, cs])
            o_ref[h * d:(h + 1) * d, :] = o.astype(o_ref.dtype)


def _nsa_mixer_ln(x, w_in, cmp_pos, cmp_w1, cmp_w2, w_out, ln_g, ln_b):
    b, s, d = x.shape
    hd = N_HEADS * HEAD_DIM
    groups = NSA_KV_GROUPS
    gd = groups * HEAD_DIM
    n = b * s
    tq = ATT_TQ
    n_sel = s // NSA_SEL_LEN
    n_cmp = s // NSA_CMP_STRIDE
    assert HEAD_DIM + ATT_BIAS_ROWS + n_sel <= ATT_QROWS and n_sel % 16 == 0 and NSA_WINDOW % tq == 0
    x2 = x.reshape(n, d)
    w = w_in.astype(BF16)
    c0 = hd + 2 * gd
    w_k2 = jnp.concatenate([w[:, c0:c0 + gd], w[:, c0 + 2 * gd:c0 + 3 * gd]], axis=1)
    w_v2 = jnp.concatenate([w[:, c0 + gd:c0 + 2 * gd], w[:, c0 + 3 * gd:c0 + 4 * gd]], axis=1)
    tok = np.arange(s, dtype=np.float64)
    qt, kvc, k_sw, vt, glt = _proj_mixed(
        x2, [(w[:, :hd], "feat", BF16, HEAD_DIM ** -0.5 * LOG2E),
             (w[:, hd:c0], "tok", F32, 1.0),
             (w_k2, "keys", BF16, 1.0, _key_tail(tok, n_sel, NSA_SEL_LEN)),
             (w_v2, "feat_tiles", BF16, 1.0),
             (_pad_cols(w[:, c0 + 4 * gd:], LANES), "feat", F32, 1.0)], seq=s)

    cmp = _nsa_compress(kvc.reshape(b, s, 2 * gd), cmp_pos, cmp_w1, cmp_w2)
    cpos = np.arange(n_cmp, dtype=np.float64) * NSA_CMP_STRIDE + (NSA_CMP_LEN - 1) / 2
    k_cmp = _keys_with_tail(cmp[:, :, :gd], groups, _key_tail(cpos))
    v_cmp_t = cmp[:, :, gd:].transpose(0, 2, 1)
    vt = vt.reshape(b, s // tq, 2 * gd, tq)
    ovt = _nsa_overlap_t(s)
    nt = s // tq
    o_t = pl.pallas_call(
        functools.partial(_nsa_body, tq=tq),
        grid=(b, nt),
        in_specs=[
            pl.BlockSpec((hd, tq), lambda i, j: (0, i * nt + j)),
            pl.BlockSpec((1, groups, n_cmp, ATT_QROWS), lambda i, j: (i, 0, 0, 0)),
            pl.BlockSpec((1, gd, n_cmp), lambda i, j: (i, 0, 0)),
            pl.BlockSpec((1, groups, s, ATT_QROWS), lambda i, j: (i, 0, 0, 0)),
            pl.BlockSpec((1, groups, s, ATT_QROWS), lambda i, j: (i, 1, 0, 0)),
            pl.BlockSpec((1, nt, 2 * gd, tq), lambda i, j: (i, 0, 0, 0)),
            pl.BlockSpec((LANES, tq), lambda i, j: (0, i * nt + j)),
            _full(ovt.shape),
        ],
        out_specs=pl.BlockSpec((hd, tq), lambda i, j: (0, i * nt + j)),
        out_shape=jax.ShapeDtypeStruct((hd, n), BF16),
        scratch_shapes=[pltpu.VMEM((2 * groups, ATT_QROWS, (N_HEADS // groups) * tq), BF16),
                        pltpu.VMEM((groups, tq, (N_HEADS // groups) * tq), F32),
                        pltpu.VMEM((groups, tq, (N_HEADS // groups) * tq), F32),
                        pltpu.VMEM((groups, tq, (N_HEADS // groups) * tq), F32),
                        pltpu.VMEM((groups, tq, (N_HEADS // groups) * tq), F32),
                        pltpu.VMEM((2 * groups, 1, (N_HEADS // groups) * tq), F32),
                        pltpu.VMEM((2 * groups, HEAD_DIM + ATT_BIAS_ROWS, (N_HEADS // groups) * tq), F32)],
        compiler_params=_cparams("parallel", "parallel"),
        name="nsa_core",
    )(qt, k_cmp, v_cmp_t, k_sw, k_sw, vt, glt, jnp.asarray(ovt, BF16))
    return _oproj_ln(o_t, w_out, x2, ln_g, ln_b, feature_major=True).reshape(b, s, d)


def kernel(x, mem, ln_g, ln_b, ffn1_w_gate, ffn1_w_up, ffn1_w_down, ffn2_w_gate, ffn2_w_up, ffn2_w_down,
           xattn_w_q, xattn_w_kv, xattn_w_o, swa_w_in, swa_sinks, swa_w_out, nsa_w_in, nsa_cmp_pos,
           nsa_cmp_w1, nsa_cmp_w2, nsa_w_out, gla_w_in, gla_w_gate2, gla_b_gate, gla_norm_g, gla_w_out,
           pool_w, pool_scale):
    b, s, d = x.shape
    ffn1 = [w.astype(BF16) for w in (ffn1_w_gate, ffn1_w_up, ffn1_w_down)]
    ffn2 = [w.astype(BF16) for w in (ffn2_w_gate, ffn2_w_up, ffn2_w_down)]
    xq, xkv, xo = (w.astype(BF16) for w in (xattn_w_q, xattn_w_kv, xattn_w_o))
    for i in range(DEPTH):
        kind = i % N_MIXERS
        j = i // N_MIXERS
        x = _ffn_ln(x.reshape(b * s, d), *ffn1, i, ln_g[i, 0], ln_b[i, 0]).reshape(b, s, d)
        if kind == 0:
            x = _swa_mixer_ln(x, swa_w_in[j], swa_sinks[j], swa_w_out[j], ln_g[i, 1], ln_b[i, 1])
        elif kind == 1:
            x = _nsa_mixer_ln(x, nsa_w_in[j], nsa_cmp_pos[j], nsa_cmp_w1[j], nsa_cmp_w2[j], nsa_w_out[j],
                              ln_g[i, 1], ln_b[i, 1])
        elif kind == 2:
            x = _gla_mixer_ln(x, gla_w_in[j], gla_w_gate2[j], gla_b_gate[j], gla_norm_g[j], gla_w_out[j],
                              ln_g[i, 1], ln_b[i, 1])
        else:
            x = _pool_mixer_ln(x, pool_w[j], pool_scale[j], ln_g[i, 1], ln_b[i, 1])
        x = _xattn_ln(x, mem, xq, xkv[i], xo, i, ln_g[i, 2], ln_b[i, 2])
        x = _ffn_ln(x.reshape(b * s, d), *ffn2, i, ln_g[i, 3], ln_b[i, 3]).reshape(b, s, d)
    return x
```
